```python
import jax, jax.numpy as jnp
from jax import lax
import numpy as np

D_MODEL = 1024
BATCH = 4
SEQ = 8192
DEPTH = 1

CHUNK = 64
N_LEFT_CHUNKS = 8
BAND = (N_LEFT_CHUNKS + 1) * CHUNK
MIX_WIDTH = D_MODEL
ATTN_WIDTH = MIX_WIDTH // 2
HEAD_DIM = 64
N_HEADS = ATTN_WIDTH // HEAD_DIM
REL_CLIP = 256
POOL_WIDTH = MIX_WIDTH - ATTN_WIDTH
POOL_WINDOWS = (2, 4, 8, 16)
N_POOL_GROUPS = len(POOL_WINDOWS)
POOL_GROUP = POOL_WIDTH // N_POOL_GROUPS
IN_WIDTH = 3 * ATTN_WIDTH + POOL_WIDTH + MIX_WIDTH
EPS = 1e-6
NEG_INF = -1e30

kernel_name = "hybrid_chunk_attn_pool_block"


def rms_norm(x, g):
    xf = x.astype(jnp.float32)
    y = xf * lax.rsqrt(jnp.mean(xf * xf, axis=-1, keepdims=True) + EPS)
    return (y * g.astype(jnp.float32)).astype(x.dtype)


def chunk_attention(q, k, v, rel_bias):
    B, S, H, Dh = q.shape
    n_chunks = S // CHUNK
    pad = N_LEFT_CHUNKS * CHUNK
    kp = jnp.pad(k, ((0, 0), (pad, 0), (0, 0), (0, 0)))
    vp = jnp.pad(v, ((0, 0), (pad, 0), (0, 0), (0, 0)))
    a = jnp.arange(CHUNK)[:, None]
    j = jnp.arange(BAND)[None, :]
    rel = j - pad - a
    idx = jnp.clip(rel, -REL_CLIP, REL_CLIP) + REL_CLIP
    bias = rel_bias.astype(jnp.float32)[:, idx]
    scale = HEAD_DIM ** -0.5
    band_offsets = jnp.arange(BAND) - pad

    def one_chunk(n):
        start = n * CHUNK
        qn = lax.dynamic_slice_in_dim(q, start, CHUNK, axis=1)
        kn = lax.dynamic_slice_in_dim(kp, start, BAND, axis=1)
        vn = lax.dynamic_slice_in_dim(vp, start, BAND, axis=1)
        s = jnp.einsum('bqhd,bkhd->bhqk', qn, kn).astype(jnp.float32) * scale + bias[None]
        valid = (start + band_offsets) >= 0
        s = jnp.where(valid[None, None, None, :], s, NEG_INF)
        p = jax.nn.softmax(s, axis=-1).astype(vn.dtype)
        return jnp.einsum('bhqk,bkhd->bqhd', p, vn)

    out = lax.map(one_chunk, jnp.arange(n_chunks))
    return jnp.transpose(out, (1, 0, 2, 3, 4)).reshape(B, S, H * Dh)


def multiscale_pool(u, w_pool, pool_scale):
    B, S, _ = u.shape
    uf = u.astype(jnp.float32).reshape(B, S, N_POOL_GROUPS, POOL_GROUP)
    cs = jnp.concatenate([jnp.zeros((B, 1, N_POOL_GROUPS, POOL_GROUP), jnp.float32),
                          jnp.cumsum(uf, axis=1)], axis=1)
    t = jnp.arange(S)[:, None]
    win = jnp.array(POOL_WINDOWS, dtype=jnp.int32)[None, :]
    lo = jnp.maximum(t + 1 - win, 0)
    cnt = jnp.minimum(t + 1, win).astype(jnp.float32)
    g_idx = jnp.arange(N_POOL_GROUPS)[None, :]
    window_sum = cs[:, 1:] - cs[:, lo, g_idx]
    mixed = window_sum / cnt[None, :, :, None] - uf
    y = jnp.einsum('bsgi,gio->bsgo', mixed.astype(u.dtype), w_pool)
    return y.reshape(B, S, POOL_WIDTH) * pool_scale


def setup_inputs(seed: int = 0) -> dict:
    key = jax.random.key(seed)
    ks = jax.random.split(key, 12)
    f32 = jnp.float32
    x = jax.random.normal(ks[0], (BATCH, SEQ, D_MODEL), f32)
    c = jax.random.normal(ks[1], (BATCH, D_MODEL), f32)
    norm_g = 1.0 + 0.02 * jax.random.normal(ks[2], (DEPTH, D_MODEL), f32)
    w_ada = 0.5 * D_MODEL ** -0.5 * jax.random.normal(ks[3], (DEPTH, D_MODEL, 3 * D_MODEL), f32)
    b_ada = 0.02 * jax.random.normal(ks[4], (DEPTH, 3 * D_MODEL), f32)
    w_in = D_MODEL ** -0.5 * jax.random.normal(ks[5], (DEPTH, D_MODEL, IN_WIDTH), f32)
    q_norm_g = 1.0 + 0.02 * jax.random.normal(ks[6], (DEPTH, HEAD_DIM), f32)
    k_norm_g = 1.0 + 0.02 * jax.random.normal(ks[7], (DEPTH, HEAD_DIM), f32)
    rel_bias = 0.5 * jax.random.normal(ks[8], (DEPTH, N_HEADS, 2 * REL_CLIP + 1), f32)
    w_pool = POOL_GROUP ** -0.5 * jax.random.normal(ks[9], (DEPTH, N_POOL_GROUPS, POOL_GROUP, POOL_GROUP), f32)
    pool_scale = 1.0 + 0.02 * jax.random.normal(ks[10], (DEPTH, POOL_WIDTH), f32)
    w_out = MIX_WIDTH ** -0.5 * jax.random.normal(ks[11], (DEPTH, MIX_WIDTH, D_MODEL), f32)
    return {"x": x, "c": c, "norm_g": norm_g, "w_ada": w_ada, "b_ada": b_ada,
            "w_in": w_in, "q_norm_g": q_norm_g, "k_norm_g": k_norm_g,
            "rel_bias": rel_bias, "w_pool": w_pool, "pool_scale": pool_scale,
            "w_out": w_out}


def reference(x, c, norm_g, w_ada, b_ada, w_in, q_norm_g, k_norm_g, rel_bias,
              w_pool, pool_scale, w_out):
    B, S, _ = x.shape
    for l in range(DEPTH):
        mod = c @ w_ada[l] + b_ada[l]
        shift, scale, gate = jnp.split(mod, 3, axis=-1)
        h = rms_norm(x, norm_g[l]) * (1.0 + scale[:, None, :]) + shift[:, None, :]
        proj = h @ w_in[l]
        o1 = ATTN_WIDTH
        o2 = 2 * ATTN_WIDTH
        o3 = 3 * ATTN_WIDTH
        o4 = o3 + POOL_WIDTH
        q = rms_norm(proj[..., :o1].reshape(B, S, N_HEADS, HEAD_DIM), q_norm_g[l])
        k = rms_norm(proj[..., o1:o2].reshape(B, S, N_HEADS, HEAD_DIM), k_norm_g[l])
        v = proj[..., o2:o3].reshape(B, S, N_HEADS, HEAD_DIM)
        u = proj[..., o3:o4]
        z = proj[..., o4:]
        a_out = chunk_attention(q, k, v, rel_bias[l])
        p_out = multiscale_pool(u, w_pool[l], pool_scale[l])
        y = jnp.concatenate([a_out, p_out], axis=-1) * jax.nn.silu(z)
        x = x + gate[:, None, :] * (y @ w_out[l])
    return x
```

```python
import functools

import jax
import jax.numpy as jnp
from jax import lax
from jax.experimental import pallas as pl
from jax.experimental.pallas import tpu as pltpu

CHUNK = 64
N_LEFT_CHUNKS = 8
HEAD_DIM = 64
REL_CLIP = 256
POOL_WINDOWS = (2, 4, 8, 16)
EPS = 1e-6
NEG_INF = -1e30

TILE = 512
QBLK = 2 * CHUNK
BANDW = TILE + QBLK
HEADS_PER_GROUP = 4
GROUP_W = HEADS_PER_GROUP * HEAD_DIM
POOL_PAD = 8
POOL_HIST = 16
POOL_OFF = POOL_PAD + POOL_HIST
SOFTMAX_ROWS = 32
V7X_VMEM_LIMIT_BYTES = 56 * 1024 * 1024

F32 = jnp.float32
BF16 = jnp.bfloat16


def _mod_kernel(c_ref, w_ref, b_ref, o_ref):
    o_ref[...] = jnp.dot(c_ref[...], w_ref[...], preferred_element_type=F32) + b_ref[...]


def _adaln_mod(c, w_ada, b_ada):
    bsz, d = c.shape
    n = w_ada.shape[1]
    rows = 8 * ((bsz + 7) // 8)
    c_pad = jnp.zeros((rows, d), F32).at[:bsz].set(c)
    out = pl.pallas_call(
        _mod_kernel,
        grid=(n // d,),
        in_specs=[
            pl.BlockSpec((rows, d), lambda i: (0, 0)),
            pl.BlockSpec((d, d), lambda i: (0, i)),
            pl.BlockSpec((1, d), lambda i: (0, i)),
        ],
        out_specs=pl.BlockSpec((rows, d), lambda i: (0, i)),
        out_shape=jax.ShapeDtypeStruct((rows, n), F32),
        name="adaln_mod",
    )(c_pad, w_ada, b_ada.reshape(1, n))
    return out[:bsz]


def _silu(z):
    return z * (1.0 / (1.0 + jnp.exp(-z)))


def _block_kernel(x_ref, mod_ref, g_ref, win_ref, wout_ref, gq_ref, gk_ref, bias_ref,
                  ones_ref, wp_ref, ps_ref, o_ref,
                  h_s, proj_s, qbd_s, k_s, v_s, u_s, t2_s, t4_s, t8_s, s_s, p_s, linv_s, y_s,
                  *, d_model, attn_w, pool_w):
    t = pl.program_id(1)
    n_groups = attn_w // GROUP_W
    o_k, o_v, o_u, o_z = attn_w, 2 * attn_w, 3 * attn_w, 3 * attn_w + pool_w
    pool_group = pool_w // len(POOL_WINDOWS)
    pool_rows = TILE + POOL_OFF

    @pl.when(t == 0)
    def _():
        k_s[0:TILE, :] = jnp.zeros((TILE, attn_w), BF16)
        v_s[0:TILE, :] = jnp.zeros((TILE, attn_w), BF16)
        u_s[0:POOL_OFF, :] = jnp.zeros((POOL_OFF, pool_w), F32)
        t2_s[0:POOL_PAD, :] = jnp.zeros((POOL_PAD, pool_w), F32)
        t4_s[0:POOL_PAD, :] = jnp.zeros((POOL_PAD, pool_w), F32)
        t8_s[0:POOL_PAD, :] = jnp.zeros((POOL_PAD, pool_w), F32)

    shift = mod_ref[:, 0:d_model]
    scale = mod_ref[:, d_model:2 * d_model]
    gate = mod_ref[:, 2 * d_model:3 * d_model]
    a_row = g_ref[...] * (1.0 + scale)

    for r in range(0, TILE, 64):
        xc = x_ref[r:r + 64, :]
        ms = jnp.mean(xc * xc, axis=-1, keepdims=True)
        hc = xc * lax.rsqrt(ms + EPS) * a_row + shift
        h_s[r:r + 64, :] = hc.astype(BF16)

    proj_s[...] = jnp.dot(h_s[...], win_ref[...], preferred_element_type=F32)

    lane_g = lax.broadcasted_iota(jnp.int32, (QBLK, GROUP_W), 1)
    for r in range(0, TILE, QBLK):
        rows = slice(r, r + QBLK)
        for g in range(n_groups):
            cols = slice(g * GROUP_W, (g + 1) * GROUP_W)
            qf = proj_s[rows, cols]
            ssq = jnp.dot((qf * qf).astype(BF16), ones_ref[...], preferred_element_type=F32)
            qn = qf * lax.rsqrt(ssq * (1.0 / HEAD_DIM) + EPS) * (gq_ref[:, cols] * HEAD_DIM ** -0.5)
            for h in range(HEADS_PER_GROUP):
                keep = (lane_g >= h * HEAD_DIM) & (lane_g < (h + 1) * HEAD_DIM)
                qbd_s[r // QBLK, g, h * QBLK:(h + 1) * QBLK, :] = jnp.where(keep, qn, 0.0).astype(BF16)
            kf = proj_s[rows, o_k + g * GROUP_W:o_k + (g + 1) * GROUP_W]
            ssk = jnp.dot((kf * kf).astype(BF16), ones_ref[...], preferred_element_type=F32)
            kn = kf * lax.rsqrt(ssk * (1.0 / HEAD_DIM) + EPS) * gk_ref[:, cols]
            k_s[TILE + r:TILE + r + QBLK, cols] = kn.astype(BF16)
        v_s[TILE + r:TILE + r + QBLK, :] = proj_s[rows, o_v:o_v + attn_w].astype(BF16)
        u_s[POOL_OFF + r:POOL_OFF + r + QBLK, :] = proj_s[rows, o_u:o_u + pool_w]

    lane_q = lax.broadcasted_iota(jnp.int32, (QBLK, 128), 1)
    band_col = lax.broadcasted_iota(jnp.int32, (1, BANDW), 1)

    def attn_block(j, carry):
        qrow = pl.multiple_of(j * QBLK, QBLK)
        key_pos = t * TILE + qrow - TILE + band_col
        start_mask = jnp.where(key_pos >= 0, 0.0, NEG_INF).astype(F32)
        for g in range(n_groups):
            cols = slice(g * GROUP_W, (g + 1) * GROUP_W)
            kb = k_s[pl.ds(qrow, BANDW), cols]
            s_s[g] = lax.dot_general(qbd_s[j, g], kb, (((1,), (1,)), ((), ())),
                                     preferred_element_type=F32)
        for g in range(n_groups):
            for h in range(HEADS_PER_GROUP):
                for rc in range(0, QBLK, SOFTMAX_ROWS):
                    rr = slice(h * QBLK + rc, h * QBLK + rc + SOFTMAX_ROWS)
                    s = s_s[g, rr, :] + bias_ref[g * HEADS_PER_GROUP + h, rc:rc + SOFTMAX_ROWS, :]
                    s = s + start_mask
                    m = jnp.max(s, axis=-1, keepdims=True)
                    p = jnp.exp(s - m)
                    l = jnp.sum(p, axis=-1, keepdims=True)
                    p_s[g, rr, :] = p.astype(BF16)
                    linv_s[g, rr, :] = jnp.broadcast_to(1.0 / l, (SOFTMAX_ROWS, 128))
        for g in range(n_groups):
            cols = slice(g * GROUP_W, (g + 1) * GROUP_W)
            vb = v_s[pl.ds(qrow, BANDW), cols]
            o4 = jnp.dot(p_s[g], vb, preferred_element_type=F32)
            for c in range(GROUP_W // 128):
                ha, hb = 2 * c, 2 * c + 1
                oa = o4[ha * QBLK:(ha + 1) * QBLK, c * 128:(c + 1) * 128] * linv_s[g, ha * QBLK:(ha + 1) * QBLK, :]
                ob = o4[hb * QBLK:(hb + 1) * QBLK, c * 128:(c + 1) * 128] * linv_s[g, hb * QBLK:(hb + 1) * QBLK, :]
                a = jnp.where(lane_q < HEAD_DIM, oa, ob)
                c0 = g * GROUP_W + c * 128
                z = proj_s[pl.ds(qrow, QBLK), o_z + c0:o_z + c0 + 128]
                y_s[pl.ds(qrow, QBLK), c0:c0 + 128] = (a * _silu(z)).astype(BF16)
        return carry

    lax.fori_loop(0, TILE // QBLK, attn_block, 0)

    step = 88
    levels = ((u_s, t2_s, 1), (t2_s, t4_s, 2), (t4_s, t8_s, 4), (t8_s, None, 8))
    for gi in range(len(POOL_WINDOWS)):
        assert POOL_WINDOWS[gi] == 2 ** (gi + 1)
    for li, (src, dst, sh) in enumerate(levels[:-1]):
        lanes = slice((li + 1) * pool_group, pool_w)
        for r in range(POOL_PAD, pool_rows, step):
            dst[r:r + step, lanes] = src[r:r + step, lanes] + src[r - sh:r - sh + step, lanes]
    tok = t * TILE + lax.broadcasted_iota(jnp.int32, (QBLK, pool_group), 0)
    for r in range(0, TILE, QBLK):
        for pair in range(len(POOL_WINDOWS) // 2):
            mixed = []
            for gi in (2 * pair, 2 * pair + 1):
                lanes = slice(gi * pool_group, (gi + 1) * pool_group)
                src = levels[gi][0]
                sh = levels[gi][2]
                rows = slice(POOL_OFF + r, POOL_OFF + r + QBLK)
                wsum = src[rows, lanes] + src[POOL_OFF + r - sh:POOL_OFF + r - sh + QBLK, lanes]
                cnt = jnp.minimum(tok + (r + 1), POOL_WINDOWS[gi]).astype(F32)
                mixed.append((wsum / cnt - u_s[rows, lanes]).astype(BF16))
            pm = jnp.concatenate(mixed, axis=1)
            c0 = pair * 2 * pool_group
            po = jnp.dot(pm, wp_ref[pair], preferred_element_type=F32) * ps_ref[:, c0:c0 + 2 * pool_group]
            z = proj_s[r:r + QBLK, o_z + attn_w + c0:o_z + attn_w + c0 + 2 * pool_group]
            y_s[r:r + QBLK, attn_w + c0:attn_w + c0 + 2 * pool_group] = (po * _silu(z)).astype(BF16)

    for r in range(0, TILE, 256):
        o = jnp.dot(y_s[r:r + 256, :], wout_ref[...], preferred_element_type=F32)
        o_ref[r:r + 256, :] = x_ref[r:r + 256, :] + gate * o

    k_s[0:TILE, :] = k_s[TILE:2 * TILE, :]
    v_s[0:TILE, :] = v_s[TILE:2 * TILE, :]
    u_s[POOL_PAD:POOL_OFF, :] = u_s[TILE + POOL_PAD:TILE + POOL_OFF, :]


def _bias_table(rel_bias):
    a = jnp.arange(QBLK)[:, None]
    j = jnp.arange(BANDW)[None, :]
    rel = j - TILE - a
    idx = jnp.clip(rel, -REL_CLIP, REL_CLIP) + REL_CLIP
    tbl = rel_bias.astype(F32)[:, idx]
    band_lo = (a // CHUNK) * CHUNK
    visible = (j >= band_lo) & (j < band_lo + (N_LEFT_CHUNKS + 1) * CHUNK)
    return jnp.where(visible[None], tbl, NEG_INF)


def _layer(x, mod, norm_g, w_in, q_norm_g, k_norm_g, rel_bias, w_pool, pool_scale, w_out):
    bsz, seq, d_model = x.shape
    n_heads = rel_bias.shape[0]
    attn_w = n_heads * HEAD_DIM
    pool_w = w_pool.shape[0] * w_pool.shape[1]
    pool_group = w_pool.shape[1]
    in_w = w_in.shape[1]
    assert seq % TILE == 0 and attn_w % GROUP_W == 0 and TILE == N_LEFT_CHUNKS * CHUNK
    assert in_w == 3 * attn_w + pool_w + attn_w + pool_w and 2 * pool_group == GROUP_W
    n_groups = attn_w // GROUP_W

    ones_bd = jnp.kron(jnp.eye(HEADS_PER_GROUP, dtype=F32), jnp.ones((HEAD_DIM, HEAD_DIM), F32)).astype(BF16)
    n_pairs = w_pool.shape[0] // 2
    wp = w_pool.reshape(n_pairs, 2, pool_group, pool_group)
    zero = jnp.zeros((n_pairs, pool_group, pool_group), F32)
    wp_bd = jnp.concatenate([jnp.concatenate([wp[:, 0], zero], axis=2),
                             jnp.concatenate([zero, wp[:, 1]], axis=2)], axis=1).astype(BF16)

    const2 = lambda b, t: (0, 0)
    const3 = lambda b, t: (0, 0, 0)
    kernel = functools.partial(_block_kernel, d_model=d_model, attn_w=attn_w, pool_w=pool_w)
    return pl.pallas_call(
        kernel,
        grid=(bsz, seq // TILE),
        in_specs=[
            pl.BlockSpec((None, TILE, d_model), lambda b, t: (b, t, 0)),
            pl.BlockSpec((None, 1, 3 * d_model), lambda b, t: (b, 0, 0)),
            pl.BlockSpec((1, d_model), const2),
            pl.BlockSpec((d_model, in_w), const2),
            pl.BlockSpec((attn_w + pool_w, d_model), const2),
            pl.BlockSpec((1, attn_w), const2),
            pl.BlockSpec((1, attn_w), const2),
            pl.BlockSpec((n_heads, QBLK, BANDW), const3),
            pl.BlockSpec((GROUP_W, GROUP_W), const2),
            pl.BlockSpec((n_pairs, GROUP_W, GROUP_W), const3),
            pl.BlockSpec((1, pool_w), const2),
        ],
        out_specs=pl.BlockSpec((None, TILE, d_model), lambda b, t: (b, t, 0)),
        out_shape=jax.ShapeDtypeStruct(x.shape, x.dtype),
        scratch_shapes=[
            pltpu.VMEM((TILE, d_model), BF16),
            pltpu.VMEM((TILE, in_w), F32),
            pltpu.VMEM((TILE // QBLK, n_groups, HEADS_PER_GROUP * QBLK, GROUP_W), BF16),
            pltpu.VMEM((2 * TILE, attn_w), BF16),
            pltpu.VMEM((2 * TILE, attn_w), BF16),
            pltpu.VMEM((TILE + POOL_OFF, pool_w), F32),
            pltpu.VMEM((TILE + POOL_OFF, pool_w), F32),
            pltpu.VMEM((TILE + POOL_OFF, pool_w), F32),
            pltpu.VMEM((TILE + POOL_OFF, pool_w), F32),
            pltpu.VMEM((n_groups, HEADS_PER_GROUP * QBLK, BANDW), F32),
            pltpu.VMEM((n_groups, HEADS_PER_GROUP * QBLK, BANDW), BF16),
            pltpu.VMEM((n_groups, HEADS_PER_GROUP * QBLK, 128), F32),
            pltpu.VMEM((TILE, attn_w + pool_w), BF16),
        ],
        compiler_params=pltpu.CompilerParams(
            dimension_semantics=("arbitrary", "arbitrary"),
            vmem_limit_bytes=V7X_VMEM_LIMIT_BYTES,
        ),
        name="hybrid_block",
    )(x, mod.reshape(bsz, 1, 3 * d_model), norm_g.reshape(1, d_model), w_in.astype(BF16),
      w_out.astype(BF16), jnp.tile(q_norm_g, n_heads).reshape(1, attn_w),
      jnp.tile(k_norm_g, n_heads).reshape(1, attn_w), _bias_table(rel_bias), ones_bd, wp_bd,
      pool_scale.reshape(1, pool_w))


def kernel(x, c, norm_g, w_ada, b_ada, w_in, q_norm_g, k_norm_g, rel_bias, w_pool, pool_scale, w_out):
    depth = w_in.shape[0]
    for l in range(depth):
        mod = _adaln_mod(c, w_ada[l], b_ada[l])
        x = _layer(x, mod, norm_g[l], w_in[l], q_norm_g[l], k_norm_g[l], rel_bias[l],
                   w_pool[l], pool_scale[l], w_out[l])
    return x
```

```python
import functools

import jax
import jax.numpy as jnp
from jax import lax
from jax.experimental import pallas as pl
from jax.experimental.pallas import tpu as pltpu

CHUNK = 64
N_LEFT_CHUNKS = 8
HEAD_DIM = 64
REL_CLIP = 256
POOL_WINDOWS = (2, 4, 8, 16)
EPS = 1e-6
NEG_INF = -1e30

TILE = 512
QBLK = 2 * CHUNK
BANDW = TILE + QBLK
BIAS_EXT = BANDW + QBLK
HEADS_PER_GROUP = 4
GROUP_W = HEADS_PER_GROUP * HEAD_DIM
POOL_PAD = 8
POOL_HIST = 16
POOL_OFF = POOL_PAD + POOL_HIST
SOFTMAX_ROWS = 32
V7X_VMEM_LIMIT_BYTES = 56 * 1024 * 1024

F32 = jnp.float32
BF16 = jnp.bfloat16


def _mod_kernel(c_ref, w_ref, b_ref, o_ref):
    o_ref[...] = jnp.dot(c_ref[...], w_ref[...], preferred_element_type=F32) + b_ref[...]


def _adaln_mod(c, w_ada, b_ada):
    bsz, d = c.shape
    n = w_ada.shape[1]
    rows = 8 * ((bsz + 7) // 8)
    c_pad = jnp.zeros((rows, d), F32).at[:bsz].set(c)
    out = pl.pallas_call(
        _mod_kernel,
        grid=(n // d,),
        in_specs=[
            pl.BlockSpec((rows, d), lambda i: (0, 0)),
            pl.BlockSpec((d, d), lambda i: (0, i)),
            pl.BlockSpec((1, d), lambda i: (0, i)),
        ],
        out_specs=pl.BlockSpec((rows, d), lambda i: (0, i)),
        out_shape=jax.ShapeDtypeStruct((rows, n), F32),
        name="adaln_mod",
    )(c_pad, w_ada, b_ada.reshape(1, n))
    return out[:bsz]


def _silu(z):
    return z * (1.0 / (1.0 + jnp.exp(-z)))


def _block_kernel(x_ref, mod_ref, g_ref, win_ref, wout_ref, gq_ref, gk_ref, rbe_ref,
                  ones_ref, wp_ref, ps_ref, o_ref,
                  h_s, proj_s, qbd_s, k_s, v_s, u_s, t2_s, t4_s, t8_s, s_s, p_s, linv_s, y_s, bias_ref,
                  *, d_model, attn_w, pool_w):
    t = pl.program_id(1)

    @pl.when((pl.program_id(0) == 0) & (t == 0))
    def _():
        a = lax.broadcasted_iota(jnp.int32, (QBLK, BANDW), 0)
        j = lax.broadcasted_iota(jnp.int32, (QBLK, BANDW), 1)
        band_lo = jnp.where(a >= CHUNK, CHUNK, 0)
        visible = (j >= band_lo) & (j < band_lo + (N_LEFT_CHUNKS + 1) * CHUNK)
        for h in range(bias_ref.shape[0]):
            row = jnp.broadcast_to(rbe_ref[h:h + 1, :], (QBLK, BIAS_EXT))
            skew = pltpu.roll(row, 0, 1, stride=1, stride_axis=0)
            bias_ref[h] = jnp.where(visible, skew[:, 0:BANDW], NEG_INF)
    n_groups = attn_w // GROUP_W
    o_k, o_v, o_u, o_z = attn_w, 2 * attn_w, 3 * attn_w, 3 * attn_w + pool_w
    pool_group = pool_w // len(POOL_WINDOWS)
    pool_rows = TILE + POOL_OFF

    @pl.when(t == 0)
    def _():
        k_s[0:TILE, :] = jnp.zeros((TILE, attn_w), BF16)
        v_s[0:TILE, :] = jnp.zeros((TILE, attn_w), BF16)
        u_s[0:POOL_OFF, :] = jnp.zeros((POOL_OFF, pool_w), F32)
        t2_s[0:POOL_PAD, :] = jnp.zeros((POOL_PAD, pool_w), F32)
        t4_s[0:POOL_PAD, :] = jnp.zeros((POOL_PAD, pool_w), F32)
        t8_s[0:POOL_PAD, :] = jnp.zeros((POOL_PAD, pool_w), F32)

    shift = mod_ref[:, 0:d_model]
    scale = mod_ref[:, d_model:2 * d_model]
    gate = mod_ref[:, 2 * d_model:3 * d_model]
    a_row = g_ref[...] * (1.0 + scale)

    for r in range(0, TILE, 64):
        xc = x_ref[r:r + 64, :]
        ms = jnp.mean(xc * xc, axis=-1, keepdims=True)
        hc = xc * lax.rsqrt(ms + EPS) * a_row + shift
        h_s[r:r + 64, :] = hc.astype(BF16)

    proj_s[...] = jnp.dot(h_s[...], win_ref[...], preferred_element_type=F32)

    lane_g = lax.broadcasted_iota(jnp.int32, (QBLK, GROUP_W), 1)
    for r in range(0, TILE, QBLK):
        rows = slice(r, r + QBLK)
        for g in range(n_groups):
            cols = slice(g * GROUP_W, (g + 1) * GROUP_W)
            qf = proj_s[rows, cols]
            ssq = jnp.dot((qf * qf).astype(BF16), ones_ref[...], preferred_element_type=F32)
            qn = qf * lax.rsqrt(ssq * (1.0 / HEAD_DIM) + EPS) * (gq_ref[:, cols] * HEAD_DIM ** -0.5)
            for h in range(HEADS_PER_GROUP):
                keep = (lane_g >= h * HEAD_DIM) & (lane_g < (h + 1) * HEAD_DIM)
                qbd_s[r // QBLK, g, h * QBLK:(h + 1) * QBLK, :] = jnp.where(keep, qn, 0.0).astype(BF16)
            kf = proj_s[rows, o_k + g * GROUP_W:o_k + (g + 1) * GROUP_W]
            ssk = jnp.dot((kf * kf).astype(BF16), ones_ref[...], preferred_element_type=F32)
            kn = kf * lax.rsqrt(ssk * (1.0 / HEAD_DIM) + EPS) * gk_ref[:, cols]
            k_s[TILE + r:TILE + r + QBLK, cols] = kn.astype(BF16)
        v_s[TILE + r:TILE + r + QBLK, :] = proj_s[rows, o_v:o_v + attn_w].astype(BF16)
        u_s[POOL_OFF + r:POOL_OFF + r + QBLK, :] = proj_s[rows, o_u:o_u + pool_w]

    lane_q = lax.broadcasted_iota(jnp.int32, (QBLK, 128), 1)
    band_col = lax.broadcasted_iota(jnp.int32, (1, BANDW), 1)

    def attn_block(j, carry):
        qrow = pl.multiple_of(j * QBLK, QBLK)
        key_pos = t * TILE + qrow - TILE + band_col
        start_mask = jnp.where(key_pos >= 0, 0.0, NEG_INF).astype(F32)
        for g in range(n_groups):
            cols = slice(g * GROUP_W, (g + 1) * GROUP_W)
            kb = k_s[pl.ds(qrow, BANDW), cols]
            s_s[g] = lax.dot_general(qbd_s[j, g], kb, (((1,), (1,)), ((), ())),
                                     preferred_element_type=F32)
        for g in range(n_groups):
            for h in range(HEADS_PER_GROUP):
                for rc in range(0, QBLK, SOFTMAX_ROWS):
                    rr = slice(h * QBLK + rc, h * QBLK + rc + SOFTMAX_ROWS)
                    s = s_s[g, rr, :] + bias_ref[g * HEADS_PER_GROUP + h, rc:rc + SOFTMAX_ROWS, :]
                    s = s + start_mask
                    m = jnp.max(s, axis=-1, keepdims=True)
                    p = jnp.exp(s - m)
                    l = jnp.sum(p, axis=-1, keepdims=True)
                    p_s[g, rr, :] = p.astype(BF16)
                    linv_s[g, rr, :] = jnp.broadcast_to(1.0 / l, (SOFTMAX_ROWS, 128))
        for g in range(n_groups):
            cols = slice(g * GROUP_W, (g + 1) * GROUP_W)
            vb = v_s[pl.ds(qrow, BANDW), cols]
            o4 = jnp.dot(p_s[g], vb, preferred_element_type=F32)
            for c in range(GROUP_W // 128):
                ha, hb = 2 * c, 2 * c + 1
                oa = o4[ha * QBLK:(ha + 1) * QBLK, c * 128:(c + 1) * 128] * linv_s[g, ha * QBLK:(ha + 1) * QBLK, :]
                ob = o4[hb * QBLK:(hb + 1) * QBLK, c * 128:(c + 1) * 128] * linv_s[g, hb * QBLK:(hb + 1) * QBLK, :]
                a = jnp.where(lane_q < HEAD_DIM, oa, ob)
                c0 = g * GROUP_W + c * 128
                z = proj_s[pl.ds(qrow, QBLK), o_z + c0:o_z + c0 + 128]
                y_s[pl.ds(qrow, QBLK), c0:c0 + 128] = (a * _silu(z)).astype(BF16)
        return carry

    lax.fori_loop(0, TILE // QBLK, attn_block, 0)

    step = 88
    levels = ((u_s, t2_s, 1), (t2_s, t4_s, 2), (t4_s, t8_s, 4), (t8_s, None, 8))
    for gi in range(len(POOL_WINDOWS)):
        assert POOL_WINDOWS[gi] == 2 ** (gi + 1)
    for li, (src, dst, sh) in enumerate(levels[:-1]):
        lanes = slice((li + 1) * pool_group, pool_w)
        for r in range(POOL_PAD, pool_rows, step):
            dst[r:r + step, lanes] = src[r:r + step, lanes] + src[r - sh:r - sh + step, lanes]
    tok = t * TILE + lax.broadcasted_iota(jnp.int32, (QBLK, pool_group), 0)
    for r in range(0, TILE, QBLK):
        for pair in range(len(POOL_WINDOWS) // 2):
            mixed = []
            for gi in (2 * pair, 2 * pair + 1):
                lanes = slice(gi * pool_group, (gi + 1) * pool_group)
                src = levels[gi][0]
                sh = levels[gi][2]
                rows = slice(POOL_OFF + r, POOL_OFF + r + QBLK)
                wsum = src[rows, lanes] + src[POOL_OFF + r - sh:POOL_OFF + r - sh + QBLK, lanes]
                cnt = jnp.minimum(tok + (r + 1), POOL_WINDOWS[gi]).astype(F32)
                mixed.append((wsum / cnt - u_s[rows, lanes]).astype(BF16))
            pm = jnp.concatenate(mixed, axis=1)
            c0 = pair * 2 * pool_group
            po = jnp.dot(pm, wp_ref[pair], preferred_element_type=F32) * ps_ref[:, c0:c0 + 2 * pool_group]
            z = proj_s[r:r + QBLK, o_z + attn_w + c0:o_z + attn_w + c0 + 2 * pool_group]
            y_s[r:r + QBLK, attn_w + c0:attn_w + c0 + 2 * pool_group] = (po * _silu(z)).astype(BF16)

    for r in range(0, TILE, 256):
        o = jnp.dot(y_s[r:r + 256, :], wout_ref[...], preferred_element_type=F32)
        o_ref[r:r + 256, :] = x_ref[r:r + 256, :] + gate * o

    k_s[0:TILE, :] = k_s[TILE:2 * TILE, :]
    v_s[0:TILE, :] = v_s[TILE:2 * TILE, :]
    u_s[POOL_PAD:POOL_OFF, :] = u_s[TILE + POOL_PAD:TILE + POOL_OFF, :]


def _bias_row_ext(rel_bias):
    assert QBLK <= REL_CLIP <= TILE
    n_heads = rel_bias.shape[0]
    first = rel_bias[:, 0:1].astype(F32)
    return jnp.concatenate([
        jnp.broadcast_to(first, (n_heads, TILE - REL_CLIP)),
        rel_bias[:, 0:REL_CLIP + QBLK].astype(F32),
        jnp.broadcast_to(first, (n_heads, BIAS_EXT - BANDW)),
    ], axis=1)


def _layer(x, mod, norm_g, w_in, q_norm_g, k_norm_g, rel_bias, w_pool, pool_scale, w_out):
    bsz, seq, d_model = x.shape
    n_heads = rel_bias.shape[0]
    attn_w = n_heads * HEAD_DIM
    pool_w = w_pool.shape[0] * w_pool.shape[1]
    pool_group = w_pool.shape[1]
    in_w = w_in.shape[1]
    assert seq % TILE == 0 and attn_w % GROUP_W == 0 and TILE == N_LEFT_CHUNKS * CHUNK
    assert in_w == 3 * attn_w + pool_w + attn_w + pool_w and 2 * pool_group == GROUP_W
    n_groups = attn_w // GROUP_W

    ones_bd = jnp.kron(jnp.eye(HEADS_PER_GROUP, dtype=F32), jnp.ones((HEAD_DIM, HEAD_DIM), F32)).astype(BF16)
    n_pairs = w_pool.shape[0] // 2
    wp = w_pool.reshape(n_pairs, 2, pool_group, pool_group)
    zero = jnp.zeros((n_pairs, pool_group, pool_group), F32)
    wp_bd = jnp.concatenate([jnp.concatenate([wp[:, 0], zero], axis=2),
                             jnp.concatenate([zero, wp[:, 1]], axis=2)], axis=1).astype(BF16)

    const2 = lambda b, t: (0, 0)
    const3 = lambda b, t: (0, 0, 0)
    kernel = functools.partial(_block_kernel, d_model=d_model, attn_w=attn_w, pool_w=pool_w)
    return pl.pallas_call(
        kernel,
        grid=(bsz, seq // TILE),
        in_specs=[
            pl.BlockSpec((None, TILE, d_model), lambda b, t: (b, t, 0)),
            pl.BlockSpec((None, 1, 3 * d_model), lambda b, t: (b, 0, 0)),
            pl.BlockSpec((1, d_model), const2),
            pl.BlockSpec((d_model, in_w), const2),
            pl.BlockSpec((attn_w + pool_w, d_model), const2),
            pl.BlockSpec((1, attn_w), const2),
            pl.BlockSpec((1, attn_w), const2),
            pl.BlockSpec((n_heads, BIAS_EXT), const2),
            pl.BlockSpec((GROUP_W, GROUP_W), const2),
            pl.BlockSpec((n_pairs, GROUP_W, GROUP_W), const3),
            pl.BlockSpec((1, pool_w), const2),
        ],
        out_specs=pl.BlockSpec((None, TILE, d_model), lambda b, t: (b, t, 0)),
        out_shape=jax.ShapeDtypeStruct(x.shape, x.dtype),
        scratch_shapes=[
            pltpu.VMEM((TILE, d_model), BF16),
            pltpu.VMEM((TILE, in_w), F32),
            pltpu.VMEM((TILE // QBLK, n_groups, HEADS_PER_GROUP * QBLK, GROUP_W), BF16),
            pltpu.VMEM((2 * TILE, attn_w), BF16),
            pltpu.VMEM((2 * TILE, attn_w), BF16),
            pltpu.VMEM((TILE + POOL_OFF, pool_w), F32),
            pltpu.VMEM((TILE + POOL_OFF, pool_w), F32),
            pltpu.VMEM((TILE + POOL_OFF, pool_w), F32),
            pltpu.VMEM((TILE + POOL_OFF, pool_w), F32),
            pltpu.VMEM((n_groups, HEADS_PER_GROUP * QBLK, BANDW), F32),
            pltpu.VMEM((n_groups, HEADS_PER_GROUP * QBLK, BANDW), BF16),
            pltpu.VMEM((n_groups, HEADS_PER_GROUP * QBLK, 128), F32),
            pltpu.VMEM((TILE, attn_w + pool_w), BF16),
            pltpu.VMEM((n_heads, QBLK, BANDW), F32),
        ],
        compiler_params=pltpu.CompilerParams(
            dimension_semantics=("arbitrary", "arbitrary"),
            vmem_limit_bytes=V7X_VMEM_LIMIT_BYTES,
        ),
        name="hybrid_block",
    )(x, mod.reshape(bsz, 1, 3 * d_model), norm_g.reshape(1, d_model), w_in.astype(BF16),
      w_out.astype(BF16), jnp.tile(q_norm_g, n_heads).reshape(1, attn_w),
      jnp.tile(k_norm_g, n_heads).reshape(1, attn_w), _bias_row_ext(rel_bias), ones_bd, wp_bd,
      pool_scale.reshape(1, pool_w))


def kernel(x, c, norm_g, w_ada, b_ada, w_in, q_norm_g, k_norm_g, rel_bias, w_pool, pool_scale, w_out):
    depth = w_in.shape[0]
    for l in range(depth):
        mod = _adaln_mod(c, w_ada[l], b_ada[l])
        x = _layer(x, mod, norm_g[l], w_in[l], q_norm_g[l], k_norm_g[l], rel_bias[l],
                   w_pool[l], pool_scale[l], w_out[l])
    return x
```

```python
import functools
import math

import jax
import jax.numpy as jnp
from jax import lax
from jax.experimental import pallas as pl
from jax.experimental.pallas import tpu as pltpu

CHUNK = 64
N_LEFT_CHUNKS = 8
HEAD_DIM = 64
REL_CLIP = 256
POOL_WINDOWS = (2, 4, 8, 16)
EPS = 1e-6
NEG_INF = -1e30
LOG2E = math.log2(math.e)

LANES = 128
TILE = 512
HALF = TILE // 2
QBLK = 2 * CHUNK
BANDW = TILE + QBLK
BAND_BLOCKS = BANDW // LANES
BIAS_EXT = BANDW + QBLK
HEADS_PER_GROUP = 4
GROUP_W = HEADS_PER_GROUP * HEAD_DIM
POOL_PAD = 8
POOL_HIST = 16
POOL_OFF = POOL_PAD + POOL_HIST
SOFTMAX_ROWS = 32
V7X_VMEM_LIMIT_BYTES = 56 * 1024 * 1024

F32 = jnp.float32
BF16 = jnp.bfloat16


def _mod_kernel(c_ref, w_ref, b_ref, o_ref):
    o_ref[...] = jnp.dot(c_ref[...], w_ref[...], preferred_element_type=F32) + b_ref[...]


def _adaln_mod(c, w_ada, b_ada):
    bsz, d = c.shape
    n = w_ada.shape[1]
    rows = 8 * ((bsz + 7) // 8)
    c_pad = jnp.zeros((rows, d), F32).at[:bsz].set(c)
    out = pl.pallas_call(
        _mod_kernel,
        grid=(n // d,),
        in_specs=[
            pl.BlockSpec((rows, d), lambda i: (0, 0)),
            pl.BlockSpec((d, d), lambda i: (0, i)),
            pl.BlockSpec((1, d), lambda i: (0, i)),
        ],
        out_specs=pl.BlockSpec((rows, d), lambda i: (0, i)),
        out_shape=jax.ShapeDtypeStruct((rows, n), F32),
        name="adaln_mod",
    )(c_pad, w_ada, b_ada.reshape(1, n))
    return out[:bsz]


def _silu(z):
    return z * (1.0 / (1.0 + jnp.exp(-z)))


def _block_kernel(x_ref, mod_ref, g_ref, win_ref, wout_ref, gq_ref, gk_ref, rbe_ref,
                  ones_ref, wp_ref, ps_ref, o_ref,
                  h_s, z_s, qbd_s, k_s, v_s, u_s, t2_s, t4_s, t8_s, s_s, p_s, linv_s, y_s, bias_ref,
                  *, d_model, attn_w, pool_w):
    t = pl.program_id(1)
    n_heads = attn_w // HEAD_DIM
    n_groups = attn_w // GROUP_W
    o_k, o_v, o_u, o_z = attn_w, 2 * attn_w, 3 * attn_w, 3 * attn_w + pool_w
    pool_group = pool_w // len(POOL_WINDOWS)
    pool_rows = TILE + POOL_OFF

    @pl.when((pl.program_id(0) == 0) & (t == 0))
    def _():
        a = lax.broadcasted_iota(jnp.int32, (QBLK, LANES), 0)
        lane = lax.broadcasted_iota(jnp.int32, (QBLK, LANES), 1)
        band_lo = jnp.where(a >= CHUNK, CHUNK, 0)
        for h in range(n_heads):
            row = jnp.broadcast_to(rbe_ref[h:h + 1, :] * LOG2E, (QBLK, BIAS_EXT))
            skew = pltpu.roll(row, 0, 1, stride=1, stride_axis=0)
            for cb in range(BAND_BLOCKS):
                j = lane + cb * LANES
                visible = (j >= band_lo) & (j < band_lo + (N_LEFT_CHUNKS + 1) * CHUNK)
                bias_ref[h, cb] = jnp.where(visible, skew[:, cb * LANES:(cb + 1) * LANES], NEG_INF)
            bias_ref[h, BAND_BLOCKS] = jnp.full((QBLK, LANES), NEG_INF, F32)

    @pl.when(t > 0)
    def _():
        k_s[0:TILE, :] = k_s[TILE:2 * TILE, :]
        v_s[0:TILE, :] = v_s[TILE:2 * TILE, :]
        u_s[POOL_PAD:POOL_OFF, :] = u_s[TILE + POOL_PAD:TILE + POOL_OFF, :]

    @pl.when(t == 0)
    def _():
        k_s[0:TILE, :] = jnp.zeros((TILE, attn_w), BF16)
        v_s[0:TILE, :] = jnp.zeros((TILE, attn_w), BF16)
        u_s[0:POOL_OFF, :] = jnp.zeros((POOL_OFF, pool_w), F32)
        t2_s[0:POOL_PAD, :] = jnp.zeros((POOL_PAD, pool_w), F32)
        t4_s[0:POOL_PAD, :] = jnp.zeros((POOL_PAD, pool_w), F32)
        t8_s[0:POOL_PAD, :] = jnp.zeros((POOL_PAD, pool_w), F32)

    shift = mod_ref[:, 0:d_model]
    scale = mod_ref[:, d_model:2 * d_model]
    gate = mod_ref[:, 2 * d_model:3 * d_model]
    a_row = g_ref[...] * (1.0 + scale)
    lane_g = lax.broadcasted_iota(jnp.int32, (QBLK, GROUP_W), 1)

    def in_proj(rows, c0, width):
        return jnp.dot(h_s[rows, :], win_ref[:, c0:c0 + width], preferred_element_type=F32)

    def head_rms(xf, gain_row):
        ssq = jnp.dot((xf * xf).astype(BF16), ones_ref[...], preferred_element_type=F32)
        return xf * lax.rsqrt(ssq * (1.0 / HEAD_DIM) + EPS) * gain_row

    for r0 in range(0, TILE, HALF):
        half = slice(r0, r0 + HALF)
        for r in range(r0, r0 + HALF, 64):
            xc = x_ref[r:r + 64, :]
            ms = jnp.mean(xc * xc, axis=-1, keepdims=True)
            hc = xc * lax.rsqrt(ms + EPS) * a_row + shift
            h_s[r:r + 64, :] = hc.astype(BF16)

        qf = in_proj(half, 0, attn_w)
        for rb in range(0, HALF, QBLK):
            for g in range(n_groups):
                cols = slice(g * GROUP_W, (g + 1) * GROUP_W)
                qn = head_rms(qf[rb:rb + QBLK, cols], gq_ref[:, cols] * (HEAD_DIM ** -0.5 * LOG2E))
                for h in range(HEADS_PER_GROUP):
                    keep = (lane_g >= h * HEAD_DIM) & (lane_g < (h + 1) * HEAD_DIM)
                    qbd_s[(r0 + rb) // QBLK, g, h * QBLK:(h + 1) * QBLK, :] = (
                        jnp.where(keep, qn, 0.0).astype(BF16))
        kf = in_proj(half, o_k, attn_w)
        for rb in range(0, HALF, QBLK):
            for g in range(n_groups):
                cols = slice(g * GROUP_W, (g + 1) * GROUP_W)
                kn = head_rms(kf[rb:rb + QBLK, cols], gk_ref[:, cols])
                k_s[TILE + r0 + rb:TILE + r0 + rb + QBLK, cols] = kn.astype(BF16)
        v_s[TILE + r0:TILE + r0 + HALF, :] = in_proj(half, o_v, attn_w).astype(BF16)
        u_s[POOL_OFF + r0:POOL_OFF + r0 + HALF, :] = in_proj(half, o_u, pool_w)
        for c0 in range(0, attn_w + pool_w, 512):
            z_s[half, c0:c0 + 512] = in_proj(half, o_z + c0, 512)

    step = 88
    levels = ((u_s, t2_s, 1), (t2_s, t4_s, 2), (t4_s, t8_s, 4), (t8_s, None, 8))
    for gi in range(len(POOL_WINDOWS)):
        assert POOL_WINDOWS[gi] == 2 ** (gi + 1)
    for li, (src, dst, sh) in enumerate(levels[:-1]):
        lanes = slice((li + 1) * pool_group, pool_w)
        for r in range(POOL_PAD, pool_rows, step):
            dst[r:r + step, lanes] = src[r:r + step, lanes] + src[r - sh:r - sh + step, lanes]
    tok = t * TILE + lax.broadcasted_iota(jnp.int32, (QBLK, pool_group), 0)
    for r in range(0, TILE, QBLK):
        for pair in range(len(POOL_WINDOWS) // 2):
            mixed = []
            for gi in (2 * pair, 2 * pair + 1):
                lanes = slice(gi * pool_group, (gi + 1) * pool_group)
                src = levels[gi][0]
                sh = levels[gi][2]
                rows = slice(POOL_OFF + r, POOL_OFF + r + QBLK)
                wsum = src[rows, lanes] + src[POOL_OFF + r - sh:POOL_OFF + r - sh + QBLK, lanes]
                cnt = jnp.minimum(tok + (r + 1), POOL_WINDOWS[gi]).astype(F32)
                mixed.append((wsum / cnt - u_s[rows, lanes]).astype(BF16))
            pm = jnp.concatenate(mixed, axis=1)
            c0 = pair * 2 * pool_group
            po = jnp.dot(pm, wp_ref[pair], preferred_element_type=F32) * ps_ref[:, c0:c0 + 2 * pool_group]
            z = z_s[r:r + QBLK, attn_w + c0:attn_w + c0 + 2 * pool_group]
            y_s[r:r + QBLK, attn_w + c0:attn_w + c0 + 2 * pool_group] = (po * _silu(z)).astype(BF16)

    lane_q = lax.broadcasted_iota(jnp.int32, (QBLK, LANES), 1)
    for jb in range(TILE // QBLK):
        qrow = jb * QBLK
        for g in range(n_groups):
            cols = slice(g * GROUP_W, (g + 1) * GROUP_W)
            s_s[g] = lax.dot_general(qbd_s[jb, g], k_s[qrow:qrow + BANDW, cols],
                                     (((1,), (1,)), ((), ())), preferred_element_type=F32)
        n_before = TILE // LANES - jb
        blk = [jnp.where(t == 0, BAND_BLOCKS, cb) if cb < n_before else cb for cb in range(BAND_BLOCKS)]
        for g in range(n_groups):
            for h in range(HEADS_PER_GROUP):
                hh = g * HEADS_PER_GROUP + h
                for rc in range(0, QBLK, SOFTMAX_ROWS):
                    rr = slice(h * QBLK + rc, h * QBLK + rc + SOFTMAX_ROWS)
                    sb = [s_s[g, rr, cb * LANES:(cb + 1) * LANES]
                          + bias_ref[hh, blk[cb], rc:rc + SOFTMAX_ROWS, :] for cb in range(BAND_BLOCKS)]
                    m = jnp.max(functools.reduce(jnp.maximum, sb), axis=-1, keepdims=True)
                    pb = [jnp.exp2(s - m) for s in sb]
                    l = jnp.sum(functools.reduce(jnp.add, pb), axis=-1, keepdims=True)
                    for cb in range(BAND_BLOCKS):
                        p_s[g, rr, cb * LANES:(cb + 1) * LANES] = pb[cb].astype(BF16)
                    linv_s[g, rr, :] = jnp.broadcast_to(1.0 / l, (SOFTMAX_ROWS, LANES))
        for g in range(n_groups):
            cols = slice(g * GROUP_W, (g + 1) * GROUP_W)
            o4 = jnp.dot(p_s[g], v_s[qrow:qrow + BANDW, cols], preferred_element_type=F32)
            for c in range(GROUP_W // LANES):
                ha, hb = 2 * c, 2 * c + 1
                oa = o4[ha * QBLK:(ha + 1) * QBLK, c * LANES:(c + 1) * LANES] * linv_s[g, ha * QBLK:(ha + 1) * QBLK, :]
                ob = o4[hb * QBLK:(hb + 1) * QBLK, c * LANES:(c + 1) * LANES] * linv_s[g, hb * QBLK:(hb + 1) * QBLK, :]
                a = jnp.where(lane_q < HEAD_DIM, oa, ob)
                c0 = g * GROUP_W + c * LANES
                z = z_s[qrow:qrow + QBLK, c0:c0 + LANES]
                y_s[qrow:qrow + QBLK, c0:c0 + LANES] = (a * _silu(z)).astype(BF16)
        if (qrow + QBLK) % HALF == 0:
            r0 = qrow + QBLK - HALF
            o = jnp.dot(y_s[r0:r0 + HALF, :], wout_ref[...], preferred_element_type=F32)
            o_ref[r0:r0 + HALF, :] = x_ref[r0:r0 + HALF, :] + gate * o


def _bias_row_ext(rel_bias):
    assert QBLK <= REL_CLIP <= TILE
    n_heads = rel_bias.shape[0]
    first = rel_bias[:, 0:1].astype(F32)
    return jnp.concatenate([
        jnp.broadcast_to(first, (n_heads, TILE - REL_CLIP)),
        rel_bias[:, 0:REL_CLIP + QBLK].astype(F32),
        jnp.broadcast_to(first, (n_heads, BIAS_EXT - BANDW)),
    ], axis=1)


def _layer(x, mod, norm_g, w_in, q_norm_g, k_norm_g, rel_bias, w_pool, pool_scale, w_out):
    bsz, seq, d_model = x.shape
    n_heads = rel_bias.shape[0]
    attn_w = n_heads * HEAD_DIM
    pool_w = w_pool.shape[0] * w_pool.shape[1]
    pool_group = w_pool.shape[1]
    in_w = w_in.shape[1]
    assert seq % TILE == 0 and attn_w % GROUP_W == 0 and TILE == N_LEFT_CHUNKS * CHUNK
    assert in_w == 3 * attn_w + pool_w + attn_w + pool_w and 2 * pool_group == GROUP_W
    assert attn_w == 512 and pool_w == 512
    n_groups = attn_w // GROUP_W

    ones_bd = jnp.kron(jnp.eye(HEADS_PER_GROUP, dtype=F32), jnp.ones((HEAD_DIM, HEAD_DIM), F32)).astype(BF16)
    n_pairs = w_pool.shape[0] // 2
    wp = w_pool.reshape(n_pairs, 2, pool_group, pool_group)
    zero = jnp.zeros((n_pairs, pool_group, pool_group), F32)
    wp_bd = jnp.concatenate([jnp.concatenate([wp[:, 0], zero], axis=2),
                             jnp.concatenate([zero, wp[:, 1]], axis=2)], axis=1).astype(BF16)

    const2 = lambda b, t: (0, 0)
    const3 = lambda b, t: (0, 0, 0)
    kernel = functools.partial(_block_kernel, d_model=d_model, attn_w=attn_w, pool_w=pool_w)
    return pl.pallas_call(
        kernel,
        grid=(bsz, seq // TILE),
        in_specs=[
            pl.BlockSpec((None, TILE, d_model), lambda b, t: (b, t, 0)),
            pl.BlockSpec((None, 1, 3 * d_model), lambda b, t: (b, 0, 0)),
            pl.BlockSpec((1, d_model), const2),
            pl.BlockSpec((d_model, in_w), const2),
            pl.BlockSpec((attn_w + pool_w, d_model), const2),
            pl.BlockSpec((1, attn_w), const2),
            pl.BlockSpec((1, attn_w), const2),
            pl.BlockSpec((n_heads, BIAS_EXT), const2),
            pl.BlockSpec((GROUP_W, GROUP_W), const2),
            pl.BlockSpec((n_pairs, GROUP_W, GROUP_W), const3),
            pl.BlockSpec((1, pool_w), const2),
        ],
        out_specs=pl.BlockSpec((None, TILE, d_model), lambda b, t: (b, t, 0)),
        out_shape=jax.ShapeDtypeStruct(x.shape, x.dtype),
        scratch_shapes=[
            pltpu.VMEM((TILE, d_model), BF16),
            pltpu.VMEM((TILE, attn_w + pool_w), F32),
            pltpu.VMEM((TILE // QBLK, n_groups, HEADS_PER_GROUP * QBLK, GROUP_W), BF16),
            pltpu.VMEM((2 * TILE, attn_w), BF16),
            pltpu.VMEM((2 * TILE, attn_w), BF16),
            pltpu.VMEM((TILE + POOL_OFF, pool_w), F32),
            pltpu.VMEM((TILE + POOL_OFF, pool_w), F32),
            pltpu.VMEM((TILE + POOL_OFF, pool_w), F32),
            pltpu.VMEM((TILE + POOL_OFF, pool_w), F32),
            pltpu.VMEM((n_groups, HEADS_PER_GROUP * QBLK, BANDW), F32),
            pltpu.VMEM((n_groups, HEADS_PER_GROUP * QBLK, BANDW), BF16),
            pltpu.VMEM((n_groups, HEADS_PER_GROUP * QBLK, LANES), F32),
            pltpu.VMEM((TILE, attn_w + pool_w), BF16),
            pltpu.VMEM((n_heads, BAND_BLOCKS + 1, QBLK, LANES), F32),
        ],
        compiler_params=pltpu.CompilerParams(
            dimension_semantics=("arbitrary", "arbitrary"),
            vmem_limit_bytes=V7X_VMEM_LIMIT_BYTES,
        ),
        name="hybrid_block",
    )(x, mod.reshape(bsz, 1, 3 * d_model), norm_g.reshape(1, d_model), w_in.astype(BF16),
      w_out.astype(BF16), jnp.tile(q_norm_g, n_heads).reshape(1, attn_w),
      jnp.tile(k_norm_g, n_heads).reshape(1, attn_w), _bias_row_ext(rel_bias), ones_bd, wp_bd,
      pool_scale.reshape(1, pool_w))


def kernel(x, c, norm_g, w_ada, b_ada, w_in, q_norm_g, k_norm_g, rel_bias, w_pool, pool_scale, w_out):
    depth = w_in.shape[0]
    for l in range(depth):
        mod = _adaln_mod(c, w_ada[l], b_ada[l])
        x = _layer(x, mod, norm_g[l], w_in[l], q_norm_g[l], k_norm_g[l], rel_bias[l],
                   w_pool[l], pool_scale[l], w_out[l])
    return x
```

```python
import functools
import math

import jax
import jax.numpy as jnp
from jax import lax
from jax.experimental import pallas as pl
from jax.experimental.pallas import tpu as pltpu

CHUNK = 64
N_LEFT_CHUNKS = 8
HEAD_DIM = 64
REL_CLIP = 256
POOL_WINDOWS = (2, 4, 8, 16)
EPS = 1e-6
NEG_INF = -1e30
LOG2E = math.log2(math.e)

LANES = 128
TILE = 512
HALF = TILE // 2
QBLK = 2 * CHUNK
BANDW = TILE + QBLK
BAND_BLOCKS = BANDW // LANES
BIAS_EXT = BANDW + QBLK
HEADS_PER_GROUP = 4
GROUP_W = HEADS_PER_GROUP * HEAD_DIM
POOL_PAD = 8
POOL_HIST = 16
POOL_OFF = POOL_PAD + POOL_HIST
SOFTMAX_ROWS = 32
V7X_VMEM_LIMIT_BYTES = 56 * 1024 * 1024

F32 = jnp.float32
BF16 = jnp.bfloat16


def _mod_kernel(c_ref, w_ref, b_ref, o_ref):
    o_ref[...] = jnp.dot(c_ref[...], w_ref[...], preferred_element_type=F32) + b_ref[...]


def _adaln_mod(c, w_ada, b_ada):
    bsz, d = c.shape
    n = w_ada.shape[1]
    rows = 8 * ((bsz + 7) // 8)
    c_pad = jnp.zeros((rows, d), F32).at[:bsz].set(c)
    out = pl.pallas_call(
        _mod_kernel,
        grid=(n // d,),
        in_specs=[
            pl.BlockSpec((rows, d), lambda i: (0, 0)),
            pl.BlockSpec((d, d), lambda i: (0, i)),
            pl.BlockSpec((1, d), lambda i: (0, i)),
        ],
        out_specs=pl.BlockSpec((rows, d), lambda i: (0, i)),
        out_shape=jax.ShapeDtypeStruct((rows, n), F32),
        name="adaln_mod",
    )(c_pad, w_ada, b_ada.reshape(1, n))
    return out[:bsz]


def _silu(z):
    return z * (1.0 / (1.0 + jnp.exp(-z)))


def _block_kernel(x_ref, mod_ref, g_ref, win_ref, wout_ref, gq_ref, gk_ref, rbe_ref,
                  ones_ref, wp_ref, ps_ref, o_ref,
                  h_s, z_s, qbd_s, k_s, v_s, u_s, t2_s, t4_s, t8_s, s_s, p_s, linv_s, y_s, bias_ref,
                  *, d_model, attn_w, pool_w):
    t = pl.program_id(1)
    n_heads = attn_w // HEAD_DIM
    n_groups = attn_w // GROUP_W
    o_k, o_v, o_u, o_z = attn_w, 2 * attn_w, 3 * attn_w, 3 * attn_w + pool_w
    pool_group = pool_w // len(POOL_WINDOWS)
    pool_rows = TILE + POOL_OFF

    @pl.when((pl.program_id(0) == 0) & (t == 0))
    def _():
        a = lax.broadcasted_iota(jnp.int32, (QBLK, LANES), 0)
        lane = lax.broadcasted_iota(jnp.int32, (QBLK, LANES), 1)
        band_lo = jnp.where(a >= CHUNK, CHUNK, 0)
        for h in range(n_heads):
            row = jnp.broadcast_to(rbe_ref[h:h + 1, :] * LOG2E, (QBLK, BIAS_EXT))
            skew = pltpu.roll(row, 0, 1, stride=1, stride_axis=0)
            for cb in range(BAND_BLOCKS):
                j = lane + cb * LANES
                visible = (j >= band_lo) & (j < band_lo + (N_LEFT_CHUNKS + 1) * CHUNK)
                bias_ref[h, cb] = jnp.where(visible, skew[:, cb * LANES:(cb + 1) * LANES], NEG_INF)
            bias_ref[h, BAND_BLOCKS] = jnp.full((QBLK, LANES), NEG_INF, F32)

    @pl.when(t > 0)
    def _():
        k_s[0:TILE, :] = k_s[TILE:2 * TILE, :]
        v_s[0:TILE, :] = v_s[TILE:2 * TILE, :]
        u_s[POOL_PAD:POOL_OFF, :] = u_s[TILE + POOL_PAD:TILE + POOL_OFF, :]

    @pl.when(t == 0)
    def _():
        k_s[0:TILE, :] = jnp.zeros((TILE, attn_w), BF16)
        v_s[0:TILE, :] = jnp.zeros((TILE, attn_w), BF16)
        u_s[0:POOL_OFF, :] = jnp.zeros((POOL_OFF, pool_w), F32)
        t2_s[0:POOL_PAD, :] = jnp.zeros((POOL_PAD, pool_w), F32)
        t4_s[0:POOL_PAD, :] = jnp.zeros((POOL_PAD, pool_w), F32)
        t8_s[0:POOL_PAD, :] = jnp.zeros((POOL_PAD, pool_w), F32)

    shift = mod_ref[:, 0:d_model]
    scale = mod_ref[:, d_model:2 * d_model]
    gate = mod_ref[:, 2 * d_model:3 * d_model]
    a_row = g_ref[...] * (1.0 + scale)
    lane_g = lax.broadcasted_iota(jnp.int32, (QBLK, GROUP_W), 1)

    def in_proj(rows, c0, width):
        return jnp.dot(h_s[rows, :], win_ref[:, c0:c0 + width], preferred_element_type=F32)

    def head_rms(xf, gain_row):
        ssq = jnp.dot((xf * xf).astype(BF16), ones_ref[...], preferred_element_type=F32)
        return xf * lax.rsqrt(ssq * (1.0 / HEAD_DIM) + EPS) * gain_row

    for r0 in range(0, TILE, HALF):
        half = slice(r0, r0 + HALF)
        for r in range(r0, r0 + HALF, 64):
            xc = x_ref[r:r + 64, :]
            ms = jnp.mean(xc * xc, axis=-1, keepdims=True)
            hc = xc * lax.rsqrt(ms + EPS) * a_row + shift
            h_s[r:r + 64, :] = hc.astype(BF16)

        qf = in_proj(half, 0, attn_w)
        for rb in range(0, HALF, QBLK):
            for g in range(n_groups):
                cols = slice(g * GROUP_W, (g + 1) * GROUP_W)
                qn = head_rms(qf[rb:rb + QBLK, cols], gq_ref[:, cols] * (HEAD_DIM ** -0.5 * LOG2E))
                for h in range(HEADS_PER_GROUP):
                    keep = (lane_g >= h * HEAD_DIM) & (lane_g < (h + 1) * HEAD_DIM)
                    qbd_s[(r0 + rb) // QBLK, g, h * QBLK:(h + 1) * QBLK, :] = (
                        jnp.where(keep, qn, 0.0).astype(BF16))
        kf = in_proj(half, o_k, attn_w)
        for rb in range(0, HALF, QBLK):
            for g in range(n_groups):
                cols = slice(g * GROUP_W, (g + 1) * GROUP_W)
                kn = head_rms(kf[rb:rb + QBLK, cols], gk_ref[:, cols])
                k_s[TILE + r0 + rb:TILE + r0 + rb + QBLK, cols] = kn.astype(BF16)
        v_s[TILE + r0:TILE + r0 + HALF, :] = in_proj(half, o_v, attn_w).astype(BF16)
        u_s[POOL_OFF + r0:POOL_OFF + r0 + HALF, :] = in_proj(half, o_u, pool_w)
        for c0 in range(0, attn_w + pool_w, 512):
            z_s[half, c0:c0 + 512] = in_proj(half, o_z + c0, 512)

    step = 88
    levels = ((u_s, t2_s, 1), (t2_s, t4_s, 2), (t4_s, t8_s, 4), (t8_s, None, 8))
    for gi in range(len(POOL_WINDOWS)):
        assert POOL_WINDOWS[gi] == 2 ** (gi + 1)
    for li, (src, dst, sh) in enumerate(levels[:-1]):
        lanes = slice((li + 1) * pool_group, pool_w)
        for r in range(POOL_PAD, pool_rows, step):
            dst[r:r + step, lanes] = src[r:r + step, lanes] + src[r - sh:r - sh + step, lanes]
    tok = t * TILE + lax.broadcasted_iota(jnp.int32, (QBLK, pool_group), 0)
    for r in range(0, TILE, QBLK):
        for pair in range(len(POOL_WINDOWS) // 2):
            mixed = []
            for gi in (2 * pair, 2 * pair + 1):
                lanes = slice(gi * pool_group, (gi + 1) * pool_group)
                src = levels[gi][0]
                sh = levels[gi][2]
                rows = slice(POOL_OFF + r, POOL_OFF + r + QBLK)
                wsum = src[rows, lanes] + src[POOL_OFF + r - sh:POOL_OFF + r - sh + QBLK, lanes]
                cnt = jnp.minimum(tok + (r + 1), POOL_WINDOWS[gi]).astype(F32)
                mixed.append((wsum / cnt - u_s[rows, lanes]).astype(BF16))
            pm = jnp.concatenate(mixed, axis=1)
            c0 = pair * 2 * pool_group
            po = jnp.dot(pm, wp_ref[pair], preferred_element_type=F32) * ps_ref[:, c0:c0 + 2 * pool_group]
            z = z_s[r:r + QBLK, attn_w + c0:attn_w + c0 + 2 * pool_group]
            y_s[r:r + QBLK, attn_w + c0:attn_w + c0 + 2 * pool_group] = (po * _silu(z)).astype(BF16)

    lane_q = lax.broadcasted_iota(jnp.int32, (QBLK, LANES), 1)
    n_qblk = TILE // QBLK

    def scores(jb):
        qrow = jb * QBLK
        for g in range(n_groups):
            cols = slice(g * GROUP_W, (g + 1) * GROUP_W)
            s_s[jb % 2, g] = lax.dot_general(qbd_s[jb, g], k_s[qrow:qrow + BANDW, cols],
                                             (((1,), (1,)), ((), ())), preferred_element_type=F32)

    def softmax(jb):
        buf = jb % 2
        n_before = TILE // LANES - jb
        blk = [jnp.where(t == 0, BAND_BLOCKS, cb) if cb < n_before else cb for cb in range(BAND_BLOCKS)]
        for g in range(n_groups):
            for h in range(HEADS_PER_GROUP):
                hh = g * HEADS_PER_GROUP + h
                for rc in range(0, QBLK, SOFTMAX_ROWS):
                    rr = slice(h * QBLK + rc, h * QBLK + rc + SOFTMAX_ROWS)
                    sb = [s_s[buf, g, rr, cb * LANES:(cb + 1) * LANES]
                          + bias_ref[hh, blk[cb], rc:rc + SOFTMAX_ROWS, :] for cb in range(BAND_BLOCKS)]
                    m = jnp.max(functools.reduce(jnp.maximum, sb), axis=-1, keepdims=True)
                    pb = [jnp.exp2(s - m) for s in sb]
                    l = jnp.sum(functools.reduce(jnp.add, pb), axis=-1, keepdims=True)
                    for cb in range(BAND_BLOCKS):
                        p_s[buf, g, rr, cb * LANES:(cb + 1) * LANES] = pb[cb].astype(BF16)
                    linv_s[buf, g, rr, :] = jnp.broadcast_to(1.0 / l, (SOFTMAX_ROWS, LANES))

    def attend(jb):
        buf = jb % 2
        qrow = jb * QBLK
        for g in range(n_groups):
            cols = slice(g * GROUP_W, (g + 1) * GROUP_W)
            o4 = jnp.dot(p_s[buf, g], v_s[qrow:qrow + BANDW, cols], preferred_element_type=F32)
            for c in range(GROUP_W // LANES):
                ha, hb = 2 * c, 2 * c + 1
                oa = (o4[ha * QBLK:(ha + 1) * QBLK, c * LANES:(c + 1) * LANES]
                      * linv_s[buf, g, ha * QBLK:(ha + 1) * QBLK, :])
                ob = (o4[hb * QBLK:(hb + 1) * QBLK, c * LANES:(c + 1) * LANES]
                      * linv_s[buf, g, hb * QBLK:(hb + 1) * QBLK, :])
                a = jnp.where(lane_q < HEAD_DIM, oa, ob)
                c0 = g * GROUP_W + c * LANES
                z = z_s[qrow:qrow + QBLK, c0:c0 + LANES]
                y_s[qrow:qrow + QBLK, c0:c0 + LANES] = (a * _silu(z)).astype(BF16)

    def out_proj(r0):
        o = jnp.dot(y_s[r0:r0 + HALF, :], wout_ref[...], preferred_element_type=F32)
        o_ref[r0:r0 + HALF, :] = x_ref[r0:r0 + HALF, :] + gate * o

    scores(0)
    for jb in range(n_qblk):
        if jb + 1 < n_qblk:
            scores(jb + 1)
        softmax(jb)
        attend(jb)
        if (jb + 1) * QBLK % HALF == 0:
            out_proj((jb + 1) * QBLK - HALF)


def _bias_row_ext(rel_bias):
    assert QBLK <= REL_CLIP <= TILE
    n_heads = rel_bias.shape[0]
    first = rel_bias[:, 0:1].astype(F32)
    return jnp.concatenate([
        jnp.broadcast_to(first, (n_heads, TILE - REL_CLIP)),
        rel_bias[:, 0:REL_CLIP + QBLK].astype(F32),
        jnp.broadcast_to(first, (n_heads, BIAS_EXT - BANDW)),
    ], axis=1)


def _layer(x, mod, norm_g, w_in, q_norm_g, k_norm_g, rel_bias, w_pool, pool_scale, w_out):
    bsz, seq, d_model = x.shape
    n_heads = rel_bias.shape[0]
    attn_w = n_heads * HEAD_DIM
    pool_w = w_pool.shape[0] * w_pool.shape[1]
    pool_group = w_pool.shape[1]
    in_w = w_in.shape[1]
    assert seq % TILE == 0 and attn_w % GROUP_W == 0 and TILE == N_LEFT_CHUNKS * CHUNK
    assert in_w == 3 * attn_w + pool_w + attn_w + pool_w and 2 * pool_group == GROUP_W
    assert attn_w == 512 and pool_w == 512
    n_groups = attn_w // GROUP_W

    ones_bd = jnp.kron(jnp.eye(HEADS_PER_GROUP, dtype=F32), jnp.ones((HEAD_DIM, HEAD_DIM), F32)).astype(BF16)
    n_pairs = w_pool.shape[0] // 2
    wp = w_pool.reshape(n_pairs, 2, pool_group, pool_group)
    zero = jnp.zeros((n_pairs, pool_group, pool_group), F32)
    wp_bd = jnp.concatenate([jnp.concatenate([wp[:, 0], zero], axis=2),
                             jnp.concatenate([zero, wp[:, 1]], axis=2)], axis=1).astype(BF16)

    const2 = lambda b, t: (0, 0)
    const3 = lambda b, t: (0, 0, 0)
    kernel = functools.partial(_block_kernel, d_model=d_model, attn_w=attn_w, pool_w=pool_w)
    return pl.pallas_call(
        kernel,
        grid=(bsz, seq // TILE),
        in_specs=[
            pl.BlockSpec((None, TILE, d_model), lambda b, t: (b, t, 0)),
            pl.BlockSpec((None, 1, 3 * d_model), lambda b, t: (b, 0, 0)),
            pl.BlockSpec((1, d_model), const2),
            pl.BlockSpec((d_model, in_w), const2),
            pl.BlockSpec((attn_w + pool_w, d_model), const2),
            pl.BlockSpec((1, attn_w), const2),
            pl.BlockSpec((1, attn_w), const2),
            pl.BlockSpec((n_heads, BIAS_EXT), const2),
            pl.BlockSpec((GROUP_W, GROUP_W), const2),
            pl.BlockSpec((n_pairs, GROUP_W, GROUP_W), const3),
            pl.BlockSpec((1, pool_w), const2),
        ],
        out_specs=pl.BlockSpec((None, TILE, d_model), lambda b, t: (b, t, 0)),
        out_shape=jax.ShapeDtypeStruct(x.shape, x.dtype),
        scratch_shapes=[
            pltpu.VMEM((TILE, d_model), BF16),
            pltpu.VMEM((TILE, attn_w + pool_w), F32),
            pltpu.VMEM((TILE // QBLK, n_groups, HEADS_PER_GROUP * QBLK, GROUP_W), BF16),
            pltpu.VMEM((2 * TILE, attn_w), BF16),
            pltpu.VMEM((2 * TILE, attn_w), BF16),
            pltpu.VMEM((TILE + POOL_OFF, pool_w), F32),
            pltpu.VMEM((TILE + POOL_OFF, pool_w), F32),
            pltpu.VMEM((TILE + POOL_OFF, pool_w), F32),
            pltpu.VMEM((TILE + POOL_OFF, pool_w), F32),
            pltpu.VMEM((2, n_groups, HEADS_PER_GROUP * QBLK, BANDW), F32),
            pltpu.VMEM((2, n_groups, HEADS_PER_GROUP * QBLK, BANDW), BF16),
            pltpu.VMEM((2, n_groups, HEADS_PER_GROUP * QBLK, LANES), F32),
            pltpu.VMEM((TILE, attn_w + pool_w), BF16),
            pltpu.VMEM((n_heads, BAND_BLOCKS + 1, QBLK, LANES), F32),
        ],
        compiler_params=pltpu.CompilerParams(
            dimension_semantics=("arbitrary", "arbitrary"),
            vmem_limit_bytes=V7X_VMEM_LIMIT_BYTES,
        ),
        name="hybrid_block",
    )(x, mod.reshape(bsz, 1, 3 * d_model), norm_g.reshape(1, d_model), w_in.astype(BF16),
      w_out.astype(BF16), jnp.tile(q_norm_g, n_heads).reshape(1, attn_w),
      jnp.tile(k_norm_g, n_heads).reshape(1, attn_w), _bias_row_ext(rel_bias), ones_bd, wp_bd,
      pool_scale.reshape(1, pool_w))


def kernel(x, c, norm_g, w_ada, b_ada, w_in, q_norm_g, k_norm_g, rel_bias, w_pool, pool_scale, w_out):
    depth = w_in.shape[0]
    for l in range(depth):
        mod = _adaln_mod(c, w_ada[l], b_ada[l])
        x = _layer(x, mod, norm_g[l], w_in[l], q_norm_g[l], k_norm_g[l], rel_bias[l],
                   w_pool[l], pool_scale[l], w_out[l])
    return x
```

```python
import functools
import math

import jax
import jax.numpy as jnp
from jax import lax
from jax.experimental import pallas as pl
from jax.experimental.pallas import tpu as pltpu

CHUNK = 64
N_LEFT_CHUNKS = 8
HEAD_DIM = 64
REL_CLIP = 256
POOL_WINDOWS = (2, 4, 8, 16)
EPS = 1e-6
NEG_INF = -1e30
LOG2E = math.log2(math.e)

LANES = 128
TILE = 512
HALF = TILE // 2
QBLK = 2 * CHUNK
BANDW = TILE + QBLK
BAND_BLOCKS = BANDW // LANES
BIAS_EXT = BANDW + QBLK
HEADS_PER_GROUP = 4
GROUP_W = HEADS_PER_GROUP * HEAD_DIM
POOL_PAD = 8
POOL_HIST = 16
POOL_OFF = POOL_PAD + POOL_HIST
SOFTMAX_ROWS = 32
V7X_VMEM_LIMIT_BYTES = 56 * 1024 * 1024

F32 = jnp.float32
BF16 = jnp.bfloat16


def _mod_kernel(c_ref, w_ref, b_ref, o_ref):
    o_ref[...] = jnp.dot(c_ref[...], w_ref[...], preferred_element_type=F32) + b_ref[...]


def _adaln_mod(c, w_ada, b_ada):
    bsz, d = c.shape
    n = w_ada.shape[1]
    rows = 8 * ((bsz + 7) // 8)
    c_pad = jnp.zeros((rows, d), F32).at[:bsz].set(c)
    out = pl.pallas_call(
        _mod_kernel,
        grid=(n // d,),
        in_specs=[
            pl.BlockSpec((rows, d), lambda i: (0, 0)),
            pl.BlockSpec((d, d), lambda i: (0, i)),
            pl.BlockSpec((1, d), lambda i: (0, i)),
        ],
        out_specs=pl.BlockSpec((rows, d), lambda i: (0, i)),
        out_shape=jax.ShapeDtypeStruct((rows, n), F32),
        name="adaln_mod",
    )(c_pad, w_ada, b_ada.reshape(1, n))
    return out[:bsz]


def _silu(z):
    return z * (1.0 / (1.0 + jnp.exp(-z)))


def _block_kernel(x_ref, mod_ref, g_ref, win_ref, wvt_ref, wout_ref, gq_ref, gk_ref, rbe_ref,
                  ones_ref, wp_ref, ps_ref, o_ref,
                  h_s, z_s, qbd_s, k_s, vt_s, u_s, t2_s, t4_s, t8_s, s_s, p_s, m_s, linv_s, y_s, bias_ref,
                  *, d_model, attn_w, pool_w):
    t = pl.program_id(1)
    n_heads = attn_w // HEAD_DIM
    n_groups = attn_w // GROUP_W
    o_k, o_v, o_u, o_z = attn_w, 2 * attn_w, 3 * attn_w, 3 * attn_w + pool_w
    pool_group = pool_w // len(POOL_WINDOWS)
    pool_rows = TILE + POOL_OFF

    @pl.when((pl.program_id(0) == 0) & (t == 0))
    def _():
        a = lax.broadcasted_iota(jnp.int32, (QBLK, LANES), 0)
        lane = lax.broadcasted_iota(jnp.int32, (QBLK, LANES), 1)
        band_lo = jnp.where(a >= CHUNK, CHUNK, 0)
        for h in range(n_heads):
            row = jnp.broadcast_to(rbe_ref[h:h + 1, :] * LOG2E, (QBLK, BIAS_EXT))
            skew = pltpu.roll(row, 0, 1, stride=1, stride_axis=0)
            for cb in range(BAND_BLOCKS):
                j = lane + cb * LANES
                visible = (j >= band_lo) & (j < band_lo + (N_LEFT_CHUNKS + 1) * CHUNK)
                bias_ref[h, cb] = jnp.where(visible, skew[:, cb * LANES:(cb + 1) * LANES], NEG_INF).T
            bias_ref[h, BAND_BLOCKS] = jnp.full((QBLK, LANES), NEG_INF, F32)

    @pl.when(t > 0)
    def _():
        k_s[0:TILE, :] = k_s[TILE:2 * TILE, :]
        vt_s[:, 0:TILE] = vt_s[:, TILE:2 * TILE]
        u_s[POOL_PAD:POOL_OFF, :] = u_s[TILE + POOL_PAD:TILE + POOL_OFF, :]

    @pl.when(t == 0)
    def _():
        k_s[0:TILE, :] = jnp.zeros((TILE, attn_w), BF16)
        vt_s[:, 0:TILE] = jnp.zeros((attn_w, TILE), BF16)
        u_s[0:POOL_OFF, :] = jnp.zeros((POOL_OFF, pool_w), F32)
        t2_s[0:POOL_PAD, :] = jnp.zeros((POOL_PAD, pool_w), F32)
        t4_s[0:POOL_PAD, :] = jnp.zeros((POOL_PAD, pool_w), F32)
        t8_s[0:POOL_PAD, :] = jnp.zeros((POOL_PAD, pool_w), F32)

    shift = mod_ref[:, 0:d_model]
    scale = mod_ref[:, d_model:2 * d_model]
    gate = mod_ref[:, 2 * d_model:3 * d_model]
    a_row = g_ref[...] * (1.0 + scale)
    lane_g = lax.broadcasted_iota(jnp.int32, (QBLK, GROUP_W), 1)

    def in_proj(rows, c0, width):
        return jnp.dot(h_s[rows, :], win_ref[:, c0:c0 + width], preferred_element_type=F32)

    def head_rms(xf, gain_row):
        ssq = jnp.dot((xf * xf).astype(BF16), ones_ref[...], preferred_element_type=F32)
        return xf * lax.rsqrt(ssq * (1.0 / HEAD_DIM) + EPS) * gain_row

    for r0 in range(0, TILE, HALF):
        half = slice(r0, r0 + HALF)
        for r in range(r0, r0 + HALF, 64):
            xc = x_ref[r:r + 64, :]
            ms = jnp.mean(xc * xc, axis=-1, keepdims=True)
            hc = xc * lax.rsqrt(ms + EPS) * a_row + shift
            h_s[r:r + 64, :] = hc.astype(BF16)

        qf = in_proj(half, 0, attn_w)
        for rb in range(0, HALF, QBLK):
            for g in range(n_groups):
                cols = slice(g * GROUP_W, (g + 1) * GROUP_W)
                qn = head_rms(qf[rb:rb + QBLK, cols], gq_ref[:, cols] * (HEAD_DIM ** -0.5 * LOG2E))
                for h in range(HEADS_PER_GROUP):
                    keep = (lane_g >= h * HEAD_DIM) & (lane_g < (h + 1) * HEAD_DIM)
                    qbd_s[(r0 + rb) // QBLK, g, h * QBLK:(h + 1) * QBLK, :] = (
                        jnp.where(keep, qn, 0.0).astype(BF16))
        kf = in_proj(half, o_k, attn_w)
        for rb in range(0, HALF, QBLK):
            for g in range(n_groups):
                cols = slice(g * GROUP_W, (g + 1) * GROUP_W)
                kn = head_rms(kf[rb:rb + QBLK, cols], gk_ref[:, cols])
                k_s[TILE + r0 + rb:TILE + r0 + rb + QBLK, cols] = kn.astype(BF16)
        vt_s[:, TILE + r0:TILE + r0 + HALF] = lax.dot_general(
            wvt_ref[...], h_s[half, :], (((1,), (1,)), ((), ())), preferred_element_type=F32).astype(BF16)
        u_s[POOL_OFF + r0:POOL_OFF + r0 + HALF, :] = in_proj(half, o_u, pool_w)
        for c0 in range(0, attn_w + pool_w, 512):
            z_s[half, c0:c0 + 512] = in_proj(half, o_z + c0, 512)

    step = 88
    levels = ((u_s, t2_s, 1), (t2_s, t4_s, 2), (t4_s, t8_s, 4), (t8_s, None, 8))
    for gi in range(len(POOL_WINDOWS)):
        assert POOL_WINDOWS[gi] == 2 ** (gi + 1)
    for li, (src, dst, sh) in enumerate(levels[:-1]):
        lanes = slice((li + 1) * pool_group, pool_w)
        for r in range(POOL_PAD, pool_rows, step):
            dst[r:r + step, lanes] = src[r:r + step, lanes] + src[r - sh:r - sh + step, lanes]
    tok = t * TILE + lax.broadcasted_iota(jnp.int32, (QBLK, pool_group), 0)
    for r in range(0, TILE, QBLK):
        for pair in range(len(POOL_WINDOWS) // 2):
            mixed = []
            for gi in (2 * pair, 2 * pair + 1):
                lanes = slice(gi * pool_group, (gi + 1) * pool_group)
                src = levels[gi][0]
                sh = levels[gi][2]
                rows = slice(POOL_OFF + r, POOL_OFF + r + QBLK)
                wsum = src[rows, lanes] + src[POOL_OFF + r - sh:POOL_OFF + r - sh + QBLK, lanes]
                cnt = jnp.minimum(tok + (r + 1), POOL_WINDOWS[gi]).astype(F32)
                mixed.append((wsum / cnt - u_s[rows, lanes]).astype(BF16))
            pm = jnp.concatenate(mixed, axis=1)
            c0 = pair * 2 * pool_group
            po = jnp.dot(pm, wp_ref[pair], preferred_element_type=F32) * ps_ref[:, c0:c0 + 2 * pool_group]
            z = z_s[r:r + QBLK, attn_w + c0:attn_w + c0 + 2 * pool_group]
            y_s[r:r + QBLK, attn_w + c0:attn_w + c0 + 2 * pool_group] = (po * _silu(z)).astype(BF16)

    row_q = lax.broadcasted_iota(jnp.int32, (QBLK, LANES), 0)
    n_qblk = TILE // QBLK

    def scores(jb):
        buf = jb % 2
        qrow = jb * QBLK
        n_before = TILE // LANES - jb
        blk = [jnp.where(t == 0, BAND_BLOCKS, cb) if cb < n_before else cb for cb in range(BAND_BLOCKS)]
        for g in range(n_groups):
            cols = slice(g * GROUP_W, (g + 1) * GROUP_W)
            st = lax.dot_general(k_s[qrow:qrow + BANDW, cols], qbd_s[jb, g],
                                 (((1,), (1,)), ((), ())), preferred_element_type=F32)
            for h in range(HEADS_PER_GROUP):
                hh = g * HEADS_PER_GROUP + h
                lanes = slice(h * QBLK, (h + 1) * QBLK)
                m = None
                for cb in range(BAND_BLOCKS):
                    rows = slice(cb * LANES, (cb + 1) * LANES)
                    sb = st[rows, lanes] + bias_ref[hh, blk[cb]]
                    s_s[buf, g, rows, lanes] = sb
                    mb = jnp.max(sb, axis=0, keepdims=True)
                    m = mb if m is None else jnp.maximum(m, mb)
                m_s[buf, hh] = jnp.broadcast_to(m, (8, QBLK))

    def softmax(jb):
        buf = jb % 2
        for g in range(n_groups):
            for h in range(HEADS_PER_GROUP):
                hh = g * HEADS_PER_GROUP + h
                lanes = slice(h * QBLK, (h + 1) * QBLK)
                m = m_s[buf, hh, 0:1, :]
                l = None
                for cb in range(BAND_BLOCKS):
                    rows = slice(cb * LANES, (cb + 1) * LANES)
                    p = jnp.exp2(s_s[buf, g, rows, lanes] - m)
                    p_s[buf, g, rows, lanes] = p.astype(BF16)
                    lb = jnp.sum(p, axis=0, keepdims=True)
                    l = lb if l is None else l + lb
                linv_s[buf, hh] = jnp.broadcast_to(1.0 / l, (8, QBLK))

    def attend(jb):
        buf = jb % 2
        qrow = jb * QBLK
        for g in range(n_groups):
            for c in range(HEADS_PER_GROUP // 2):
                ha = g * HEADS_PER_GROUP + 2 * c
                ot = jnp.dot(vt_s[ha * HEAD_DIM:(ha + 2) * HEAD_DIM, qrow:qrow + BANDW],
                             p_s[buf, g, :, 2 * c * QBLK:(2 * c + 2) * QBLK], preferred_element_type=F32)
                oa = ot[:, 0:QBLK] * linv_s[buf, ha, 0:1, :]
                ob = ot[:, QBLK:2 * QBLK] * linv_s[buf, ha + 1, 0:1, :]
                a = jnp.where(row_q < HEAD_DIM, oa, ob).T
                c0 = ha * HEAD_DIM
                z = z_s[qrow:qrow + QBLK, c0:c0 + LANES]
                y_s[qrow:qrow + QBLK, c0:c0 + LANES] = (a * _silu(z)).astype(BF16)

    def out_proj(r0):
        o = jnp.dot(y_s[r0:r0 + HALF, :], wout_ref[...], preferred_element_type=F32)
        o_ref[r0:r0 + HALF, :] = x_ref[r0:r0 + HALF, :] + gate * o

    scores(0)
    for jb in range(n_qblk):
        if jb + 1 < n_qblk:
            scores(jb + 1)
        softmax(jb)
        attend(jb)
        if (jb + 1) * QBLK % HALF == 0:
            out_proj((jb + 1) * QBLK - HALF)


def _bias_row_ext(rel_bias):
    assert QBLK <= REL_CLIP <= TILE
    n_heads = rel_bias.shape[0]
    first = rel_bias[:, 0:1].astype(F32)
    return jnp.concatenate([
        jnp.broadcast_to(first, (n_heads, TILE - REL_CLIP)),
        rel_bias[:, 0:REL_CLIP + QBLK].astype(F32),
        jnp.broadcast_to(first, (n_heads, BIAS_EXT - BANDW)),
    ], axis=1)


def _layer(x, mod, norm_g, w_in, q_norm_g, k_norm_g, rel_bias, w_pool, pool_scale, w_out):
    bsz, seq, d_model = x.shape
    n_heads = rel_bias.shape[0]
    attn_w = n_heads * HEAD_DIM
    pool_w = w_pool.shape[0] * w_pool.shape[1]
    pool_group = w_pool.shape[1]
    in_w = w_in.shape[1]
    assert seq % TILE == 0 and attn_w % GROUP_W == 0 and TILE == N_LEFT_CHUNKS * CHUNK
    assert in_w == 3 * attn_w + pool_w + attn_w + pool_w and 2 * pool_group == GROUP_W
    assert attn_w == 512 and pool_w == 512
    n_groups = attn_w // GROUP_W

    ones_bd = jnp.kron(jnp.eye(HEADS_PER_GROUP, dtype=F32), jnp.ones((HEAD_DIM, HEAD_DIM), F32)).astype(BF16)
    n_pairs = w_pool.shape[0] // 2
    wp = w_pool.reshape(n_pairs, 2, pool_group, pool_group)
    zero = jnp.zeros((n_pairs, pool_group, pool_group), F32)
    wp_bd = jnp.concatenate([jnp.concatenate([wp[:, 0], zero], axis=2),
                             jnp.concatenate([zero, wp[:, 1]], axis=2)], axis=1).astype(BF16)

    const2 = lambda b, t: (0, 0)
    const3 = lambda b, t: (0, 0, 0)
    kernel = functools.partial(_block_kernel, d_model=d_model, attn_w=attn_w, pool_w=pool_w)
    return pl.pallas_call(
        kernel,
        grid=(bsz, seq // TILE),
        in_specs=[
            pl.BlockSpec((None, TILE, d_model), lambda b, t: (b, t, 0)),
            pl.BlockSpec((None, 1, 3 * d_model), lambda b, t: (b, 0, 0)),
            pl.BlockSpec((1, d_model), const2),
            pl.BlockSpec((d_model, in_w), const2),
            pl.BlockSpec((attn_w, d_model), const2),
            pl.BlockSpec((attn_w + pool_w, d_model), const2),
            pl.BlockSpec((1, attn_w), const2),
            pl.BlockSpec((1, attn_w), const2),
            pl.BlockSpec((n_heads, BIAS_EXT), const2),
            pl.BlockSpec((GROUP_W, GROUP_W), const2),
            pl.BlockSpec((n_pairs, GROUP_W, GROUP_W), const3),
            pl.BlockSpec((1, pool_w), const2),
        ],
        out_specs=pl.BlockSpec((None, TILE, d_model), lambda b, t: (b, t, 0)),
        out_shape=jax.ShapeDtypeStruct(x.shape, x.dtype),
        scratch_shapes=[
            pltpu.VMEM((TILE, d_model), BF16),
            pltpu.VMEM((TILE, attn_w + pool_w), F32),
            pltpu.VMEM((TILE // QBLK, n_groups, HEADS_PER_GROUP * QBLK, GROUP_W), BF16),
            pltpu.VMEM((2 * TILE, attn_w), BF16),
            pltpu.VMEM((attn_w, 2 * TILE), BF16),
            pltpu.VMEM((TILE + POOL_OFF, pool_w), F32),
            pltpu.VMEM((TILE + POOL_OFF, pool_w), F32),
            pltpu.VMEM((TILE + POOL_OFF, pool_w), F32),
            pltpu.VMEM((TILE + POOL_OFF, pool_w), F32),
            pltpu.VMEM((2, n_groups, BANDW, HEADS_PER_GROUP * QBLK), F32),
            pltpu.VMEM((2, n_groups, BANDW, HEADS_PER_GROUP * QBLK), BF16),
            pltpu.VMEM((2, n_heads, 8, QBLK), F32),
            pltpu.VMEM((2, n_heads, 8, QBLK), F32),
            pltpu.VMEM((TILE, attn_w + pool_w), BF16),
            pltpu.VMEM((n_heads, BAND_BLOCKS + 1, QBLK, LANES), F32),
        ],
        compiler_params=pltpu.CompilerParams(
            dimension_semantics=("arbitrary", "arbitrary"),
            vmem_limit_bytes=V7X_VMEM_LIMIT_BYTES,
        ),
        name="hybrid_block",
    )(x, mod.reshape(bsz, 1, 3 * d_model), norm_g.reshape(1, d_model), w_in.astype(BF16),
      w_in[:, 2 * attn_w:3 * attn_w].T.astype(BF16), w_out.astype(BF16), jnp.tile(q_norm_g, n_heads).reshape(1, attn_w),
      jnp.tile(k_norm_g, n_heads).reshape(1, attn_w), _bias_row_ext(rel_bias), ones_bd, wp_bd,
      pool_scale.reshape(1, pool_w))


def kernel(x, c, norm_g, w_ada, b_ada, w_in, q_norm_g, k_norm_g, rel_bias, w_pool, pool_scale, w_out):
    depth = w_in.shape[0]
    for l in range(depth):
        mod = _adaln_mod(c, w_ada[l], b_ada[l])
        x = _layer(x, mod, norm_g[l], w_in[l], q_norm_g[l], k_norm_g[l], rel_bias[l],
                   w_pool[l], pool_scale[l], w_out[l])
    return x
```

```python
import functools
import math

import jax
import jax.numpy as jnp
from jax import lax
from jax.experimental import pallas as pl
from jax.experimental.pallas import tpu as pltpu

CHUNK = 64
N_LEFT_CHUNKS = 8
HEAD_DIM = 64
REL_CLIP = 256
POOL_WINDOWS = (2, 4, 8, 16)
EPS = 1e-6
NEG_INF = -1e30
LOG2E = math.log2(math.e)

LANES = 128
TILE = 512
HALF = TILE // 2
QBLK = 2 * CHUNK
BANDW = TILE + QBLK
BAND_BLOCKS = BANDW // LANES
BIAS_EXT = BANDW + QBLK
HEADS_PER_GROUP = 4
GROUP_W = HEADS_PER_GROUP * HEAD_DIM
POOL_PAD = 8
POOL_HIST = 16
POOL_OFF = POOL_PAD + POOL_HIST
SOFTMAX_ROWS = 32
V7X_VMEM_LIMIT_BYTES = 56 * 1024 * 1024

F32 = jnp.float32
BF16 = jnp.bfloat16


def _mod_kernel(c_ref, w_ref, b_ref, o_ref):
    o_ref[...] = jnp.dot(c_ref[...], w_ref[...], preferred_element_type=F32) + b_ref[...]


def _adaln_mod(c, w_ada, b_ada):
    bsz, d = c.shape
    n = w_ada.shape[1]
    rows = 8 * ((bsz + 7) // 8)
    c_pad = jnp.zeros((rows, d), F32).at[:bsz].set(c)
    out = pl.pallas_call(
        _mod_kernel,
        grid=(n // d,),
        in_specs=[
            pl.BlockSpec((rows, d), lambda i: (0, 0)),
            pl.BlockSpec((d, d), lambda i: (0, i)),
            pl.BlockSpec((1, d), lambda i: (0, i)),
        ],
        out_specs=pl.BlockSpec((rows, d), lambda i: (0, i)),
        out_shape=jax.ShapeDtypeStruct((rows, n), F32),
        name="adaln_mod",
    )(c_pad, w_ada, b_ada.reshape(1, n))
    return out[:bsz]


def _silu(z):
    return z * (1.0 / (1.0 + jnp.exp(-z)))


def _block_kernel(x_ref, mod_ref, g_ref, win_ref, wvt_ref, wout_ref, gq_ref, gk_ref, rbe_ref,
                  ones_ref, wp_ref, ps_ref, o_ref,
                  h_s, z_s, qbd_s, k_s, vt_s, u_s, t2_s, t4_s, t8_s, s_s, p_s, m_s, linv_s, y_s, bias_ref,
                  *, d_model, attn_w, pool_w):
    t = pl.program_id(1)
    n_heads = attn_w // HEAD_DIM
    n_groups = attn_w // GROUP_W
    o_k, o_v, o_u, o_z = attn_w, 2 * attn_w, 3 * attn_w, 3 * attn_w + pool_w
    pool_group = pool_w // len(POOL_WINDOWS)
    pool_rows = TILE + POOL_OFF

    @pl.when((pl.program_id(0) == 0) & (t == 0))
    def _():
        a = lax.broadcasted_iota(jnp.int32, (QBLK, LANES), 0)
        lane = lax.broadcasted_iota(jnp.int32, (QBLK, LANES), 1)
        band_lo = jnp.where(a >= CHUNK, CHUNK, 0)
        for h in range(n_heads):
            row = jnp.broadcast_to(rbe_ref[h:h + 1, :] * LOG2E, (QBLK, BIAS_EXT))
            skew = pltpu.roll(row, 0, 1, stride=1, stride_axis=0)
            for cb in range(BAND_BLOCKS):
                j = lane + cb * LANES
                visible = (j >= band_lo) & (j < band_lo + (N_LEFT_CHUNKS + 1) * CHUNK)
                bias_ref[h, cb] = jnp.where(visible, skew[:, cb * LANES:(cb + 1) * LANES], NEG_INF).T
            bias_ref[h, BAND_BLOCKS] = jnp.full((QBLK, LANES), NEG_INF, F32)

    @pl.when(t > 0)
    def _():
        k_s[0:TILE, :] = k_s[TILE:2 * TILE, :]
        vt_s[:, 0:TILE] = vt_s[:, TILE:2 * TILE]
        u_s[POOL_PAD:POOL_OFF, :] = u_s[TILE + POOL_PAD:TILE + POOL_OFF, :]

    @pl.when(t == 0)
    def _():
        k_s[0:TILE, :] = jnp.zeros((TILE, attn_w), BF16)
        vt_s[:, 0:TILE] = jnp.zeros((attn_w, TILE), BF16)
        u_s[0:POOL_OFF, :] = jnp.zeros((POOL_OFF, pool_w), F32)
        t2_s[0:POOL_PAD, :] = jnp.zeros((POOL_PAD, pool_w), F32)
        t4_s[0:POOL_PAD, :] = jnp.zeros((POOL_PAD, pool_w), F32)
        t8_s[0:POOL_PAD, :] = jnp.zeros((POOL_PAD, pool_w), F32)

    shift = mod_ref[:, 0:d_model]
    scale = mod_ref[:, d_model:2 * d_model]
    gate = mod_ref[:, 2 * d_model:3 * d_model]
    a_row = g_ref[...] * (1.0 + scale)
    lane_g = lax.broadcasted_iota(jnp.int32, (QBLK, GROUP_W), 1)

    def in_proj(rows, c0, width):
        return jnp.dot(h_s[rows, :], win_ref[:, c0:c0 + width], preferred_element_type=F32)

    def head_rms(xf, gain_row):
        ssq = jnp.dot((xf * xf).astype(BF16), ones_ref[...], preferred_element_type=F32)
        return xf * lax.rsqrt(ssq * (1.0 / HEAD_DIM) + EPS) * gain_row

    for r0 in range(0, TILE, HALF):
        half = slice(r0, r0 + HALF)
        for r in range(r0, r0 + HALF, 64):
            xc = x_ref[r:r + 64, :]
            ms = jnp.mean(xc * xc, axis=-1, keepdims=True)
            hc = xc * lax.rsqrt(ms + EPS) * a_row + shift
            h_s[r:r + 64, :] = hc.astype(BF16)

        qf = in_proj(half, 0, attn_w)
        for rb in range(0, HALF, QBLK):
            for g in range(n_groups):
                cols = slice(g * GROUP_W, (g + 1) * GROUP_W)
                qn = head_rms(qf[rb:rb + QBLK, cols], gq_ref[:, cols] * (HEAD_DIM ** -0.5 * LOG2E))
                for h in range(HEADS_PER_GROUP):
                    keep = (lane_g >= h * HEAD_DIM) & (lane_g < (h + 1) * HEAD_DIM)
                    qbd_s[(r0 + rb) // QBLK, g, h * QBLK:(h + 1) * QBLK, :] = (
                        jnp.where(keep, qn, 0.0).astype(BF16))
        kf = in_proj(half, o_k, attn_w)
        for rb in range(0, HALF, QBLK):
            for g in range(n_groups):
                cols = slice(g * GROUP_W, (g + 1) * GROUP_W)
                kn = head_rms(kf[rb:rb + QBLK, cols], gk_ref[:, cols])
                k_s[TILE + r0 + rb:TILE + r0 + rb + QBLK, cols] = kn.astype(BF16)
        vt_s[:, TILE + r0:TILE + r0 + HALF] = lax.dot_general(
            wvt_ref[...], h_s[half, :], (((1,), (1,)), ((), ())), preferred_element_type=F32).astype(BF16)
        u_s[POOL_OFF + r0:POOL_OFF + r0 + HALF, :] = in_proj(half, o_u, pool_w)
        for c0 in range(0, attn_w + pool_w, 512):
            z_s[half, c0:c0 + 512] = in_proj(half, o_z + c0, 512)

    step = 88
    levels = ((u_s, t2_s, 1), (t2_s, t4_s, 2), (t4_s, t8_s, 4), (t8_s, None, 8))
    for gi in range(len(POOL_WINDOWS)):
        assert POOL_WINDOWS[gi] == 2 ** (gi + 1)
    for li, (src, dst, sh) in enumerate(levels[:-1]):
        lanes = slice((li + 1) * pool_group, pool_w)
        for r in range(POOL_PAD, pool_rows, step):
            dst[r:r + step, lanes] = src[r:r + step, lanes] + src[r - sh:r - sh + step, lanes]
    tok = t * TILE + lax.broadcasted_iota(jnp.int32, (QBLK, pool_group), 0)
    for r in range(0, TILE, QBLK):
        for pair in range(len(POOL_WINDOWS) // 2):
            mixed = []
            for gi in (2 * pair, 2 * pair + 1):
                lanes = slice(gi * pool_group, (gi + 1) * pool_group)
                src = levels[gi][0]
                sh = levels[gi][2]
                rows = slice(POOL_OFF + r, POOL_OFF + r + QBLK)
                wsum = src[rows, lanes] + src[POOL_OFF + r - sh:POOL_OFF + r - sh + QBLK, lanes]
                cnt = jnp.minimum(tok + (r + 1), POOL_WINDOWS[gi]).astype(F32)
                mixed.append((wsum / cnt - u_s[rows, lanes]).astype(BF16))
            pm = jnp.concatenate(mixed, axis=1)
            c0 = pair * 2 * pool_group
            po = jnp.dot(pm, wp_ref[pair], preferred_element_type=F32) * ps_ref[:, c0:c0 + 2 * pool_group]
            z = z_s[r:r + QBLK, attn_w + c0:attn_w + c0 + 2 * pool_group]
            y_s[r:r + QBLK, attn_w + c0:attn_w + c0 + 2 * pool_group] = (po * _silu(z)).astype(BF16)

    row_q = lax.broadcasted_iota(jnp.int32, (QBLK, LANES), 0)
    n_qblk = TILE // QBLK

    def scores(jb):
        buf = jb % 2
        qrow = jb * QBLK
        n_before = TILE // LANES - jb
        blk = [jnp.where(t == 0, BAND_BLOCKS, cb) if cb < n_before else cb for cb in range(BAND_BLOCKS)]
        for g in range(n_groups):
            cols = slice(g * GROUP_W, (g + 1) * GROUP_W)
            st = lax.dot_general(k_s[qrow:qrow + BANDW, cols], qbd_s[jb, g],
                                 (((1,), (1,)), ((), ())), preferred_element_type=F32)
            for h in range(HEADS_PER_GROUP):
                hh = g * HEADS_PER_GROUP + h
                lanes = slice(h * QBLK, (h + 1) * QBLK)
                m = None
                for cb in range(BAND_BLOCKS):
                    rows = slice(cb * LANES, (cb + 1) * LANES)
                    sb = st[rows, lanes] + bias_ref[hh, blk[cb]]
                    s_s[buf, hh, rows, :] = sb
                    mb = jnp.max(sb, axis=0, keepdims=True)
                    m = mb if m is None else jnp.maximum(m, mb)
                m_s[buf, hh] = jnp.broadcast_to(m, (8, QBLK))

    def softmax(jb):
        buf = jb % 2
        for g in range(n_groups):
            for h in range(HEADS_PER_GROUP):
                hh = g * HEADS_PER_GROUP + h
                lanes = slice(h * QBLK, (h + 1) * QBLK)
                m = m_s[buf, hh, 0:1, :]
                l = None
                for cb in range(BAND_BLOCKS):
                    rows = slice(cb * LANES, (cb + 1) * LANES)
                    p = jnp.exp2(s_s[buf, hh, rows, :] - m)
                    p_s[buf, hh // 2, rows, (hh % 2) * QBLK:(hh % 2 + 1) * QBLK] = p.astype(BF16)
                    lb = jnp.sum(p, axis=0, keepdims=True)
                    l = lb if l is None else l + lb
                linv_s[buf, hh] = jnp.broadcast_to(1.0 / l, (8, QBLK))

    def attend(jb):
        buf = jb % 2
        qrow = jb * QBLK
        for g in range(n_groups):
            for c in range(HEADS_PER_GROUP // 2):
                ha = g * HEADS_PER_GROUP + 2 * c
                ot = jnp.dot(vt_s[ha * HEAD_DIM:(ha + 2) * HEAD_DIM, qrow:qrow + BANDW],
                             p_s[buf, ha // 2], preferred_element_type=F32)
                oa = ot[:, 0:QBLK] * linv_s[buf, ha, 0:1, :]
                ob = ot[:, QBLK:2 * QBLK] * linv_s[buf, ha + 1, 0:1, :]
                a = jnp.where(row_q < HEAD_DIM, oa, ob).T
                c0 = ha * HEAD_DIM
                z = z_s[qrow:qrow + QBLK, c0:c0 + LANES]
                y_s[qrow:qrow + QBLK, c0:c0 + LANES] = (a * _silu(z)).astype(BF16)

    def out_proj(r0):
        o = jnp.dot(y_s[r0:r0 + HALF, :], wout_ref[...], preferred_element_type=F32)
        o_ref[r0:r0 + HALF, :] = x_ref[r0:r0 + HALF, :] + gate * o

    scores(0)
    for jb in range(n_qblk):
        if jb + 1 < n_qblk:
            scores(jb + 1)
        softmax(jb)
        attend(jb)
        if (jb + 1) * QBLK % HALF == 0:
            out_proj((jb + 1) * QBLK - HALF)


def _bias_row_ext(rel_bias):
    assert QBLK <= REL_CLIP <= TILE
    n_heads = rel_bias.shape[0]
    first = rel_bias[:, 0:1].astype(F32)
    return jnp.concatenate([
        jnp.broadcast_to(first, (n_heads, TILE - REL_CLIP)),
        rel_bias[:, 0:REL_CLIP + QBLK].astype(F32),
        jnp.broadcast_to(first, (n_heads, BIAS_EXT - BANDW)),
    ], axis=1)


def _layer(x, mod, norm_g, w_in, q_norm_g, k_norm_g, rel_bias, w_pool, pool_scale, w_out):
    bsz, seq, d_model = x.shape
    n_heads = rel_bias.shape[0]
    attn_w = n_heads * HEAD_DIM
    pool_w = w_pool.shape[0] * w_pool.shape[1]
    pool_group = w_pool.shape[1]
    in_w = w_in.shape[1]
    assert seq % TILE == 0 and attn_w % GROUP_W == 0 and TILE == N_LEFT_CHUNKS * CHUNK
    assert in_w == 3 * attn_w + pool_w + attn_w + pool_w and 2 * pool_group == GROUP_W
    assert attn_w == 512 and pool_w == 512
    n_groups = attn_w // GROUP_W

    ones_bd = jnp.kron(jnp.eye(HEADS_PER_GROUP, dtype=F32), jnp.ones((HEAD_DIM, HEAD_DIM), F32)).astype(BF16)
    n_pairs = w_pool.shape[0] // 2
    wp = w_pool.reshape(n_pairs, 2, pool_group, pool_group)
    zero = jnp.zeros((n_pairs, pool_group, pool_group), F32)
    wp_bd = jnp.concatenate([jnp.concatenate([wp[:, 0], zero], axis=2),
                             jnp.concatenate([zero, wp[:, 1]], axis=2)], axis=1).astype(BF16)

    const2 = lambda b, t: (0, 0)
    const3 = lambda b, t: (0, 0, 0)
    kernel = functools.partial(_block_kernel, d_model=d_model, attn_w=attn_w, pool_w=pool_w)
    return pl.pallas_call(
        kernel,
        grid=(bsz, seq // TILE),
        in_specs=[
            pl.BlockSpec((None, TILE, d_model), lambda b, t: (b, t, 0)),
            pl.BlockSpec((None, 1, 3 * d_model), lambda b, t: (b, 0, 0)),
            pl.BlockSpec((1, d_model), const2),
            pl.BlockSpec((d_model, in_w), const2),
            pl.BlockSpec((attn_w, d_model), const2),
            pl.BlockSpec((attn_w + pool_w, d_model), const2),
            pl.BlockSpec((1, attn_w), const2),
            pl.BlockSpec((1, attn_w), const2),
            pl.BlockSpec((n_heads, BIAS_EXT), const2),
            pl.BlockSpec((GROUP_W, GROUP_W), const2),
            pl.BlockSpec((n_pairs, GROUP_W, GROUP_W), const3),
            pl.BlockSpec((1, pool_w), const2),
        ],
        out_specs=pl.BlockSpec((None, TILE, d_model), lambda b, t: (b, t, 0)),
        out_shape=jax.ShapeDtypeStruct(x.shape, x.dtype),
        scratch_shapes=[
            pltpu.VMEM((TILE, d_model), BF16),
            pltpu.VMEM((TILE, attn_w + pool_w), F32),
            pltpu.VMEM((TILE // QBLK, n_groups, HEADS_PER_GROUP * QBLK, GROUP_W), BF16),
            pltpu.VMEM((2 * TILE, attn_w), BF16),
            pltpu.VMEM((attn_w, 2 * TILE), BF16),
            pltpu.VMEM((TILE + POOL_OFF, pool_w), F32),
            pltpu.VMEM((TILE + POOL_OFF, pool_w), F32),
            pltpu.VMEM((TILE + POOL_OFF, pool_w), F32),
            pltpu.VMEM((TILE + POOL_OFF, pool_w), F32),
            pltpu.VMEM((2, n_heads, BANDW, QBLK), F32),
            pltpu.VMEM((2, n_heads // 2, BANDW, 2 * QBLK), BF16),
            pltpu.VMEM((2, n_heads, 8, QBLK), F32),
            pltpu.VMEM((2, n_heads, 8, QBLK), F32),
            pltpu.VMEM((TILE, attn_w + pool_w), BF16),
            pltpu.VMEM((n_heads, BAND_BLOCKS + 1, QBLK, LANES), F32),
        ],
        compiler_params=pltpu.CompilerParams(
            dimension_semantics=("arbitrary", "arbitrary"),
            vmem_limit_bytes=V7X_VMEM_LIMIT_BYTES,
        ),
        name="hybrid_block",
    )(x, mod.reshape(bsz, 1, 3 * d_model), norm_g.reshape(1, d_model), w_in.astype(BF16),
      w_in[:, 2 * attn_w:3 * attn_w].T.astype(BF16), w_out.astype(BF16), jnp.tile(q_norm_g, n_heads).reshape(1, attn_w),
      jnp.tile(k_norm_g, n_heads).reshape(1, attn_w), _bias_row_ext(rel_bias), ones_bd, wp_bd,
      pool_scale.reshape(1, pool_w))


def kernel(x, c, norm_g, w_ada, b_ada, w_in, q_norm_g, k_norm_g, rel_bias, w_pool, pool_scale, w_out):
    depth = w_in.shape[0]
    for l in range(depth):
        mod = _adaln_mod(c, w_ada[l], b_ada[l])
        x = _layer(x, mod, norm_g[l], w_in[l], q_norm_g[l], k_norm_g[l], rel_bias[l],
                   w_pool[l], pool_scale[l], w_out[l])
    return x
```

```python
import functools
import math

import jax
import jax.numpy as jnp
from jax import lax
from jax.experimental import pallas as pl
from jax.experimental.pallas import tpu as pltpu

CHUNK = 64
N_LEFT_CHUNKS = 8
HEAD_DIM = 64
REL_CLIP = 256
POOL_WINDOWS = (2, 4, 8, 16)
EPS = 1e-6
NEG_INF = -1e30
LOG2E = math.log2(math.e)

LANES = 128
TILE = 512
HALF = TILE // 2
QBLK = 2 * CHUNK
BANDW = TILE + QBLK
BAND_BLOCKS = BANDW // LANES
BIAS_EXT = BANDW + QBLK
HEADS_PER_GROUP = 4
GROUP_W = HEADS_PER_GROUP * HEAD_DIM
POOL_PAD = 8
POOL_HIST = 16
POOL_OFF = POOL_PAD + POOL_HIST
ONES_ROWS = 16
SUB_ROWS = 32
NORM_ROWS = 16
V7X_VMEM_LIMIT_BYTES = 56 * 1024 * 1024

F32 = jnp.float32
BF16 = jnp.bfloat16


def _mod_kernel(c_ref, w_ref, b_ref, o_ref):
    o_ref[...] = jnp.dot(c_ref[...], w_ref[...], preferred_element_type=F32) + b_ref[...]


def _adaln_mod(c, w_ada, b_ada):
    bsz, d = c.shape
    n = w_ada.shape[1]
    rows = 8 * ((bsz + 7) // 8)
    c_pad = jnp.zeros((rows, d), F32).at[:bsz].set(c)
    out = pl.pallas_call(
        _mod_kernel,
        grid=(n // d,),
        in_specs=[
            pl.BlockSpec((rows, d), lambda i: (0, 0)),
            pl.BlockSpec((d, d), lambda i: (0, i)),
            pl.BlockSpec((1, d), lambda i: (0, i)),
        ],
        out_specs=pl.BlockSpec((rows, d), lambda i: (0, i)),
        out_shape=jax.ShapeDtypeStruct((rows, n), F32),
        name="adaln_mod",
    )(c_pad, w_ada, b_ada.reshape(1, n))
    return out[:bsz]


def _silu(z):
    return z * (1.0 / (1.0 + jnp.exp(-z)))


def _block_kernel(x_ref, mod_ref, g_ref, win_ref, wvt_ref, wout_ref, gq_ref, gk_ref, rbe_ref,
                  ones_ref, wp_ref, ps_ref, o_ref,
                  h_s, z_s, qbd_s, k_s, vt_s, u_s, t2_s, t4_s, t8_s, s_s, p_s, m_s, y_s, bias_ref,
                  *, d_model, attn_w, pool_w):
    t = pl.program_id(1)
    n_heads = attn_w // HEAD_DIM
    n_groups = attn_w // GROUP_W
    o_k, o_v, o_u, o_z = attn_w, 2 * attn_w, 3 * attn_w, 3 * attn_w + pool_w
    pool_group = pool_w // len(POOL_WINDOWS)
    pool_rows = TILE + POOL_OFF

    @pl.when((pl.program_id(0) == 0) & (t == 0))
    def _():
        a = lax.broadcasted_iota(jnp.int32, (QBLK, LANES), 0)
        lane = lax.broadcasted_iota(jnp.int32, (QBLK, LANES), 1)
        band_lo = jnp.where(a >= CHUNK, CHUNK, 0)
        for h in range(n_heads):
            row = jnp.broadcast_to(rbe_ref[h:h + 1, :] * LOG2E, (QBLK, BIAS_EXT))
            skew = pltpu.roll(row, 0, 1, stride=1, stride_axis=0)
            for cb in range(BAND_BLOCKS):
                j = lane + cb * LANES
                visible = (j >= band_lo) & (j < band_lo + (N_LEFT_CHUNKS + 1) * CHUNK)
                bias_ref[h, cb] = jnp.where(visible, skew[:, cb * LANES:(cb + 1) * LANES], NEG_INF).T
            bias_ref[h, BAND_BLOCKS] = jnp.full((QBLK, LANES), NEG_INF, F32)

    @pl.when(t > 0)
    def _():
        k_s[0:TILE, :] = k_s[TILE:2 * TILE, :]
        vt_s[:, 0:TILE] = vt_s[:, TILE:2 * TILE]
        u_s[POOL_PAD:POOL_OFF, :] = u_s[TILE + POOL_PAD:TILE + POOL_OFF, :]

    @pl.when(t == 0)
    def _():
        k_s[0:TILE, :] = jnp.zeros((TILE, attn_w), BF16)
        vt_s[:, 0:TILE] = jnp.zeros((attn_w, TILE), BF16)
        u_s[0:POOL_OFF, :] = jnp.zeros((POOL_OFF, pool_w), F32)
        t2_s[0:POOL_PAD, :] = jnp.zeros((POOL_PAD, pool_w), F32)
        t4_s[0:POOL_PAD, :] = jnp.zeros((POOL_PAD, pool_w), F32)
        t8_s[0:POOL_PAD, :] = jnp.zeros((POOL_PAD, pool_w), F32)

    shift = mod_ref[:, 0:d_model]
    scale = mod_ref[:, d_model:2 * d_model]
    gate = mod_ref[:, 2 * d_model:3 * d_model]
    a_row = g_ref[...] * (1.0 + scale)
    lane_g = lax.broadcasted_iota(jnp.int32, (SUB_ROWS, GROUP_W), 1)

    def in_proj(rows, c0, width):
        return jnp.dot(h_s[rows, :], win_ref[:, c0:c0 + width], preferred_element_type=F32)

    def head_rms(xf, gain_row):
        ssq = jnp.dot((xf * xf).astype(BF16), ones_ref[...], preferred_element_type=F32)
        for sub in range(0, xf.shape[0], SUB_ROWS):
            rows = slice(sub, sub + SUB_ROWS)
            yield sub, xf[rows] * lax.rsqrt(ssq[rows] * (1.0 / HEAD_DIM) + EPS) * gain_row

    for r0 in range(0, TILE, HALF):
        half = slice(r0, r0 + HALF)
        for r in range(r0, r0 + HALF, NORM_ROWS):
            xc = x_ref[r:r + NORM_ROWS, :]
            ms = jnp.mean(xc * xc, axis=-1, keepdims=True)
            hc = xc * lax.rsqrt(ms + EPS) * a_row + shift
            h_s[r:r + NORM_ROWS, :] = hc.astype(BF16)

        qf = in_proj(half, 0, attn_w)
        for rb in range(0, HALF, QBLK):
            for g in range(n_groups):
                cols = slice(g * GROUP_W, (g + 1) * GROUP_W)
                gain = gq_ref[:, cols] * (HEAD_DIM ** -0.5 * LOG2E)
                for sub, qn in head_rms(qf[rb:rb + QBLK, cols], gain):
                    for h in range(HEADS_PER_GROUP):
                        keep = (lane_g >= h * HEAD_DIM) & (lane_g < (h + 1) * HEAD_DIM)
                        qbd_s[(r0 + rb) // QBLK, g, h * QBLK + sub:h * QBLK + sub + SUB_ROWS, :] = (
                            jnp.where(keep, qn, 0.0).astype(BF16))
        kf = in_proj(half, o_k, attn_w)
        for rb in range(0, HALF, QBLK):
            for g in range(n_groups):
                cols = slice(g * GROUP_W, (g + 1) * GROUP_W)
                for sub, kn in head_rms(kf[rb:rb + QBLK, cols], gk_ref[:, cols]):
                    row = TILE + r0 + rb + sub
                    k_s[row:row + SUB_ROWS, cols] = kn.astype(BF16)
        vt_s[:, TILE + r0:TILE + r0 + HALF] = lax.dot_general(
            wvt_ref[...], h_s[half, :], (((1,), (1,)), ((), ())), preferred_element_type=F32).astype(BF16)
        u_s[POOL_OFF + r0:POOL_OFF + r0 + HALF, :] = in_proj(half, o_u, pool_w)
        for c0 in range(0, attn_w + pool_w, 512):
            z_s[half, c0:c0 + 512] = in_proj(half, o_z + c0, 512)

    step = 88
    levels = ((u_s, t2_s, 1), (t2_s, t4_s, 2), (t4_s, t8_s, 4), (t8_s, None, 8))
    for gi in range(len(POOL_WINDOWS)):
        assert POOL_WINDOWS[gi] == 2 ** (gi + 1)
    for li, (src, dst, sh) in enumerate(levels[:-1]):
        lanes = slice((li + 1) * pool_group, pool_w)
        for r in range(POOL_PAD, pool_rows, step):
            dst[r:r + step, lanes] = src[r:r + step, lanes] + src[r - sh:r - sh + step, lanes]
    tok = t * TILE + lax.broadcasted_iota(jnp.int32, (QBLK, pool_group), 0)
    for r in range(0, TILE, QBLK):
        for pair in range(len(POOL_WINDOWS) // 2):
            mixed = []
            for gi in (2 * pair, 2 * pair + 1):
                lanes = slice(gi * pool_group, (gi + 1) * pool_group)
                src = levels[gi][0]
                sh = levels[gi][2]
                rows = slice(POOL_OFF + r, POOL_OFF + r + QBLK)
                wsum = src[rows, lanes] + src[POOL_OFF + r - sh:POOL_OFF + r - sh + QBLK, lanes]
                cnt = jnp.minimum(tok + (r + 1), POOL_WINDOWS[gi]).astype(F32)
                mixed.append((wsum / cnt - u_s[rows, lanes]).astype(BF16))
            pm = jnp.concatenate(mixed, axis=1)
            c0 = pair * 2 * pool_group
            po = jnp.dot(pm, wp_ref[pair], preferred_element_type=F32) * ps_ref[:, c0:c0 + 2 * pool_group]
            z = z_s[r:r + QBLK, attn_w + c0:attn_w + c0 + 2 * pool_group]
            y_s[r:r + QBLK, attn_w + c0:attn_w + c0 + 2 * pool_group] = (po * _silu(z)).astype(BF16)

    row_q = lax.broadcasted_iota(jnp.int32, (QBLK, LANES), 0)
    n_qblk = TILE // QBLK

    def scores(jb):
        buf = jb % 2
        qrow = jb * QBLK
        n_before = TILE // LANES - jb
        blk = [jnp.where(t == 0, BAND_BLOCKS, cb) if cb < n_before else cb for cb in range(BAND_BLOCKS)]
        for g in range(n_groups):
            cols = slice(g * GROUP_W, (g + 1) * GROUP_W)
            st = lax.dot_general(k_s[qrow:qrow + BANDW, cols], qbd_s[jb, g],
                                 (((1,), (1,)), ((), ())), preferred_element_type=F32)
            for h in range(HEADS_PER_GROUP):
                hh = g * HEADS_PER_GROUP + h
                lanes = slice(h * QBLK, (h + 1) * QBLK)
                macc = None
                for cb in range(BAND_BLOCKS):
                    for sub in range(0, LANES, SUB_ROWS):
                        rows = slice(cb * LANES + sub, cb * LANES + sub + SUB_ROWS)
                        sb = st[rows, lanes] + bias_ref[hh, blk[cb], sub:sub + SUB_ROWS, :]
                        s_s[buf, hh, rows, :] = sb
                        for r8 in range(0, SUB_ROWS, 8):
                            macc = sb[r8:r8 + 8, :] if macc is None else jnp.maximum(macc, sb[r8:r8 + 8, :])
                m_s[buf, hh] = jnp.broadcast_to(jnp.max(macc, axis=0, keepdims=True), (8, QBLK))

    def softmax(jb):
        buf = jb % 2
        for g in range(n_groups):
            for h in range(HEADS_PER_GROUP):
                hh = g * HEADS_PER_GROUP + h
                m = m_s[buf, hh, 0:1, :]
                for r in range(0, BANDW, SUB_ROWS):
                    rows = slice(r, r + SUB_ROWS)
                    p = jnp.exp2(s_s[buf, hh, rows, :] - m)
                    p_s[buf, hh // 2, rows, (hh % 2) * QBLK:(hh % 2 + 1) * QBLK] = p.astype(BF16)

    ones_rows = jnp.ones((ONES_ROWS, BANDW), BF16)

    def attend(jb):
        buf = jb % 2
        qrow = jb * QBLK
        for g in range(n_groups):
            for c in range(HEADS_PER_GROUP // 2):
                ha = g * HEADS_PER_GROUP + 2 * c
                vt1 = jnp.concatenate(
                    [vt_s[ha * HEAD_DIM:(ha + 2) * HEAD_DIM, qrow:qrow + BANDW], ones_rows], axis=0)
                ot = jnp.dot(vt1, p_s[buf, ha // 2], preferred_element_type=F32)
                linv = 1.0 / ot[2 * HEAD_DIM:2 * HEAD_DIM + 1, :]
                oa = ot[0:2 * HEAD_DIM, 0:QBLK] * linv[:, 0:QBLK]
                ob = ot[0:2 * HEAD_DIM, QBLK:2 * QBLK] * linv[:, QBLK:2 * QBLK]
                a = jnp.where(row_q < HEAD_DIM, oa, ob).T
                c0 = ha * HEAD_DIM
                z = z_s[qrow:qrow + QBLK, c0:c0 + LANES]
                y_s[qrow:qrow + QBLK, c0:c0 + LANES] = (a * _silu(z)).astype(BF16)

    def out_proj(r0):
        o = jnp.dot(y_s[r0:r0 + HALF, :], wout_ref[...], preferred_element_type=F32)
        o_ref[r0:r0 + HALF, :] = x_ref[r0:r0 + HALF, :] + gate * o

    scores(0)
    for jb in range(n_qblk):
        if jb + 1 < n_qblk:
            scores(jb + 1)
        softmax(jb)
        attend(jb)
        if (jb + 1) * QBLK % HALF == 0:
            out_proj((jb + 1) * QBLK - HALF)


def _bias_row_ext(rel_bias):
    assert QBLK <= REL_CLIP <= TILE
    n_heads = rel_bias.shape[0]
    first = rel_bias[:, 0:1].astype(F32)
    return jnp.concatenate([
        jnp.broadcast_to(first, (n_heads, TILE - REL_CLIP)),
        rel_bias[:, 0:REL_CLIP + QBLK].astype(F32),
        jnp.broadcast_to(first, (n_heads, BIAS_EXT - BANDW)),
    ], axis=1)


def _layer(x, mod, norm_g, w_in, q_norm_g, k_norm_g, rel_bias, w_pool, pool_scale, w_out):
    bsz, seq, d_model = x.shape
    n_heads = rel_bias.shape[0]
    attn_w = n_heads * HEAD_DIM
    pool_w = w_pool.shape[0] * w_pool.shape[1]
    pool_group = w_pool.shape[1]
    in_w = w_in.shape[1]
    assert seq % TILE == 0 and attn_w % GROUP_W == 0 and TILE == N_LEFT_CHUNKS * CHUNK
    assert in_w == 3 * attn_w + pool_w + attn_w + pool_w and 2 * pool_group == GROUP_W
    assert attn_w == 512 and pool_w == 512
    n_groups = attn_w // GROUP_W

    ones_bd = jnp.kron(jnp.eye(HEADS_PER_GROUP, dtype=F32), jnp.ones((HEAD_DIM, HEAD_DIM), F32)).astype(BF16)
    n_pairs = w_pool.shape[0] // 2
    wp = w_pool.reshape(n_pairs, 2, pool_group, pool_group)
    zero = jnp.zeros((n_pairs, pool_group, pool_group), F32)
    wp_bd = jnp.concatenate([jnp.concatenate([wp[:, 0], zero], axis=2),
                             jnp.concatenate([zero, wp[:, 1]], axis=2)], axis=1).astype(BF16)

    const2 = lambda b, t: (0, 0)
    const3 = lambda b, t: (0, 0, 0)
    kernel = functools.partial(_block_kernel, d_model=d_model, attn_w=attn_w, pool_w=pool_w)
    return pl.pallas_call(
        kernel,
        grid=(bsz, seq // TILE),
        in_specs=[
            pl.BlockSpec((None, TILE, d_model), lambda b, t: (b, t, 0)),
            pl.BlockSpec((None, 1, 3 * d_model), lambda b, t: (b, 0, 0)),
            pl.BlockSpec((1, d_model), const2),
            pl.BlockSpec((d_model, in_w), const2),
            pl.BlockSpec((attn_w, d_model), const2),
            pl.BlockSpec((attn_w + pool_w, d_model), const2),
            pl.BlockSpec((1, attn_w), const2),
            pl.BlockSpec((1, attn_w), const2),
            pl.BlockSpec((n_heads, BIAS_EXT), const2),
            pl.BlockSpec((GROUP_W, GROUP_W), const2),
            pl.BlockSpec((n_pairs, GROUP_W, GROUP_W), const3),
            pl.BlockSpec((1, pool_w), const2),
        ],
        out_specs=pl.BlockSpec((None, TILE, d_model), lambda b, t: (b, t, 0)),
        out_shape=jax.ShapeDtypeStruct(x.shape, x.dtype),
        scratch_shapes=[
            pltpu.VMEM((TILE, d_model), BF16),
            pltpu.VMEM((TILE, attn_w + pool_w), F32),
            pltpu.VMEM((TILE // QBLK, n_groups, HEADS_PER_GROUP * QBLK, GROUP_W), BF16),
            pltpu.VMEM((2 * TILE, attn_w), BF16),
            pltpu.VMEM((attn_w, 2 * TILE), BF16),
            pltpu.VMEM((TILE + POOL_OFF, pool_w), F32),
            pltpu.VMEM((TILE + POOL_OFF, pool_w), F32),
            pltpu.VMEM((TILE + POOL_OFF, pool_w), F32),
            pltpu.VMEM((TILE + POOL_OFF, pool_w), F32),
            pltpu.VMEM((2, n_heads, BANDW, QBLK), F32),
            pltpu.VMEM((2, n_heads // 2, BANDW, 2 * QBLK), BF16),
            pltpu.VMEM((2, n_heads, 8, QBLK), F32),
            pltpu.VMEM((TILE, attn_w + pool_w), BF16),
            pltpu.VMEM((n_heads, BAND_BLOCKS + 1, QBLK, LANES), F32),
        ],
        compiler_params=pltpu.CompilerParams(
            dimension_semantics=("arbitrary", "arbitrary"),
            vmem_limit_bytes=V7X_VMEM_LIMIT_BYTES,
        ),
        name="hybrid_block",
    )(x, mod.reshape(bsz, 1, 3 * d_model), norm_g.reshape(1, d_model), w_in.astype(BF16),
      w_in[:, 2 * attn_w:3 * attn_w].T.astype(BF16), w_out.astype(BF16), jnp.tile(q_norm_g, n_heads).reshape(1, attn_w),
      jnp.tile(k_norm_g, n_heads).reshape(1, attn_w), _bias_row_ext(rel_bias), ones_bd, wp_bd,
      pool_scale.reshape(1, pool_w))


def kernel(x, c, norm_g, w_ada, b_ada, w_in, q_norm_g, k_norm_g, rel_bias, w_pool, pool_scale, w_out):
    depth = w_in.shape[0]
    for l in range(depth):
        mod = _adaln_mod(c, w_ada[l], b_ada[l])
        x = _layer(x, mod, norm_g[l], w_in[l], q_norm_g[l], k_norm_g[l], rel_bias[l],
                   w_pool[l], pool_scale[l], w_out[l])
    return x
```

```python
import functools
import math

import jax
import jax.numpy as jnp
from jax import lax
from jax.experimental import pallas as pl
from jax.experimental.pallas import tpu as pltpu

CHUNK = 64
N_LEFT_CHUNKS = 8
HEAD_DIM = 64
REL_CLIP = 256
POOL_WINDOWS = (2, 4, 8, 16)
EPS = 1e-6
NEG_INF = -1e30
LOG2E = math.log2(math.e)

LANES = 128
TILE = 512
HALF = TILE // 2
QBLK = 2 * CHUNK
BANDW = TILE + QBLK
BAND_BLOCKS = BANDW // LANES
BIAS_EXT = BANDW + QBLK
HEADS_PER_GROUP = 4
GROUP_W = HEADS_PER_GROUP * HEAD_DIM
POOL_PAD = 8
POOL_HIST = 16
POOL_OFF = POOL_PAD + POOL_HIST
ONES_ROWS = 16
SUB_ROWS = 32
NORM_ROWS = 16
V7X_VMEM_LIMIT_BYTES = 56 * 1024 * 1024

F32 = jnp.float32
BF16 = jnp.bfloat16


def _mod_kernel(c_ref, w_ref, b_ref, o_ref):
    o_ref[...] = jnp.dot(c_ref[...], w_ref[...], preferred_element_type=F32) + b_ref[...]


def _adaln_mod(c, w_ada, b_ada):
    bsz, d = c.shape
    n = w_ada.shape[1]
    rows = 8 * ((bsz + 7) // 8)
    c_pad = jnp.zeros((rows, d), F32).at[:bsz].set(c)
    out = pl.pallas_call(
        _mod_kernel,
        grid=(n // d,),
        in_specs=[
            pl.BlockSpec((rows, d), lambda i: (0, 0)),
            pl.BlockSpec((d, d), lambda i: (0, i)),
            pl.BlockSpec((1, d), lambda i: (0, i)),
        ],
        out_specs=pl.BlockSpec((rows, d), lambda i: (0, i)),
        out_shape=jax.ShapeDtypeStruct((rows, n), F32),
        name="adaln_mod",
    )(c_pad, w_ada, b_ada.reshape(1, n))
    return out[:bsz]


def _silu(z):
    return z * (1.0 / (1.0 + jnp.exp(-z)))


def _block_kernel(x_ref, mod_ref, g_ref, win_ref, wvt_ref, wout_ref, gq_ref, gk_ref, rbe_ref,
                  wp_ref, ps_ref, o_ref,
                  h_s, z_s, qbd_s, k_s, vt_s, u_s, t2_s, t4_s, t8_s, s_s, p_s, m_s, y_s, bias_ref,
                  *, d_model, attn_w, pool_w):
    t = pl.program_id(1)
    n_heads = attn_w // HEAD_DIM
    n_groups = attn_w // GROUP_W
    o_k, o_v, o_u, o_z = attn_w, 2 * attn_w, 3 * attn_w, 3 * attn_w + pool_w
    pool_group = pool_w // len(POOL_WINDOWS)
    pool_rows = TILE + POOL_OFF

    @pl.when((pl.program_id(0) == 0) & (t == 0))
    def _():
        a = lax.broadcasted_iota(jnp.int32, (QBLK, LANES), 0)
        lane = lax.broadcasted_iota(jnp.int32, (QBLK, LANES), 1)
        band_lo = jnp.where(a >= CHUNK, CHUNK, 0)
        for h in range(n_heads):
            row = jnp.broadcast_to(rbe_ref[h:h + 1, :] * LOG2E, (QBLK, BIAS_EXT))
            skew = pltpu.roll(row, 0, 1, stride=1, stride_axis=0)
            for cb in range(BAND_BLOCKS):
                j = lane + cb * LANES
                visible = (j >= band_lo) & (j < band_lo + (N_LEFT_CHUNKS + 1) * CHUNK)
                bias_ref[h, cb] = jnp.where(visible, skew[:, cb * LANES:(cb + 1) * LANES], NEG_INF).T
            bias_ref[h, BAND_BLOCKS] = jnp.full((QBLK, LANES), NEG_INF, F32)

    @pl.when(t > 0)
    def _():
        k_s[0:TILE, :] = k_s[TILE:2 * TILE, :]
        vt_s[:, 0:TILE] = vt_s[:, TILE:2 * TILE]
        u_s[POOL_PAD:POOL_OFF, :] = u_s[TILE + POOL_PAD:TILE + POOL_OFF, :]

    @pl.when(t == 0)
    def _():
        k_s[0:TILE, :] = jnp.zeros((TILE, attn_w), BF16)
        vt_s[:, 0:TILE] = jnp.zeros((attn_w, TILE), BF16)
        u_s[0:POOL_OFF, :] = jnp.zeros((POOL_OFF, pool_w), F32)
        t2_s[0:POOL_PAD, :] = jnp.zeros((POOL_PAD, pool_w), F32)
        t4_s[0:POOL_PAD, :] = jnp.zeros((POOL_PAD, pool_w), F32)
        t8_s[0:POOL_PAD, :] = jnp.zeros((POOL_PAD, pool_w), F32)

    shift = mod_ref[:, 0:d_model]
    scale = mod_ref[:, d_model:2 * d_model]
    gate = mod_ref[:, 2 * d_model:3 * d_model]
    a_row = g_ref[...] * (1.0 + scale)
    lane_g = lax.broadcasted_iota(jnp.int32, (SUB_ROWS, GROUP_W), 1)
    lane_h = lax.broadcasted_iota(jnp.int32, (SUB_ROWS, LANES), 1)

    def in_proj(rows, c0, width):
        return jnp.dot(h_s[rows, :], win_ref[:, c0:c0 + width], preferred_element_type=F32)

    def head_rms(xf, gain_row):
        for sub in range(0, xf.shape[0], SUB_ROWS):
            x = xf[sub:sub + SUB_ROWS]
            sq = x * x
            ssq = []
            for c0 in range(0, GROUP_W, LANES):
                blk = sq[:, c0:c0 + LANES]
                first = jnp.sum(jnp.where(lane_h < HEAD_DIM, blk, 0.0), axis=-1, keepdims=True)
                both = jnp.sum(blk, axis=-1, keepdims=True)
                ssq.append(jnp.where(lane_h < HEAD_DIM, first, both - first))
            ssq = jnp.concatenate(ssq, axis=1)
            yield sub, x * lax.rsqrt(ssq * (1.0 / HEAD_DIM) + EPS) * gain_row

    for r0 in range(0, TILE, HALF):
        half = slice(r0, r0 + HALF)
        for r in range(r0, r0 + HALF, NORM_ROWS):
            xc = x_ref[r:r + NORM_ROWS, :]
            ms = jnp.mean(xc * xc, axis=-1, keepdims=True)
            hc = xc * lax.rsqrt(ms + EPS) * a_row + shift
            h_s[r:r + NORM_ROWS, :] = hc.astype(BF16)

        qf = in_proj(half, 0, attn_w)
        for rb in range(0, HALF, QBLK):
            for g in range(n_groups):
                cols = slice(g * GROUP_W, (g + 1) * GROUP_W)
                gain = gq_ref[:, cols] * (HEAD_DIM ** -0.5 * LOG2E)
                for sub, qn in head_rms(qf[rb:rb + QBLK, cols], gain):
                    for h in range(HEADS_PER_GROUP):
                        keep = (lane_g >= h * HEAD_DIM) & (lane_g < (h + 1) * HEAD_DIM)
                        qbd_s[(r0 + rb) // QBLK, g, h * QBLK + sub:h * QBLK + sub + SUB_ROWS, :] = (
                            jnp.where(keep, qn, 0.0).astype(BF16))
        kf = in_proj(half, o_k, attn_w)
        for rb in range(0, HALF, QBLK):
            for g in range(n_groups):
                cols = slice(g * GROUP_W, (g + 1) * GROUP_W)
                for sub, kn in head_rms(kf[rb:rb + QBLK, cols], gk_ref[:, cols]):
                    row = TILE + r0 + rb + sub
                    k_s[row:row + SUB_ROWS, cols] = kn.astype(BF16)
        vt_s[:, TILE + r0:TILE + r0 + HALF] = lax.dot_general(
            wvt_ref[...], h_s[half, :], (((1,), (1,)), ((), ())), preferred_element_type=F32).astype(BF16)
        u_s[POOL_OFF + r0:POOL_OFF + r0 + HALF, :] = in_proj(half, o_u, pool_w)
        for c0 in range(0, attn_w + pool_w, 512):
            z_s[half, c0:c0 + 512] = _silu(in_proj(half, o_z + c0, 512)).astype(BF16)

    step = 88
    levels = ((u_s, t2_s, 1), (t2_s, t4_s, 2), (t4_s, t8_s, 4), (t8_s, None, 8))
    for gi in range(len(POOL_WINDOWS)):
        assert POOL_WINDOWS[gi] == 2 ** (gi + 1)
    for li, (src, dst, sh) in enumerate(levels[:-1]):
        lanes = slice((li + 1) * pool_group, pool_w)
        for r in range(POOL_PAD, pool_rows, step):
            dst[r:r + step, lanes] = src[r:r + step, lanes] + src[r - sh:r - sh + step, lanes]
    tok = t * TILE + lax.broadcasted_iota(jnp.int32, (QBLK, pool_group), 0)
    for r in range(0, TILE, QBLK):
        for pair in range(len(POOL_WINDOWS) // 2):
            mixed = []
            for gi in (2 * pair, 2 * pair + 1):
                lanes = slice(gi * pool_group, (gi + 1) * pool_group)
                src = levels[gi][0]
                sh = levels[gi][2]
                rows = slice(POOL_OFF + r, POOL_OFF + r + QBLK)
                wsum = src[rows, lanes] + src[POOL_OFF + r - sh:POOL_OFF + r - sh + QBLK, lanes]
                cnt = jnp.minimum(tok + (r + 1), POOL_WINDOWS[gi]).astype(F32)
                mixed.append((wsum / cnt - u_s[rows, lanes]).astype(BF16))
            pm = jnp.concatenate(mixed, axis=1)
            c0 = pair * 2 * pool_group
            po = jnp.dot(pm, wp_ref[pair], preferred_element_type=F32) * ps_ref[:, c0:c0 + 2 * pool_group]
            gz = z_s[r:r + QBLK, attn_w + c0:attn_w + c0 + 2 * pool_group].astype(F32)
            y_s[r:r + QBLK, attn_w + c0:attn_w + c0 + 2 * pool_group] = (po * gz).astype(BF16)

    row_q = lax.broadcasted_iota(jnp.int32, (QBLK, LANES), 0)
    n_qblk = TILE // QBLK

    def scores(jb, groups):
        buf = jb % 2
        qrow = jb * QBLK
        n_before = TILE // LANES - jb
        blk = [jnp.where(t == 0, BAND_BLOCKS, cb) if cb < n_before else cb for cb in range(BAND_BLOCKS)]
        for g in groups:
            cols = slice(g * GROUP_W, (g + 1) * GROUP_W)
            st = lax.dot_general(k_s[qrow:qrow + BANDW, cols], qbd_s[jb, g],
                                 (((1,), (1,)), ((), ())), preferred_element_type=F32)
            for h in range(HEADS_PER_GROUP):
                hh = g * HEADS_PER_GROUP + h
                lanes = slice(h * QBLK, (h + 1) * QBLK)
                macc = None
                for cb in range(BAND_BLOCKS):
                    for sub in range(0, LANES, SUB_ROWS):
                        rows = slice(cb * LANES + sub, cb * LANES + sub + SUB_ROWS)
                        sb = st[rows, lanes] + bias_ref[hh, blk[cb], sub:sub + SUB_ROWS, :]
                        s_s[buf, hh, rows, :] = sb
                        for r8 in range(0, SUB_ROWS, 8):
                            macc = sb[r8:r8 + 8, :] if macc is None else jnp.maximum(macc, sb[r8:r8 + 8, :])
                m_s[buf, hh] = jnp.broadcast_to(jnp.max(macc, axis=0, keepdims=True), (8, QBLK))

    def softmax(jb, groups):
        buf = jb % 2
        for g in groups:
            for h in range(HEADS_PER_GROUP):
                hh = g * HEADS_PER_GROUP + h
                m = m_s[buf, hh, 0:1, :]
                for r in range(0, BANDW, SUB_ROWS):
                    rows = slice(r, r + SUB_ROWS)
                    p = jnp.exp2(s_s[buf, hh, rows, :] - m)
                    p_s[buf, hh // 2, rows, (hh % 2) * QBLK:(hh % 2 + 1) * QBLK] = p.astype(BF16)

    ones_rows = jnp.ones((ONES_ROWS, BANDW), BF16)

    def attend(jb, groups):
        buf = jb % 2
        qrow = jb * QBLK
        for g in groups:
            for c in range(HEADS_PER_GROUP // 2):
                ha = g * HEADS_PER_GROUP + 2 * c
                vt1 = jnp.concatenate(
                    [vt_s[ha * HEAD_DIM:(ha + 2) * HEAD_DIM, qrow:qrow + BANDW], ones_rows], axis=0)
                ot = jnp.dot(vt1, p_s[buf, ha // 2], preferred_element_type=F32)
                linv = 1.0 / ot[2 * HEAD_DIM:2 * HEAD_DIM + 1, :]
                oa = ot[0:2 * HEAD_DIM, 0:QBLK] * linv[:, 0:QBLK]
                ob = ot[0:2 * HEAD_DIM, QBLK:2 * QBLK] * linv[:, QBLK:2 * QBLK]
                a = jnp.where(row_q < HEAD_DIM, oa, ob).T
                c0 = ha * HEAD_DIM
                gz = z_s[qrow:qrow + QBLK, c0:c0 + LANES].astype(F32)
                y_s[qrow:qrow + QBLK, c0:c0 + LANES] = (a * gz).astype(BF16)

    def out_proj(r0):
        o = jnp.dot(y_s[r0:r0 + HALF, :], wout_ref[...], preferred_element_type=F32)
        o_ref[r0:r0 + HALF, :] = x_ref[r0:r0 + HALF, :] + gate * o

    all_groups = range(n_groups)
    scores(0, all_groups)
    for jb in range(n_qblk):
        if jb + 1 < n_qblk:
            scores(jb + 1, all_groups)
        softmax(jb, all_groups)
        attend(jb, all_groups)
        if (jb + 1) * QBLK % HALF == 0:
            out_proj((jb + 1) * QBLK - HALF)


def _bias_row_ext(rel_bias):
    assert QBLK <= REL_CLIP <= TILE
    n_heads = rel_bias.shape[0]
    first = rel_bias[:, 0:1].astype(F32)
    return jnp.concatenate([
        jnp.broadcast_to(first, (n_heads, TILE - REL_CLIP)),
        rel_bias[:, 0:REL_CLIP + QBLK].astype(F32),
        jnp.broadcast_to(first, (n_heads, BIAS_EXT - BANDW)),
    ], axis=1)


def _layer(x, mod, norm_g, w_in, q_norm_g, k_norm_g, rel_bias, w_pool, pool_scale, w_out):
    bsz, seq, d_model = x.shape
    n_heads = rel_bias.shape[0]
    attn_w = n_heads * HEAD_DIM
    pool_w = w_pool.shape[0] * w_pool.shape[1]
    pool_group = w_pool.shape[1]
    in_w = w_in.shape[1]
    assert seq % TILE == 0 and attn_w % GROUP_W == 0 and TILE == N_LEFT_CHUNKS * CHUNK
    assert in_w == 3 * attn_w + pool_w + attn_w + pool_w and 2 * pool_group == GROUP_W
    assert attn_w == 512 and pool_w == 512
    n_groups = attn_w // GROUP_W

    n_pairs = w_pool.shape[0] // 2
    wp = w_pool.reshape(n_pairs, 2, pool_group, pool_group)
    zero = jnp.zeros((n_pairs, pool_group, pool_group), F32)
    wp_bd = jnp.concatenate([jnp.concatenate([wp[:, 0], zero], axis=2),
                             jnp.concatenate([zero, wp[:, 1]], axis=2)], axis=1).astype(BF16)

    const2 = lambda b, t: (0, 0)
    const3 = lambda b, t: (0, 0, 0)
    kernel = functools.partial(_block_kernel, d_model=d_model, attn_w=attn_w, pool_w=pool_w)
    return pl.pallas_call(
        kernel,
        grid=(bsz, seq // TILE),
        in_specs=[
            pl.BlockSpec((None, TILE, d_model), lambda b, t: (b, t, 0)),
            pl.BlockSpec((None, 1, 3 * d_model), lambda b, t: (b, 0, 0)),
            pl.BlockSpec((1, d_model), const2),
            pl.BlockSpec((d_model, in_w), const2),
            pl.BlockSpec((attn_w, d_model), const2),
            pl.BlockSpec((attn_w + pool_w, d_model), const2),
            pl.BlockSpec((1, attn_w), const2),
            pl.BlockSpec((1, attn_w), const2),
            pl.BlockSpec((n_heads, BIAS_EXT), const2),
            pl.BlockSpec((n_pairs, GROUP_W, GROUP_W), const3),
            pl.BlockSpec((1, pool_w), const2),
        ],
        out_specs=pl.BlockSpec((None, TILE, d_model), lambda b, t: (b, t, 0)),
        out_shape=jax.ShapeDtypeStruct(x.shape, x.dtype),
        scratch_shapes=[
            pltpu.VMEM((TILE, d_model), BF16),
            pltpu.VMEM((TILE, attn_w + pool_w), BF16),
            pltpu.VMEM((TILE // QBLK, n_groups, HEADS_PER_GROUP * QBLK, GROUP_W), BF16),
            pltpu.VMEM((2 * TILE, attn_w), BF16),
            pltpu.VMEM((attn_w, 2 * TILE), BF16),
            pltpu.VMEM((TILE + POOL_OFF, pool_w), F32),
            pltpu.VMEM((TILE + POOL_OFF, pool_w), F32),
            pltpu.VMEM((TILE + POOL_OFF, pool_w), F32),
            pltpu.VMEM((TILE + POOL_OFF, pool_w), F32),
            pltpu.VMEM((2, n_heads, BANDW, QBLK), F32),
            pltpu.VMEM((2, n_heads // 2, BANDW, 2 * QBLK), BF16),
            pltpu.VMEM((2, n_heads, 8, QBLK), F32),
            pltpu.VMEM((TILE, attn_w + pool_w), BF16),
            pltpu.VMEM((n_heads, BAND_BLOCKS + 1, QBLK, LANES), F32),
        ],
        compiler_params=pltpu.CompilerParams(
            dimension_semantics=("arbitrary", "arbitrary"),
            vmem_limit_bytes=V7X_VMEM_LIMIT_BYTES,
        ),
        name="hybrid_block",
    )(x, mod.reshape(bsz, 1, 3 * d_model), norm_g.reshape(1, d_model), w_in.astype(BF16),
      w_in[:, 2 * attn_w:3 * attn_w].T.astype(BF16), w_out.astype(BF16), jnp.tile(q_norm_g, n_heads).reshape(1, attn_w),
      jnp.tile(k_norm_g, n_heads).reshape(1, attn_w), _bias_row_ext(rel_bias), wp_bd,
      pool_scale.reshape(1, pool_w))


def kernel(x, c, norm_g, w_ada, b_ada, w_in, q_norm_g, k_norm_g, rel_bias, w_pool, pool_scale, w_out):
    depth = w_in.shape[0]
    for l in range(depth):
        mod = _adaln_mod(c, w_ada[l], b_ada[l])
        x = _layer(x, mod, norm_g[l], w_in[l], q_norm_g[l], k_norm_g[l], rel_bias[l],
                   w_pool[l], pool_scale[l], w_out[l])
    return x
```

```python
import functools
import math

import jax
import jax.numpy as jnp
from jax import lax
from jax.experimental import pallas as pl
from jax.experimental.pallas import tpu as pltpu

CHUNK = 64
N_LEFT_CHUNKS = 8
HEAD_DIM = 64
REL_CLIP = 256
POOL_WINDOWS = (2, 4, 8, 16)
EPS = 1e-6
NEG_INF = -1e30
LOG2E = math.log2(math.e)

LANES = 128
TILE = 512
HALF = TILE // 2
QBLK = 2 * CHUNK
BANDW = TILE + QBLK
BAND_BLOCKS = BANDW // LANES
BIAS_EXT = BANDW + QBLK
HEADS_PER_GROUP = 4
GROUP_W = HEADS_PER_GROUP * HEAD_DIM
POOL_PAD = 8
POOL_HIST = 16
POOL_OFF = POOL_PAD + POOL_HIST
ONES_ROWS = 16
SUB_ROWS = 32
NORM_ROWS = 16
MOD_BLOCK_W = 256
V7X_VMEM_LIMIT_BYTES = 56 * 1024 * 1024

F32 = jnp.float32
BF16 = jnp.bfloat16


def _mod_kernel(c_ref, w_ref, b_ref, o_ref):
    res = jnp.dot(c_ref[...], w_ref[...], preferred_element_type=F32) + b_ref[...]
    for b in range(o_ref.shape[0]):
        o_ref[b] = res[b:b + 1, :]


def _adaln_mod(c, w_ada, b_ada):
    bsz, d = c.shape
    n = w_ada.shape[1]
    return pl.pallas_call(
        _mod_kernel,
        grid=(n // MOD_BLOCK_W,),
        in_specs=[
            pl.BlockSpec((bsz, d), lambda i: (0, 0)),
            pl.BlockSpec((d, MOD_BLOCK_W), lambda i: (0, i)),
            pl.BlockSpec((1, MOD_BLOCK_W), lambda i: (0, i)),
        ],
        out_specs=pl.BlockSpec((bsz, 1, MOD_BLOCK_W), lambda i: (0, 0, i)),
        out_shape=jax.ShapeDtypeStruct((bsz, 1, n), F32),
        name="adaln_mod",
    )(c, w_ada, b_ada.reshape(1, n))


def _silu(z):
    return z * (1.0 / (1.0 + jnp.exp(-z)))


def _block_kernel(x_ref, mod_ref, g_ref, win_ref, wout_ref, gq_ref, gk_ref, rb_ref, wpool_ref, ps_ref, o_ref,
                  h_s, z_s, qbd_s, k_s, vt_s, u_s, t2_s, t4_s, t8_s, s_s, p_s, m_s, y_s, bias_ref,
                  wvt_s, gain_s, wp_s,
                  *, d_model, attn_w, pool_w):
    t = pl.program_id(1)
    n_heads = attn_w // HEAD_DIM
    n_groups = attn_w // GROUP_W
    o_k, o_v, o_u, o_z = attn_w, 2 * attn_w, 3 * attn_w, 3 * attn_w + pool_w
    pool_group = pool_w // len(POOL_WINDOWS)
    pool_rows = TILE + POOL_OFF

    @pl.when((pl.program_id(0) == 0) & (t == 0))
    def _():
        a = lax.broadcasted_iota(jnp.int32, (QBLK, LANES), 0)
        lane = lax.broadcasted_iota(jnp.int32, (QBLK, LANES), 1)
        band_lo = jnp.where(a >= CHUNK, CHUNK, 0)
        for h in range(n_heads):
            first = rb_ref[h:h + 1, 0:1] * LOG2E
            row_ext = jnp.concatenate([
                jnp.broadcast_to(first, (1, TILE - REL_CLIP)),
                rb_ref[h:h + 1, 0:REL_CLIP + QBLK] * LOG2E,
                jnp.broadcast_to(first, (1, BIAS_EXT - BANDW))], axis=1)
            row = jnp.broadcast_to(row_ext, (QBLK, BIAS_EXT))
            skew = pltpu.roll(row, 0, 1, stride=1, stride_axis=0)
            for cb in range(BAND_BLOCKS):
                j = lane + cb * LANES
                visible = (j >= band_lo) & (j < band_lo + (N_LEFT_CHUNKS + 1) * CHUNK)
                bias_ref[h, cb] = jnp.where(visible, skew[:, cb * LANES:(cb + 1) * LANES], NEG_INF).T
            bias_ref[h, BAND_BLOCKS] = jnp.full((QBLK, LANES), NEG_INF, F32)
        for r in range(0, d_model, LANES):
            for c in range(0, attn_w, LANES):
                blk = win_ref[r:r + LANES, o_v + c:o_v + c + LANES].astype(F32)
                wvt_s[c:c + LANES, r:r + LANES] = blk.T.astype(BF16)
        gain_s[0] = jnp.broadcast_to(jnp.concatenate([gq_ref[...]] * n_heads, axis=1)
                                     * (HEAD_DIM ** -0.5 * LOG2E), (8, attn_w))
        gain_s[1] = jnp.broadcast_to(jnp.concatenate([gk_ref[...]] * n_heads, axis=1), (8, attn_w))
        wp_s[...] = jnp.zeros(wp_s.shape, BF16)
        for gi in range(len(POOL_WINDOWS)):
            lo = (gi % 2) * pool_group
            wp_s[gi // 2, lo:lo + pool_group, lo:lo + pool_group] = wpool_ref[gi].astype(BF16)

    @pl.when(t > 0)
    def _():
        k_s[0:TILE, :] = k_s[TILE:2 * TILE, :]
        vt_s[:, 0:TILE] = vt_s[:, TILE:2 * TILE]
        u_s[POOL_PAD:POOL_OFF, :] = u_s[TILE + POOL_PAD:TILE + POOL_OFF, :]

    @pl.when(t == 0)
    def _():
        k_s[0:TILE, :] = jnp.zeros((TILE, attn_w), BF16)
        vt_s[:, 0:TILE] = jnp.zeros((attn_w, TILE), BF16)
        u_s[0:POOL_OFF, :] = jnp.zeros((POOL_OFF, pool_w), F32)
        t2_s[0:POOL_PAD, :] = jnp.zeros((POOL_PAD, pool_w), F32)
        t4_s[0:POOL_PAD, :] = jnp.zeros((POOL_PAD, pool_w), F32)
        t8_s[0:POOL_PAD, :] = jnp.zeros((POOL_PAD, pool_w), F32)

    shift = mod_ref[:, 0:d_model]
    scale = mod_ref[:, d_model:2 * d_model]
    gate = mod_ref[:, 2 * d_model:3 * d_model]
    a_row = g_ref[...] * (1.0 + scale)
    lane_g = lax.broadcasted_iota(jnp.int32, (SUB_ROWS, GROUP_W), 1)
    lane_h = lax.broadcasted_iota(jnp.int32, (SUB_ROWS, LANES), 1)

    def in_proj(rows, c0, width):
        return jnp.dot(h_s[rows, :], win_ref[:, c0:c0 + width], preferred_element_type=F32)

    def head_rms(xf, gain_row):
        for sub in range(0, xf.shape[0], SUB_ROWS):
            x = xf[sub:sub + SUB_ROWS]
            sq = x * x
            ssq = []
            for c0 in range(0, GROUP_W, LANES):
                blk = sq[:, c0:c0 + LANES]
                first = jnp.sum(jnp.where(lane_h < HEAD_DIM, blk, 0.0), axis=-1, keepdims=True)
                both = jnp.sum(blk, axis=-1, keepdims=True)
                ssq.append(jnp.where(lane_h < HEAD_DIM, first, both - first))
            ssq = jnp.concatenate(ssq, axis=1)
            yield sub, x * lax.rsqrt(ssq * (1.0 / HEAD_DIM) + EPS) * gain_row

    for r0 in range(0, TILE, HALF):
        half = slice(r0, r0 + HALF)
        for r in range(r0, r0 + HALF, NORM_ROWS):
            xc = x_ref[r:r + NORM_ROWS, :]
            ms = jnp.mean(xc * xc, axis=-1, keepdims=True)
            hc = xc * lax.rsqrt(ms + EPS) * a_row + shift
            h_s[r:r + NORM_ROWS, :] = hc.astype(BF16)

        qf = in_proj(half, 0, attn_w)
        for rb in range(0, HALF, QBLK):
            for g in range(n_groups):
                cols = slice(g * GROUP_W, (g + 1) * GROUP_W)
                gain = gain_s[0, 0:1, cols]
                for sub, qn in head_rms(qf[rb:rb + QBLK, cols], gain):
                    for h in range(HEADS_PER_GROUP):
                        keep = (lane_g >= h * HEAD_DIM) & (lane_g < (h + 1) * HEAD_DIM)
                        qbd_s[(r0 + rb) // QBLK, g, h * QBLK + sub:h * QBLK + sub + SUB_ROWS, :] = (
                            jnp.where(keep, qn, 0.0).astype(BF16))
        kf = in_proj(half, o_k, attn_w)
        for rb in range(0, HALF, QBLK):
            for g in range(n_groups):
                cols = slice(g * GROUP_W, (g + 1) * GROUP_W)
                for sub, kn in head_rms(kf[rb:rb + QBLK, cols], gain_s[1, 0:1, cols]):
                    row = TILE + r0 + rb + sub
                    k_s[row:row + SUB_ROWS, cols] = kn.astype(BF16)
        vt_s[:, TILE + r0:TILE + r0 + HALF] = lax.dot_general(
            wvt_s[...], h_s[half, :], (((1,), (1,)), ((), ())), preferred_element_type=F32).astype(BF16)
        u_s[POOL_OFF + r0:POOL_OFF + r0 + HALF, :] = in_proj(half, o_u, pool_w)
        for c0 in range(0, attn_w + pool_w, 512):
            z_s[half, c0:c0 + 512] = _silu(in_proj(half, o_z + c0, 512)).astype(BF16)

    step = 88
    levels = ((u_s, t2_s, 1), (t2_s, t4_s, 2), (t4_s, t8_s, 4), (t8_s, None, 8))
    for gi in range(len(POOL_WINDOWS)):
        assert POOL_WINDOWS[gi] == 2 ** (gi + 1)
    for li, (src, dst, sh) in enumerate(levels[:-1]):
        lanes = slice((li + 1) * pool_group, pool_w)
        for r in range(POOL_PAD, pool_rows, step):
            dst[r:r + step, lanes] = src[r:r + step, lanes] + src[r - sh:r - sh + step, lanes]
    tok = t * TILE + lax.broadcasted_iota(jnp.int32, (QBLK, pool_group), 0)
    for r in range(0, TILE, QBLK):
        for pair in range(len(POOL_WINDOWS) // 2):
            mixed = []
            for gi in (2 * pair, 2 * pair + 1):
                lanes = slice(gi * pool_group, (gi + 1) * pool_group)
                src = levels[gi][0]
                sh = levels[gi][2]
                rows = slice(POOL_OFF + r, POOL_OFF + r + QBLK)
                wsum = src[rows, lanes] + src[POOL_OFF + r - sh:POOL_OFF + r - sh + QBLK, lanes]
                cnt = jnp.minimum(tok + (r + 1), POOL_WINDOWS[gi]).astype(F32)
                mixed.append((wsum / cnt - u_s[rows, lanes]).astype(BF16))
            pm = jnp.concatenate(mixed, axis=1)
            c0 = pair * 2 * pool_group
            po = jnp.dot(pm, wp_s[pair], preferred_element_type=F32) * ps_ref[:, c0:c0 + 2 * pool_group]
            gz = z_s[r:r + QBLK, attn_w + c0:attn_w + c0 + 2 * pool_group].astype(F32)
            y_s[r:r + QBLK, attn_w + c0:attn_w + c0 + 2 * pool_group] = (po * gz).astype(BF16)

    row_q = lax.broadcasted_iota(jnp.int32, (QBLK, LANES), 0)
    n_qblk = TILE // QBLK

    def scores(jb, groups):
        buf = jb % 2
        qrow = jb * QBLK
        n_before = TILE // LANES - jb
        blk = [jnp.where(t == 0, BAND_BLOCKS, cb) if cb < n_before else cb for cb in range(BAND_BLOCKS)]
        for g in groups:
            cols = slice(g * GROUP_W, (g + 1) * GROUP_W)
            st = lax.dot_general(k_s[qrow:qrow + BANDW, cols], qbd_s[jb, g],
                                 (((1,), (1,)), ((), ())), preferred_element_type=F32)
            for h in range(HEADS_PER_GROUP):
                hh = g * HEADS_PER_GROUP + h
                lanes = slice(h * QBLK, (h + 1) * QBLK)
                macc = None
                for cb in range(BAND_BLOCKS):
                    for sub in range(0, LANES, SUB_ROWS):
                        rows = slice(cb * LANES + sub, cb * LANES + sub + SUB_ROWS)
                        sb = st[rows, lanes] + bias_ref[hh, blk[cb], sub:sub + SUB_ROWS, :]
                        s_s[buf, hh, rows, :] = sb
                        for r8 in range(0, SUB_ROWS, 8):
                            macc = sb[r8:r8 + 8, :] if macc is None else jnp.maximum(macc, sb[r8:r8 + 8, :])
                m_s[buf, hh] = jnp.broadcast_to(jnp.max(macc, axis=0, keepdims=True), (8, QBLK))

    def softmax(jb, groups):
        buf = jb % 2
        for g in groups:
            for h in range(HEADS_PER_GROUP):
                hh = g * HEADS_PER_GROUP + h
                m = m_s[buf, hh, 0:1, :]
                for r in range(0, BANDW, SUB_ROWS):
                    rows = slice(r, r + SUB_ROWS)
                    p = jnp.exp2(s_s[buf, hh, rows, :] - m)
                    p_s[buf, hh // 2, rows, (hh % 2) * QBLK:(hh % 2 + 1) * QBLK] = p.astype(BF16)

    ones_rows = jnp.ones((ONES_ROWS, BANDW), BF16)

    def attend(jb, groups):
        buf = jb % 2
        qrow = jb * QBLK
        for g in groups:
            for c in range(HEADS_PER_GROUP // 2):
                ha = g * HEADS_PER_GROUP + 2 * c
                vt1 = jnp.concatenate(
                    [vt_s[ha * HEAD_DIM:(ha + 2) * HEAD_DIM, qrow:qrow + BANDW], ones_rows], axis=0)
                ot = jnp.dot(vt1, p_s[buf, ha // 2], preferred_element_type=F32)
                linv = 1.0 / ot[2 * HEAD_DIM:2 * HEAD_DIM + 1, :]
                oa = ot[0:2 * HEAD_DIM, 0:QBLK] * linv[:, 0:QBLK]
                ob = ot[0:2 * HEAD_DIM, QBLK:2 * QBLK] * linv[:, QBLK:2 * QBLK]
                a = jnp.where(row_q < HEAD_DIM, oa, ob).T
                c0 = ha * HEAD_DIM
                gz = z_s[qrow:qrow + QBLK, c0:c0 + LANES].astype(F32)
                y_s[qrow:qrow + QBLK, c0:c0 + LANES] = (a * gz).astype(BF16)

    def out_proj(r0):
        o = jnp.dot(y_s[r0:r0 + HALF, :], wout_ref[...], preferred_element_type=F32)
        o_ref[r0:r0 + HALF, :] = x_ref[r0:r0 + HALF, :] + gate * o

    all_groups = range(n_groups)
    scores(0, all_groups)
    for jb in range(n_qblk):
        if jb + 1 < n_qblk:
            scores(jb + 1, all_groups)
        softmax(jb, all_groups)
        attend(jb, all_groups)
        if (jb + 1) * QBLK % HALF == 0:
            out_proj((jb + 1) * QBLK - HALF)


def _layer(x, mod, norm_g, w_in, q_norm_g, k_norm_g, rel_bias, w_pool, pool_scale, w_out):
    bsz, seq, d_model = x.shape
    n_heads = rel_bias.shape[0]
    attn_w = n_heads * HEAD_DIM
    pool_w = w_pool.shape[0] * w_pool.shape[1]
    pool_group = w_pool.shape[1]
    in_w = w_in.shape[1]
    assert seq % TILE == 0 and attn_w % GROUP_W == 0 and TILE == N_LEFT_CHUNKS * CHUNK
    assert in_w == 3 * attn_w + pool_w + attn_w + pool_w and 2 * pool_group == GROUP_W
    assert attn_w == 512 and pool_w == 512
    n_groups = attn_w // GROUP_W

    n_pairs = w_pool.shape[0] // 2
    assert QBLK <= REL_CLIP <= TILE and rel_bias.shape[1] == 2 * REL_CLIP + 1

    const2 = lambda b, t: (0, 0)
    const3 = lambda b, t: (0, 0, 0)
    kernel = functools.partial(_block_kernel, d_model=d_model, attn_w=attn_w, pool_w=pool_w)
    return pl.pallas_call(
        kernel,
        grid=(bsz, seq // TILE),
        in_specs=[
            pl.BlockSpec((None, TILE, d_model), lambda b, t: (b, t, 0)),
            pl.BlockSpec((None, 1, 3 * d_model), lambda b, t: (b, 0, 0)),
            pl.BlockSpec((1, d_model), const2),
            pl.BlockSpec((d_model, in_w), const2),
            pl.BlockSpec((attn_w + pool_w, d_model), const2),
            pl.BlockSpec((1, HEAD_DIM), const2),
            pl.BlockSpec((1, HEAD_DIM), const2),
            pl.BlockSpec((n_heads, 2 * REL_CLIP + 1), const2),
            pl.BlockSpec((2 * n_pairs, pool_group, pool_group), const3),
            pl.BlockSpec((1, pool_w), const2),
        ],
        out_specs=pl.BlockSpec((None, TILE, d_model), lambda b, t: (b, t, 0)),
        out_shape=jax.ShapeDtypeStruct(x.shape, x.dtype),
        scratch_shapes=[
            pltpu.VMEM((TILE, d_model), BF16),
            pltpu.VMEM((TILE, attn_w + pool_w), BF16),
            pltpu.VMEM((TILE // QBLK, n_groups, HEADS_PER_GROUP * QBLK, GROUP_W), BF16),
            pltpu.VMEM((2 * TILE, attn_w), BF16),
            pltpu.VMEM((attn_w, 2 * TILE), BF16),
            pltpu.VMEM((TILE + POOL_OFF, pool_w), F32),
            pltpu.VMEM((TILE + POOL_OFF, pool_w), F32),
            pltpu.VMEM((TILE + POOL_OFF, pool_w), F32),
            pltpu.VMEM((TILE + POOL_OFF, pool_w), F32),
            pltpu.VMEM((2, n_heads, BANDW, QBLK), F32),
            pltpu.VMEM((2, n_heads // 2, BANDW, 2 * QBLK), BF16),
            pltpu.VMEM((2, n_heads, 8, QBLK), F32),
            pltpu.VMEM((TILE, attn_w + pool_w), BF16),
            pltpu.VMEM((n_heads, BAND_BLOCKS + 1, QBLK, LANES), F32),
            pltpu.VMEM((attn_w, d_model), BF16),
            pltpu.VMEM((2, 8, attn_w), F32),
            pltpu.VMEM((n_pairs, GROUP_W, GROUP_W), BF16),
        ],
        compiler_params=pltpu.CompilerParams(
            dimension_semantics=("arbitrary", "arbitrary"),
            vmem_limit_bytes=V7X_VMEM_LIMIT_BYTES,
        ),
        name="hybrid_block",
    )(x, mod, norm_g.reshape(1, d_model), w_in.astype(BF16), w_out.astype(BF16),
      q_norm_g.reshape(1, HEAD_DIM), k_norm_g.reshape(1, HEAD_DIM), rel_bias, w_pool,
      pool_scale.reshape(1, pool_w))


def kernel(x, c, norm_g, w_ada, b_ada, w_in, q_norm_g, k_norm_g, rel_bias, w_pool, pool_scale, w_out):
    depth = w_in.shape[0]
    for l in range(depth):
        mod = _adaln_mod(c, w_ada[l], b_ada[l])
        x = _layer(x, mod, norm_g[l], w_in[l], q_norm_g[l], k_norm_g[l], rel_bias[l],
                   w_pool[l], pool_scale[l], w_out[l])
    return x
```

```python
import functools
import math

import jax
import jax.numpy as jnp
from jax import lax
from jax.experimental import pallas as pl
from jax.experimental.pallas import tpu as pltpu

CHUNK = 64
N_LEFT_CHUNKS = 8
HEAD_DIM = 64
REL_CLIP = 256
POOL_WINDOWS = (2, 4, 8, 16)
EPS = 1e-6
NEG_INF = -1e30
LOG2E = math.log2(math.e)

LANES = 128
TILE = 512
HALF = TILE // 2
QBLK = 2 * CHUNK
BANDW = TILE + QBLK
BAND_BLOCKS = BANDW // LANES
BIAS_EXT = BANDW + QBLK
HEADS_PER_GROUP = 4
GROUP_W = HEADS_PER_GROUP * HEAD_DIM
POOL_PAD = 8
POOL_HIST = 16
POOL_OFF = POOL_PAD + POOL_HIST
ONES_ROWS = 16
SUB_ROWS = 32
NORM_ROWS = 16
MOD_BLOCK_W = 256
V7X_VMEM_LIMIT_BYTES = 56 * 1024 * 1024

F32 = jnp.float32
BF16 = jnp.bfloat16


def _mod_kernel(c_ref, w_ref, b_ref, o_ref):
    res = jnp.dot(c_ref[...], w_ref[...], preferred_element_type=F32) + b_ref[...]
    for b in range(o_ref.shape[0]):
        o_ref[b] = res[b:b + 1, :]


def _adaln_mod(c, w_ada, b_ada):
    bsz, d = c.shape
    n = w_ada.shape[1]
    return pl.pallas_call(
        _mod_kernel,
        grid=(n // MOD_BLOCK_W,),
        in_specs=[
            pl.BlockSpec((bsz, d), lambda i: (0, 0)),
            pl.BlockSpec((d, MOD_BLOCK_W), lambda i: (0, i)),
            pl.BlockSpec((1, MOD_BLOCK_W), lambda i: (0, i)),
        ],
        out_specs=pl.BlockSpec((bsz, 1, MOD_BLOCK_W), lambda i: (0, 0, i)),
        out_shape=jax.ShapeDtypeStruct((bsz, 1, n), F32),
        name="adaln_mod",
    )(c, w_ada, b_ada.reshape(1, n))


def _silu(z):
    return z * (1.0 / (1.0 + jnp.exp(-z)))


def _block_kernel(x_ref, xn_ref, mod_ref, modn_ref, g_ref, win_ref, wout_ref, gq_ref, gk_ref, rb_ref, wpool_ref, ps_ref, o_ref,
                  h_s, z_s, qbd_s, k_s, vt_s, u_s, t2_s, t4_s, t8_s, s_s, p_s, m_s, y_s, bias_ref,
                  wvt_s, gain_s, wp_s,
                  *, d_model, attn_w, pool_w):
    t = pl.program_id(1)
    n_heads = attn_w // HEAD_DIM
    n_groups = attn_w // GROUP_W
    o_k, o_v, o_u, o_z = attn_w, 2 * attn_w, 3 * attn_w, 3 * attn_w + pool_w
    pool_group = pool_w // len(POOL_WINDOWS)
    pool_rows = TILE + POOL_OFF

    @pl.when((pl.program_id(0) == 0) & (t == 0))
    def _():
        a = lax.broadcasted_iota(jnp.int32, (QBLK, LANES), 0)
        lane = lax.broadcasted_iota(jnp.int32, (QBLK, LANES), 1)
        band_lo = jnp.where(a >= CHUNK, CHUNK, 0)
        for h in range(n_heads):
            first = rb_ref[h:h + 1, 0:1] * LOG2E
            row_ext = jnp.concatenate([
                jnp.broadcast_to(first, (1, TILE - REL_CLIP)),
                rb_ref[h:h + 1, 0:REL_CLIP + QBLK] * LOG2E,
                jnp.broadcast_to(first, (1, BIAS_EXT - BANDW))], axis=1)
            row = jnp.broadcast_to(row_ext, (QBLK, BIAS_EXT))
            skew = pltpu.roll(row, 0, 1, stride=1, stride_axis=0)
            for cb in range(BAND_BLOCKS):
                j = lane + cb * LANES
                visible = (j >= band_lo) & (j < band_lo + (N_LEFT_CHUNKS + 1) * CHUNK)
                bias_ref[h, cb] = jnp.where(visible, skew[:, cb * LANES:(cb + 1) * LANES], NEG_INF).T
            bias_ref[h, BAND_BLOCKS] = jnp.full((QBLK, LANES), NEG_INF, F32)
        for r in range(0, d_model, LANES):
            for c in range(0, attn_w, LANES):
                blk = win_ref[r:r + LANES, o_v + c:o_v + c + LANES].astype(F32)
                wvt_s[c:c + LANES, r:r + LANES] = blk.T.astype(BF16)
        gain_s[0] = jnp.broadcast_to(jnp.concatenate([gq_ref[...]] * n_heads, axis=1)
                                     * (HEAD_DIM ** -0.5 * LOG2E), (8, attn_w))
        gain_s[1] = jnp.broadcast_to(jnp.concatenate([gk_ref[...]] * n_heads, axis=1), (8, attn_w))
        wp_s[...] = jnp.zeros(wp_s.shape, BF16)
        for gi in range(len(POOL_WINDOWS)):
            lo = (gi % 2) * pool_group
            wp_s[gi // 2, lo:lo + pool_group, lo:lo + pool_group] = wpool_ref[gi].astype(BF16)

    @pl.when(t > 0)
    def _():
        k_s[0:TILE, :] = k_s[TILE:2 * TILE, :]
        vt_s[:, 0:TILE] = vt_s[:, TILE:2 * TILE]
        u_s[POOL_PAD:POOL_OFF, :] = u_s[TILE + POOL_PAD:TILE + POOL_OFF, :]

    @pl.when(t == 0)
    def _():
        k_s[0:TILE, :] = jnp.zeros((TILE, attn_w), BF16)
        vt_s[:, 0:TILE] = jnp.zeros((attn_w, TILE), BF16)
        u_s[0:POOL_OFF, :] = jnp.zeros((POOL_OFF, pool_w), F32)
        t2_s[0:POOL_PAD, :] = jnp.zeros((POOL_PAD, pool_w), F32)
        t4_s[0:POOL_PAD, :] = jnp.zeros((POOL_PAD, pool_w), F32)
        t8_s[0:POOL_PAD, :] = jnp.zeros((POOL_PAD, pool_w), F32)

    gate = mod_ref[:, 2 * d_model:3 * d_model]

    def norm_rows(src_ref, m_ref, r):
        a_row = g_ref[...] * (1.0 + m_ref[:, d_model:2 * d_model])
        xc = src_ref[r:r + NORM_ROWS, :]
        ms = jnp.mean(xc * xc, axis=-1, keepdims=True)
        hc = xc * lax.rsqrt(ms + EPS) * a_row + m_ref[:, 0:d_model]
        h_s[r:r + NORM_ROWS, :] = hc.astype(BF16)
        return hc[0:1, 0:LANES]

    @pl.when((pl.program_id(0) == 0) & (t == 0))
    def _():
        for r in range(0, TILE, NORM_ROWS):
            norm_rows(x_ref, mod_ref, r)

    lane_g = lax.broadcasted_iota(jnp.int32, (SUB_ROWS, GROUP_W), 1)
    lane_h = lax.broadcasted_iota(jnp.int32, (SUB_ROWS, LANES), 1)

    def in_proj(rows, c0, width):
        return jnp.dot(h_s[rows, :], win_ref[:, c0:c0 + width], preferred_element_type=F32)

    def head_rms(xf, gain_row):
        for sub in range(0, xf.shape[0], SUB_ROWS):
            x = xf[sub:sub + SUB_ROWS]
            sq = x * x
            ssq = []
            for c0 in range(0, GROUP_W, LANES):
                blk = sq[:, c0:c0 + LANES]
                first = jnp.sum(jnp.where(lane_h < HEAD_DIM, blk, 0.0), axis=-1, keepdims=True)
                both = jnp.sum(blk, axis=-1, keepdims=True)
                ssq.append(jnp.where(lane_h < HEAD_DIM, first, both - first))
            ssq = jnp.concatenate(ssq, axis=1)
            yield sub, x * lax.rsqrt(ssq * (1.0 / HEAD_DIM) + EPS) * gain_row

    tile_rows = slice(0, TILE)
    qf = in_proj(tile_rows, 0, attn_w)
    for rb in range(0, TILE, QBLK):
        for g in range(n_groups):
            cols = slice(g * GROUP_W, (g + 1) * GROUP_W)
            gain = gain_s[0, 0:1, cols]
            for sub, qn in head_rms(qf[rb:rb + QBLK, cols], gain):
                for h in range(HEADS_PER_GROUP):
                    keep = (lane_g >= h * HEAD_DIM) & (lane_g < (h + 1) * HEAD_DIM)
                    qbd_s[rb // QBLK, g, h * QBLK + sub:h * QBLK + sub + SUB_ROWS, :] = (
                        jnp.where(keep, qn, 0.0).astype(BF16))
    kf = in_proj(tile_rows, o_k, attn_w)
    for rb in range(0, TILE, QBLK):
        for g in range(n_groups):
            cols = slice(g * GROUP_W, (g + 1) * GROUP_W)
            for sub, kn in head_rms(kf[rb:rb + QBLK, cols], gain_s[1, 0:1, cols]):
                row = TILE + rb + sub
                k_s[row:row + SUB_ROWS, cols] = kn.astype(BF16)
    vt_s[:, TILE:2 * TILE] = lax.dot_general(
        wvt_s[...], h_s[...], (((1,), (1,)), ((), ())), preferred_element_type=F32).astype(BF16)
    u_s[POOL_OFF:POOL_OFF + TILE, :] = in_proj(tile_rows, o_u, pool_w)
    for c0 in range(0, attn_w + pool_w, 512):
        z_s[:, c0:c0 + 512] = _silu(in_proj(tile_rows, o_z + c0, 512)).astype(BF16)

    step = 88
    levels = ((u_s, t2_s, 1), (t2_s, t4_s, 2), (t4_s, t8_s, 4), (t8_s, None, 8))
    for gi in range(len(POOL_WINDOWS)):
        assert POOL_WINDOWS[gi] == 2 ** (gi + 1)
    for li, (src, dst, sh) in enumerate(levels[:-1]):
        lanes = slice((li + 1) * pool_group, pool_w)
        for r in range(POOL_PAD, pool_rows, step):
            dst[r:r + step, lanes] = src[r:r + step, lanes] + src[r - sh:r - sh + step, lanes]
    tok = t * TILE + lax.broadcasted_iota(jnp.int32, (QBLK, pool_group), 0)
    for r in range(0, TILE, QBLK):
        for pair in range(len(POOL_WINDOWS) // 2):
            mixed = []
            for gi in (2 * pair, 2 * pair + 1):
                lanes = slice(gi * pool_group, (gi + 1) * pool_group)
                src = levels[gi][0]
                sh = levels[gi][2]
                rows = slice(POOL_OFF + r, POOL_OFF + r + QBLK)
                wsum = src[rows, lanes] + src[POOL_OFF + r - sh:POOL_OFF + r - sh + QBLK, lanes]
                cnt = jnp.minimum(tok + (r + 1), POOL_WINDOWS[gi]).astype(F32)
                mixed.append((wsum / cnt - u_s[rows, lanes]).astype(BF16))
            pm = jnp.concatenate(mixed, axis=1)
            c0 = pair * 2 * pool_group
            po = jnp.dot(pm, wp_s[pair], preferred_element_type=F32) * ps_ref[:, c0:c0 + 2 * pool_group]
            gz = z_s[r:r + QBLK, attn_w + c0:attn_w + c0 + 2 * pool_group].astype(F32)
            y_s[r:r + QBLK, attn_w + c0:attn_w + c0 + 2 * pool_group] = (po * gz).astype(BF16)

    row_q = lax.broadcasted_iota(jnp.int32, (QBLK, LANES), 0)
    n_qblk = TILE // QBLK

    def scores(jb, groups):
        buf = jb % 2
        qrow = jb * QBLK
        n_before = TILE // LANES - jb
        blk = [jnp.where(t == 0, BAND_BLOCKS, cb) if cb < n_before else cb for cb in range(BAND_BLOCKS)]
        for g in groups:
            cols = slice(g * GROUP_W, (g + 1) * GROUP_W)
            st = lax.dot_general(k_s[qrow:qrow + BANDW, cols], qbd_s[jb, g],
                                 (((1,), (1,)), ((), ())), preferred_element_type=F32)
            for h in range(HEADS_PER_GROUP):
                hh = g * HEADS_PER_GROUP + h
                lanes = slice(h * QBLK, (h + 1) * QBLK)
                macc = None
                for cb in range(BAND_BLOCKS):
                    for sub in range(0, LANES, SUB_ROWS):
                        rows = slice(cb * LANES + sub, cb * LANES + sub + SUB_ROWS)
                        sb = st[rows, lanes] + bias_ref[hh, blk[cb], sub:sub + SUB_ROWS, :]
                        s_s[buf, hh, rows, :] = sb
                        for r8 in range(0, SUB_ROWS, 8):
                            macc = sb[r8:r8 + 8, :] if macc is None else jnp.maximum(macc, sb[r8:r8 + 8, :])
                m_s[buf, hh] = jnp.broadcast_to(jnp.max(macc, axis=0, keepdims=True), (8, QBLK))

    def softmax(jb, groups):
        buf = jb % 2
        for g in groups:
            for h in range(HEADS_PER_GROUP):
                hh = g * HEADS_PER_GROUP + h
                m = m_s[buf, hh, 0:1, :]
                for r in range(0, BANDW, SUB_ROWS):
                    rows = slice(r, r + SUB_ROWS)
                    p = jnp.exp2(s_s[buf, hh, rows, :] - m)
                    p_s[buf, hh // 2, rows, (hh % 2) * QBLK:(hh % 2 + 1) * QBLK] = p.astype(BF16)

    ones_rows = jnp.ones((ONES_ROWS, BANDW), BF16)

    def attend(jb, groups):
        buf = jb % 2
        qrow = jb * QBLK
        for g in groups:
            for c in range(HEADS_PER_GROUP // 2):
                ha = g * HEADS_PER_GROUP + 2 * c
                vt1 = jnp.concatenate(
                    [vt_s[ha * HEAD_DIM:(ha + 2) * HEAD_DIM, qrow:qrow + BANDW], ones_rows], axis=0)
                ot = jnp.dot(vt1, p_s[buf, ha // 2], preferred_element_type=F32)
                linv = 1.0 / ot[2 * HEAD_DIM:2 * HEAD_DIM + 1, :]
                oa = ot[0:2 * HEAD_DIM, 0:QBLK] * linv[:, 0:QBLK]
                ob = ot[0:2 * HEAD_DIM, QBLK:2 * QBLK] * linv[:, QBLK:2 * QBLK]
                a = jnp.where(row_q < HEAD_DIM, oa, ob).T
                c0 = ha * HEAD_DIM
                gz = z_s[qrow:qrow + QBLK, c0:c0 + LANES].astype(F32)
                y_s[qrow:qrow + QBLK, c0:c0 + LANES] = (a * gz).astype(BF16)

    all_groups = range(n_groups)
    scores(0, all_groups)
    for jb in range(n_qblk):
        if jb + 1 < n_qblk:
            scores(jb + 1, all_groups)
        softmax(jb, all_groups)
        attend(jb, all_groups)

    o = jnp.dot(y_s[...], wout_ref[...], preferred_element_type=F32)
    never = t < 0
    for r in range(0, TILE, SUB_ROWS):
        tie = sum(norm_rows(xn_ref, modn_ref, rn) for rn in range(r, r + SUB_ROWS, NORM_ROWS))
        gate_r = jnp.concatenate([jnp.where(never, tie, gate[:, 0:LANES]), gate[:, LANES:]], axis=1)
        o_ref[r:r + SUB_ROWS, :] = x_ref[r:r + SUB_ROWS, :] + gate_r * o[r:r + SUB_ROWS, :]


def _layer(x, mod, norm_g, w_in, q_norm_g, k_norm_g, rel_bias, w_pool, pool_scale, w_out):
    bsz, seq, d_model = x.shape
    n_heads = rel_bias.shape[0]
    attn_w = n_heads * HEAD_DIM
    pool_w = w_pool.shape[0] * w_pool.shape[1]
    pool_group = w_pool.shape[1]
    in_w = w_in.shape[1]
    assert seq % TILE == 0 and attn_w % GROUP_W == 0 and TILE == N_LEFT_CHUNKS * CHUNK
    assert in_w == 3 * attn_w + pool_w + attn_w + pool_w and 2 * pool_group == GROUP_W
    assert attn_w == 512 and pool_w == 512
    n_groups = attn_w // GROUP_W

    n_pairs = w_pool.shape[0] // 2
    assert QBLK <= REL_CLIP <= TILE and rel_bias.shape[1] == 2 * REL_CLIP + 1

    tiles_per_seq = seq // TILE

    def next_tile(b, t):
        n = jnp.minimum(b * tiles_per_seq + t + 1, bsz * tiles_per_seq - 1)
        return n // tiles_per_seq, n % tiles_per_seq

    const2 = lambda b, t: (0, 0)
    const3 = lambda b, t: (0, 0, 0)
    kernel = functools.partial(_block_kernel, d_model=d_model, attn_w=attn_w, pool_w=pool_w)
    return pl.pallas_call(
        kernel,
        grid=(bsz, seq // TILE),
        in_specs=[
            pl.BlockSpec((None, TILE, d_model), lambda b, t: (b, t, 0)),
            pl.BlockSpec((None, TILE, d_model), lambda b, t: (*next_tile(b, t), 0)),
            pl.BlockSpec((None, 1, 3 * d_model), lambda b, t: (b, 0, 0)),
            pl.BlockSpec((None, 1, 3 * d_model), lambda b, t: (next_tile(b, t)[0], 0, 0)),
            pl.BlockSpec((1, d_model), const2),
            pl.BlockSpec((d_model, in_w), const2),
            pl.BlockSpec((attn_w + pool_w, d_model), const2),
            pl.BlockSpec((1, HEAD_DIM), const2),
            pl.BlockSpec((1, HEAD_DIM), const2),
            pl.BlockSpec((n_heads, 2 * REL_CLIP + 1), const2),
            pl.BlockSpec((2 * n_pairs, pool_group, pool_group), const3),
            pl.BlockSpec((1, pool_w), const2),
        ],
        out_specs=pl.BlockSpec((None, TILE, d_model), lambda b, t: (b, t, 0)),
        out_shape=jax.ShapeDtypeStruct(x.shape, x.dtype),
        scratch_shapes=[
            pltpu.VMEM((TILE, d_model), BF16),
            pltpu.VMEM((TILE, attn_w + pool_w), BF16),
            pltpu.VMEM((TILE // QBLK, n_groups, HEADS_PER_GROUP * QBLK, GROUP_W), BF16),
            pltpu.VMEM((2 * TILE, attn_w), BF16),
            pltpu.VMEM((attn_w, 2 * TILE), BF16),
            pltpu.VMEM((TILE + POOL_OFF, pool_w), F32),
            pltpu.VMEM((TILE + POOL_OFF, pool_w), F32),
            pltpu.VMEM((TILE + POOL_OFF, pool_w), F32),
            pltpu.VMEM((TILE + POOL_OFF, pool_w), F32),
            pltpu.VMEM((2, n_heads, BANDW, QBLK), F32),
            pltpu.VMEM((2, n_heads // 2, BANDW, 2 * QBLK), BF16),
            pltpu.VMEM((2, n_heads, 8, QBLK), F32),
            pltpu.VMEM((TILE, attn_w + pool_w), BF16),
            pltpu.VMEM((n_heads, BAND_BLOCKS + 1, QBLK, LANES), F32),
            pltpu.VMEM((attn_w, d_model), BF16),
            pltpu.VMEM((2, 8, attn_w), F32),
            pltpu.VMEM((n_pairs, GROUP_W, GROUP_W), BF16),
        ],
        compiler_params=pltpu.CompilerParams(
            dimension_semantics=("arbitrary", "arbitrary"),
            vmem_limit_bytes=V7X_VMEM_LIMIT_BYTES,
        ),
        name="hybrid_block",
    )(x, x, mod, mod, norm_g.reshape(1, d_model), w_in.astype(BF16), w_out.astype(BF16),
      q_norm_g.reshape(1, HEAD_DIM), k_norm_g.reshape(1, HEAD_DIM), rel_bias, w_pool,
      pool_scale.reshape(1, pool_w))


def kernel(x, c, norm_g, w_ada, b_ada, w_in, q_norm_g, k_norm_g, rel_bias, w_pool, pool_scale, w_out):
    depth = w_in.shape[0]
    for l in range(depth):
        mod = _adaln_mod(c, w_ada[l], b_ada[l])
        x = _layer(x, mod, norm_g[l], w_in[l], q_norm_g[l], k_norm_g[l], rel_bias[l],
                   w_pool[l], pool_scale[l], w_out[l])
    return x
```

```python
import functools
import math

import jax
import jax.numpy as jnp
from jax import lax
from jax.experimental import pallas as pl
from jax.experimental.pallas import tpu as pltpu

CHUNK = 64
N_LEFT_CHUNKS = 8
HEAD_DIM = 64
REL_CLIP = 256
POOL_WINDOWS = (2, 4, 8, 16)
EPS = 1e-6
NEG_INF = -1e30
LOG2E = math.log2(math.e)

LANES = 128
TILE = 512
HALF = TILE // 2
QBLK = 2 * CHUNK
BANDW = TILE + QBLK
BAND_BLOCKS = BANDW // LANES
BIAS_EXT = BANDW + QBLK
HEADS_PER_GROUP = 4
GROUP_W = HEADS_PER_GROUP * HEAD_DIM
POOL_PAD = 8
POOL_HIST = 16
POOL_OFF = POOL_PAD + POOL_HIST
ONES_ROWS = 16
SUB_ROWS = 32
NORM_ROWS = 16
MOD_BLOCK_W = 1024
V7X_VMEM_LIMIT_BYTES = 56 * 1024 * 1024

F32 = jnp.float32
BF16 = jnp.bfloat16


def _mod_kernel(c_ref, w_ref, b_ref, o_ref):
    res = jnp.dot(c_ref[...], w_ref[...], preferred_element_type=F32) + b_ref[...]
    for b in range(o_ref.shape[0]):
        o_ref[b] = res[b:b + 1, :]


def _adaln_mod(c, w_ada, b_ada):
    bsz, d = c.shape
    n = w_ada.shape[1]
    return pl.pallas_call(
        _mod_kernel,
        grid=(n // MOD_BLOCK_W,),
        in_specs=[
            pl.BlockSpec((bsz, d), lambda i: (0, 0)),
            pl.BlockSpec((d, MOD_BLOCK_W), lambda i: (0, i)),
            pl.BlockSpec((1, MOD_BLOCK_W), lambda i: (0, i)),
        ],
        out_specs=pl.BlockSpec((bsz, 1, MOD_BLOCK_W), lambda i: (0, 0, i)),
        out_shape=jax.ShapeDtypeStruct((bsz, 1, n), F32),
        name="adaln_mod",
    )(c, w_ada, b_ada.reshape(1, n))


def _silu(z):
    return z * (1.0 / (1.0 + jnp.exp(-z)))


def _block_kernel(x_ref, xn_ref, mod_ref, modn_ref, g_ref, win_ref, wout_ref, gq_ref, gk_ref, rb_ref, wpool_ref, ps_ref, o_ref,
                  h_s, z_s, qbd_s, k_s, vt_s, u_s, t2_s, t4_s, t8_s, s_s, p_s, m_s, y_s, bias_ref,
                  wvt_s, gain_s, wp_s,
                  *, d_model, attn_w, pool_w):
    t = pl.program_id(1)
    n_heads = attn_w // HEAD_DIM
    n_groups = attn_w // GROUP_W
    o_k, o_v, o_u, o_z = attn_w, 2 * attn_w, 3 * attn_w, 3 * attn_w + pool_w
    pool_group = pool_w // len(POOL_WINDOWS)
    pool_rows = TILE + POOL_OFF

    @pl.when((pl.program_id(0) == 0) & (t == 0))
    def _():
        a = lax.broadcasted_iota(jnp.int32, (QBLK, LANES), 0)
        lane = lax.broadcasted_iota(jnp.int32, (QBLK, LANES), 1)
        band_lo = jnp.where(a >= CHUNK, CHUNK, 0)
        for h in range(n_heads):
            first = rb_ref[h:h + 1, 0:1] * LOG2E
            row_ext = jnp.concatenate([
                jnp.broadcast_to(first, (1, TILE - REL_CLIP)),
                rb_ref[h:h + 1, 0:REL_CLIP + QBLK] * LOG2E,
                jnp.broadcast_to(first, (1, BIAS_EXT - BANDW))], axis=1)
            row = jnp.broadcast_to(row_ext, (QBLK, BIAS_EXT))
            skew = pltpu.roll(row, 0, 1, stride=1, stride_axis=0)
            for cb in range(BAND_BLOCKS):
                j = lane + cb * LANES
                visible = (j >= band_lo) & (j < band_lo + (N_LEFT_CHUNKS + 1) * CHUNK)
                bias_ref[h, cb] = jnp.where(visible, skew[:, cb * LANES:(cb + 1) * LANES], NEG_INF).T
            bias_ref[h, BAND_BLOCKS] = jnp.full((QBLK, LANES), NEG_INF, F32)
        for r in range(0, d_model, LANES):
            for c in range(0, attn_w, LANES):
                blk = win_ref[r:r + LANES, o_v + c:o_v + c + LANES].astype(F32)
                wvt_s[c:c + LANES, r:r + LANES] = blk.T.astype(BF16)
        gain_s[0] = jnp.broadcast_to(jnp.concatenate([gq_ref[...]] * n_heads, axis=1)
                                     * (HEAD_DIM ** -0.5 * LOG2E), (8, attn_w))
        gain_s[1] = jnp.broadcast_to(jnp.concatenate([gk_ref[...]] * n_heads, axis=1), (8, attn_w))
        k_s[...] = jnp.zeros(k_s.shape, BF16)
        vt_s[...] = jnp.zeros(vt_s.shape, BF16)
        wp_s[...] = jnp.zeros(wp_s.shape, BF16)
        for gi in range(len(POOL_WINDOWS)):
            lo = (gi % 2) * pool_group
            wp_s[gi // 2, lo:lo + pool_group, lo:lo + pool_group] = wpool_ref[gi].astype(BF16)

    @pl.when(t > 0)
    def _():
        u_s[POOL_PAD:POOL_OFF, :] = u_s[TILE + POOL_PAD:TILE + POOL_OFF, :]

    @pl.when(t == 0)
    def _():
        u_s[0:POOL_OFF, :] = jnp.zeros((POOL_OFF, pool_w), F32)
        t2_s[0:POOL_PAD, :] = jnp.zeros((POOL_PAD, pool_w), F32)
        t4_s[0:POOL_PAD, :] = jnp.zeros((POOL_PAD, pool_w), F32)
        t8_s[0:POOL_PAD, :] = jnp.zeros((POOL_PAD, pool_w), F32)

    k_s[0:TILE, :] = k_s[TILE:2 * TILE, :]
    vt_s[:, 0:TILE] = vt_s[:, TILE:2 * TILE]

    gate = mod_ref[:, 2 * d_model:3 * d_model]

    def norm_rows(src_ref, m_ref, r):
        a_row = g_ref[...] * (1.0 + m_ref[:, d_model:2 * d_model])
        xc = src_ref[r:r + NORM_ROWS, :]
        ms = jnp.mean(xc * xc, axis=-1, keepdims=True)
        hc = xc * lax.rsqrt(ms + EPS) * a_row + m_ref[:, 0:d_model]
        h_s[r:r + NORM_ROWS, :] = hc.astype(BF16)
        return hc[0:1, 0:LANES]

    @pl.when((pl.program_id(0) == 0) & (t == 0))
    def _():
        for r in range(0, TILE, NORM_ROWS):
            norm_rows(x_ref, mod_ref, r)

    lane_g = lax.broadcasted_iota(jnp.int32, (SUB_ROWS, GROUP_W), 1)
    lane_h = lax.broadcasted_iota(jnp.int32, (SUB_ROWS, LANES), 1)

    def in_proj(rows, c0, width):
        return jnp.dot(h_s[rows, :], win_ref[:, c0:c0 + width], preferred_element_type=F32)

    def head_rms(xf, gain_row):
        for sub in range(0, xf.shape[0], SUB_ROWS):
            x = xf[sub:sub + SUB_ROWS]
            sq = x * x
            ssq = []
            for c0 in range(0, GROUP_W, LANES):
                blk = sq[:, c0:c0 + LANES]
                first = jnp.sum(jnp.where(lane_h < HEAD_DIM, blk, 0.0), axis=-1, keepdims=True)
                both = jnp.sum(blk, axis=-1, keepdims=True)
                ssq.append(jnp.where(lane_h < HEAD_DIM, first, both - first))
            ssq = jnp.concatenate(ssq, axis=1)
            yield sub, x * lax.rsqrt(ssq * (1.0 / HEAD_DIM) + EPS) * gain_row

    tile_rows = slice(0, TILE)
    qf = in_proj(tile_rows, 0, attn_w)
    for rb in range(0, TILE, QBLK):
        for g in range(n_groups):
            cols = slice(g * GROUP_W, (g + 1) * GROUP_W)
            gain = gain_s[0, 0:1, cols]
            for sub, qn in head_rms(qf[rb:rb + QBLK, cols], gain):
                for h in range(HEADS_PER_GROUP):
                    keep = (lane_g >= h * HEAD_DIM) & (lane_g < (h + 1) * HEAD_DIM)
                    qbd_s[rb // QBLK, g, h * QBLK + sub:h * QBLK + sub + SUB_ROWS, :] = (
                        jnp.where(keep, qn, 0.0).astype(BF16))
    kf = in_proj(tile_rows, o_k, attn_w)
    for rb in range(0, TILE, QBLK):
        for g in range(n_groups):
            cols = slice(g * GROUP_W, (g + 1) * GROUP_W)
            for sub, kn in head_rms(kf[rb:rb + QBLK, cols], gain_s[1, 0:1, cols]):
                row = TILE + rb + sub
                k_s[row:row + SUB_ROWS, cols] = kn.astype(BF16)
    vt_s[:, TILE:2 * TILE] = lax.dot_general(
        wvt_s[...], h_s[...], (((1,), (1,)), ((), ())), preferred_element_type=F32).astype(BF16)
    u_s[POOL_OFF:POOL_OFF + TILE, :] = in_proj(tile_rows, o_u, pool_w)
    for c0 in range(0, attn_w + pool_w, 512):
        z_s[:, c0:c0 + 512] = _silu(in_proj(tile_rows, o_z + c0, 512)).astype(BF16)

    step = 88
    levels = ((u_s, t2_s, 1), (t2_s, t4_s, 2), (t4_s, t8_s, 4), (t8_s, None, 8))
    for gi in range(len(POOL_WINDOWS)):
        assert POOL_WINDOWS[gi] == 2 ** (gi + 1)
    for li, (src, dst, sh) in enumerate(levels[:-1]):
        lanes = slice((li + 1) * pool_group, pool_w)
        for r in range(POOL_PAD, pool_rows, step):
            dst[r:r + step, lanes] = src[r:r + step, lanes] + src[r - sh:r - sh + step, lanes]
    tok = t * TILE + lax.broadcasted_iota(jnp.int32, (QBLK, pool_group), 0)
    for r in range(0, TILE, QBLK):
        for pair in range(len(POOL_WINDOWS) // 2):
            mixed = []
            for gi in (2 * pair, 2 * pair + 1):
                lanes = slice(gi * pool_group, (gi + 1) * pool_group)
                src = levels[gi][0]
                sh = levels[gi][2]
                rows = slice(POOL_OFF + r, POOL_OFF + r + QBLK)
                wsum = src[rows, lanes] + src[POOL_OFF + r - sh:POOL_OFF + r - sh + QBLK, lanes]
                cnt = jnp.minimum(tok + (r + 1), POOL_WINDOWS[gi]).astype(F32)
                mixed.append((wsum / cnt - u_s[rows, lanes]).astype(BF16))
            pm = jnp.concatenate(mixed, axis=1)
            c0 = pair * 2 * pool_group
            po = jnp.dot(pm, wp_s[pair], preferred_element_type=F32) * ps_ref[:, c0:c0 + 2 * pool_group]
            gz = z_s[r:r + QBLK, attn_w + c0:attn_w + c0 + 2 * pool_group].astype(F32)
            y_s[r:r + QBLK, attn_w + c0:attn_w + c0 + 2 * pool_group] = (po * gz).astype(BF16)

    row_q = lax.broadcasted_iota(jnp.int32, (QBLK, LANES), 0)
    n_qblk = TILE // QBLK

    def scores(jb, groups):
        buf = jb % 2
        qrow = jb * QBLK
        n_before = TILE // LANES - jb
        blk = [jnp.where(t == 0, BAND_BLOCKS, cb) if cb < n_before else cb for cb in range(BAND_BLOCKS)]
        for g in groups:
            cols = slice(g * GROUP_W, (g + 1) * GROUP_W)
            st = lax.dot_general(k_s[qrow:qrow + BANDW, cols], qbd_s[jb, g],
                                 (((1,), (1,)), ((), ())), preferred_element_type=F32)
            for h in range(HEADS_PER_GROUP):
                hh = g * HEADS_PER_GROUP + h
                lanes = slice(h * QBLK, (h + 1) * QBLK)
                macc = None
                for cb in range(BAND_BLOCKS):
                    for sub in range(0, LANES, SUB_ROWS):
                        rows = slice(cb * LANES + sub, cb * LANES + sub + SUB_ROWS)
                        sb = st[rows, lanes] + bias_ref[hh, blk[cb], sub:sub + SUB_ROWS, :]
                        s_s[buf, hh, rows, :] = sb
                        for r8 in range(0, SUB_ROWS, 8):
                            macc = sb[r8:r8 + 8, :] if macc is None else jnp.maximum(macc, sb[r8:r8 + 8, :])
                m_s[buf, hh] = jnp.broadcast_to(jnp.max(macc, axis=0, keepdims=True), (8, QBLK))

    def softmax(jb, groups):
        buf = jb % 2
        for g in groups:
            for h in range(HEADS_PER_GROUP):
                hh = g * HEADS_PER_GROUP + h
                m = m_s[buf, hh, 0:1, :]
                for r in range(0, BANDW, SUB_ROWS):
                    rows = slice(r, r + SUB_ROWS)
                    p = jnp.exp2(s_s[buf, hh, rows, :] - m)
                    p_s[buf, hh // 2, rows, (hh % 2) * QBLK:(hh % 2 + 1) * QBLK] = p.astype(BF16)

    ones_rows = jnp.ones((ONES_ROWS, BANDW), BF16)

    def attend(jb, groups):
        buf = jb % 2
        qrow = jb * QBLK
        for g in groups:
            for c in range(HEADS_PER_GROUP // 2):
                ha = g * HEADS_PER_GROUP + 2 * c
                vt1 = jnp.concatenate(
                    [vt_s[ha * HEAD_DIM:(ha + 2) * HEAD_DIM, qrow:qrow + BANDW], ones_rows], axis=0)
                ot = jnp.dot(vt1, p_s[buf, ha // 2], preferred_element_type=F32)
                linv = 1.0 / ot[2 * HEAD_DIM:2 * HEAD_DIM + 1, :]
                oa = ot[0:2 * HEAD_DIM, 0:QBLK] * linv[:, 0:QBLK]
                ob = ot[0:2 * HEAD_DIM, QBLK:2 * QBLK] * linv[:, QBLK:2 * QBLK]
                a = jnp.where(row_q < HEAD_DIM, oa, ob).T
                c0 = ha * HEAD_DIM
                gz = z_s[qrow:qrow + QBLK, c0:c0 + LANES].astype(F32)
                y_s[qrow:qrow + QBLK, c0:c0 + LANES] = (a * gz).astype(BF16)

    all_groups = range(n_groups)
    scores(0, all_groups)
    for jb in range(n_qblk):
        if jb + 1 < n_qblk:
            scores(jb + 1, all_groups)
        softmax(jb, all_groups)
        attend(jb, all_groups)

    o = jnp.dot(y_s[...], wout_ref[...], preferred_element_type=F32)
    never = t < 0
    for r in range(0, TILE, SUB_ROWS):
        tie = sum(norm_rows(xn_ref, modn_ref, rn) for rn in range(r, r + SUB_ROWS, NORM_ROWS))
        gate_r = jnp.concatenate([jnp.where(never, tie, gate[:, 0:LANES]), gate[:, LANES:]], axis=1)
        o_ref[r:r + SUB_ROWS, :] = x_ref[r:r + SUB_ROWS, :] + gate_r * o[r:r + SUB_ROWS, :]


def _layer(x, mod, norm_g, w_in, q_norm_g, k_norm_g, rel_bias, w_pool, pool_scale, w_out):
    bsz, seq, d_model = x.shape
    n_heads = rel_bias.shape[0]
    attn_w = n_heads * HEAD_DIM
    pool_w = w_pool.shape[0] * w_pool.shape[1]
    pool_group = w_pool.shape[1]
    in_w = w_in.shape[1]
    assert seq % TILE == 0 and attn_w % GROUP_W == 0 and TILE == N_LEFT_CHUNKS * CHUNK
    assert in_w == 3 * attn_w + pool_w + attn_w + pool_w and 2 * pool_group == GROUP_W
    assert attn_w == 512 and pool_w == 512
    n_groups = attn_w // GROUP_W

    n_pairs = w_pool.shape[0] // 2
    assert QBLK <= REL_CLIP <= TILE and rel_bias.shape[1] == 2 * REL_CLIP + 1

    tiles_per_seq = seq // TILE

    def next_tile(b, t):
        n = jnp.minimum(b * tiles_per_seq + t + 1, bsz * tiles_per_seq - 1)
        return n // tiles_per_seq, n % tiles_per_seq

    const2 = lambda b, t: (0, 0)
    const3 = lambda b, t: (0, 0, 0)
    kernel = functools.partial(_block_kernel, d_model=d_model, attn_w=attn_w, pool_w=pool_w)
    return pl.pallas_call(
        kernel,
        grid=(bsz, seq // TILE),
        in_specs=[
            pl.BlockSpec((None, TILE, d_model), lambda b, t: (b, t, 0)),
            pl.BlockSpec((None, TILE, d_model), lambda b, t: (*next_tile(b, t), 0)),
            pl.BlockSpec((None, 1, 3 * d_model), lambda b, t: (b, 0, 0)),
            pl.BlockSpec((None, 1, 3 * d_model), lambda b, t: (next_tile(b, t)[0], 0, 0)),
            pl.BlockSpec((1, d_model), const2),
            pl.BlockSpec((d_model, in_w), const2),
            pl.BlockSpec((attn_w + pool_w, d_model), const2),
            pl.BlockSpec((1, HEAD_DIM), const2),
            pl.BlockSpec((1, HEAD_DIM), const2),
            pl.BlockSpec((n_heads, 2 * REL_CLIP + 1), const2),
            pl.BlockSpec((2 * n_pairs, pool_group, pool_group), const3),
            pl.BlockSpec((1, pool_w), const2),
        ],
        out_specs=pl.BlockSpec((None, TILE, d_model), lambda b, t: (b, t, 0)),
        out_shape=jax.ShapeDtypeStruct(x.shape, x.dtype),
        scratch_shapes=[
            pltpu.VMEM((TILE, d_model), BF16),
            pltpu.VMEM((TILE, attn_w + pool_w), BF16),
            pltpu.VMEM((TILE // QBLK, n_groups, HEADS_PER_GROUP * QBLK, GROUP_W), BF16),
            pltpu.VMEM((2 * TILE, attn_w), BF16),
            pltpu.VMEM((attn_w, 2 * TILE), BF16),
            pltpu.VMEM((TILE + POOL_OFF, pool_w), F32),
            pltpu.VMEM((TILE + POOL_OFF, pool_w), F32),
            pltpu.VMEM((TILE + POOL_OFF, pool_w), F32),
            pltpu.VMEM((TILE + POOL_OFF, pool_w), F32),
            pltpu.VMEM((2, n_heads, BANDW, QBLK), F32),
            pltpu.VMEM((2, n_heads // 2, BANDW, 2 * QBLK), BF16),
            pltpu.VMEM((2, n_heads, 8, QBLK), F32),
            pltpu.VMEM((TILE, attn_w + pool_w), BF16),
            pltpu.VMEM((n_heads, BAND_BLOCKS + 1, QBLK, LANES), F32),
            pltpu.VMEM((attn_w, d_model), BF16),
            pltpu.VMEM((2, 8, attn_w), F32),
            pltpu.VMEM((n_pairs, GROUP_W, GROUP_W), BF16),
        ],
        compiler_params=pltpu.CompilerParams(
            dimension_semantics=("arbitrary", "arbitrary"),
            vmem_limit_bytes=V7X_VMEM_LIMIT_BYTES,
        ),
        name="hybrid_block",
    )(x, x, mod, mod, norm_g.reshape(1, d_model), w_in.astype(BF16), w_out.astype(BF16),
      q_norm_g.reshape(1, HEAD_DIM), k_norm_g.reshape(1, HEAD_DIM), rel_bias, w_pool,
      pool_scale.reshape(1, pool_w))


def kernel(x, c, norm_g, w_ada, b_ada, w_in, q_norm_g, k_norm_g, rel_bias, w_pool, pool_scale, w_out):
    depth = w_in.shape[0]
    for l in range(depth):
        mod = _adaln_mod(c, w_ada[l], b_ada[l])
        x = _layer(x, mod, norm_g[l], w_in[l], q_norm_g[l], k_norm_g[l], rel_bias[l],
                   w_pool[l], pool_scale[l], w_out[l])
    return x
```

```python
import functools
import math

import jax
import jax.numpy as jnp
from jax import lax
from jax.experimental import pallas as pl
from jax.experimental.pallas import tpu as pltpu

CHUNK = 64
N_LEFT_CHUNKS = 8
HEAD_DIM = 64
REL_CLIP = 256
POOL_WINDOWS = (2, 4, 8, 16)
EPS = 1e-6
NEG_INF = -1e30
LOG2E = math.log2(math.e)

LANES = 128
TILE = 512
HALF = TILE // 2
QBLK = 2 * CHUNK
BANDW = TILE + QBLK
BAND_BLOCKS = BANDW // LANES
BIAS_EXT = BANDW + QBLK
HEADS_PER_GROUP = 4
GROUP_W = HEADS_PER_GROUP * HEAD_DIM
POOL_PAD = 8
POOL_HIST = 16
POOL_OFF = POOL_PAD + POOL_HIST
ONES_ROWS = 16
SUB_ROWS = 32
NORM_ROWS = 16
BOUND_MARGIN = 1.02
L_FLOOR = 2.0 ** -90
MOD_BLOCK_W = 1024
V7X_VMEM_LIMIT_BYTES = 56 * 1024 * 1024

F32 = jnp.float32
BF16 = jnp.bfloat16


def _mod_kernel(c_ref, w_ref, b_ref, o_ref):
    res = jnp.dot(c_ref[...], w_ref[...], preferred_element_type=F32) + b_ref[...]
    for b in range(o_ref.shape[0]):
        o_ref[b] = res[b:b + 1, :]


def _adaln_mod(c, w_ada, b_ada):
    bsz, d = c.shape
    n = w_ada.shape[1]
    return pl.pallas_call(
        _mod_kernel,
        grid=(n // MOD_BLOCK_W,),
        in_specs=[
            pl.BlockSpec((bsz, d), lambda i: (0, 0)),
            pl.BlockSpec((d, MOD_BLOCK_W), lambda i: (0, i)),
            pl.BlockSpec((1, MOD_BLOCK_W), lambda i: (0, i)),
        ],
        out_specs=pl.BlockSpec((bsz, 1, MOD_BLOCK_W), lambda i: (0, 0, i)),
        out_shape=jax.ShapeDtypeStruct((bsz, 1, n), F32),
        name="adaln_mod",
    )(c, w_ada, b_ada.reshape(1, n))


def _silu(z):
    return z * (1.0 / (1.0 + jnp.exp(-z)))


def _block_kernel(x_ref, xn_ref, mod_ref, modn_ref, g_ref, win_ref, wout_ref, gq_ref, gk_ref, rb_ref, wpool_ref, ps_ref, o_ref,
                  h_s, z_s, qbd_s, k_s, vt_s, u_s, t2_s, t4_s, t8_s, s_s, p_s, y_s, bias_ref, biasx_ref,
                  wvt_s, gain_s, wp_s,
                  *, d_model, attn_w, pool_w):
    t = pl.program_id(1)
    n_heads = attn_w // HEAD_DIM
    n_groups = attn_w // GROUP_W
    o_k, o_v, o_u, o_z = attn_w, 2 * attn_w, 3 * attn_w, 3 * attn_w + pool_w
    pool_group = pool_w // len(POOL_WINDOWS)
    pool_rows = TILE + POOL_OFF

    @pl.when((pl.program_id(0) == 0) & (t == 0))
    def _():
        a = lax.broadcasted_iota(jnp.int32, (QBLK, LANES), 0)
        lane = lax.broadcasted_iota(jnp.int32, (QBLK, LANES), 1)
        band_lo = jnp.where(a >= CHUNK, CHUNK, 0)
        gq_max = jnp.max(jnp.abs(gq_ref[...]), axis=-1, keepdims=True)
        gk_max = jnp.max(jnp.abs(gk_ref[...]), axis=-1, keepdims=True)
        dot_bound = gq_max * gk_max * (HEAD_DIM ** 0.5 * LOG2E * BOUND_MARGIN)
        for h in range(n_heads):
            first = rb_ref[h:h + 1, 0:1] * LOG2E
            row_ext = jnp.concatenate([
                jnp.broadcast_to(first, (1, TILE - REL_CLIP)),
                rb_ref[h:h + 1, 0:REL_CLIP + QBLK] * LOG2E,
                jnp.broadcast_to(first, (1, BIAS_EXT - BANDW))], axis=1)
            row = jnp.broadcast_to(row_ext, (QBLK, BIAS_EXT))
            skew = pltpu.roll(row, 0, 1, stride=1, stride_axis=0)
            bound = dot_bound + jnp.max(rb_ref[h:h + 1, :], axis=-1, keepdims=True) * LOG2E + 1.0
            for cb in range(BAND_BLOCKS):
                j = lane + cb * LANES
                visible = (j >= band_lo) & (j < band_lo + (N_LEFT_CHUNKS + 1) * CHUNK)
                blk = skew[:, cb * LANES:(cb + 1) * LANES]
                biasx_ref[h, cb] = jnp.where(visible, blk, NEG_INF).T
                bias_ref[h, cb] = jnp.where(visible, blk - bound, NEG_INF).T
            biasx_ref[h, BAND_BLOCKS] = jnp.full((QBLK, LANES), NEG_INF, F32)
            bias_ref[h, BAND_BLOCKS] = jnp.full((QBLK, LANES), NEG_INF, F32)
        for r in range(0, d_model, LANES):
            for c in range(0, attn_w, LANES):
                blk = win_ref[r:r + LANES, o_v + c:o_v + c + LANES].astype(F32)
                wvt_s[c:c + LANES, r:r + LANES] = blk.T.astype(BF16)
        gain_s[0] = jnp.broadcast_to(jnp.concatenate([gq_ref[...]] * n_heads, axis=1)
                                     * (HEAD_DIM ** -0.5 * LOG2E), (8, attn_w))
        gain_s[1] = jnp.broadcast_to(jnp.concatenate([gk_ref[...]] * n_heads, axis=1), (8, attn_w))
        k_s[...] = jnp.zeros(k_s.shape, BF16)
        vt_s[...] = jnp.zeros(vt_s.shape, BF16)
        wp_s[...] = jnp.zeros(wp_s.shape, BF16)
        for gi in range(len(POOL_WINDOWS)):
            lo = (gi % 2) * pool_group
            wp_s[gi // 2, lo:lo + pool_group, lo:lo + pool_group] = wpool_ref[gi].astype(BF16)

    @pl.when(t > 0)
    def _():
        u_s[POOL_PAD:POOL_OFF, :] = u_s[TILE + POOL_PAD:TILE + POOL_OFF, :]

    @pl.when(t == 0)
    def _():
        u_s[0:POOL_OFF, :] = jnp.zeros((POOL_OFF, pool_w), F32)
        t2_s[0:POOL_PAD, :] = jnp.zeros((POOL_PAD, pool_w), F32)
        t4_s[0:POOL_PAD, :] = jnp.zeros((POOL_PAD, pool_w), F32)
        t8_s[0:POOL_PAD, :] = jnp.zeros((POOL_PAD, pool_w), F32)

    k_s[0:TILE, :] = k_s[TILE:2 * TILE, :]
    vt_s[:, 0:TILE] = vt_s[:, TILE:2 * TILE]

    gate = mod_ref[:, 2 * d_model:3 * d_model]

    def norm_rows(src_ref, m_ref, r):
        a_row = g_ref[...] * (1.0 + m_ref[:, d_model:2 * d_model])
        xc = src_ref[r:r + NORM_ROWS, :]
        ms = jnp.mean(xc * xc, axis=-1, keepdims=True)
        hc = xc * lax.rsqrt(ms + EPS) * a_row + m_ref[:, 0:d_model]
        h_s[r:r + NORM_ROWS, :] = hc.astype(BF16)
        return hc[0:1, 0:LANES]

    @pl.when((pl.program_id(0) == 0) & (t == 0))
    def _():
        for r in range(0, TILE, NORM_ROWS):
            norm_rows(x_ref, mod_ref, r)

    lane_g = lax.broadcasted_iota(jnp.int32, (SUB_ROWS, GROUP_W), 1)
    lane_h = lax.broadcasted_iota(jnp.int32, (SUB_ROWS, LANES), 1)

    def in_proj(rows, c0, width):
        return jnp.dot(h_s[rows, :], win_ref[:, c0:c0 + width], preferred_element_type=F32)

    def head_rms(xf, gain_row):
        for sub in range(0, xf.shape[0], SUB_ROWS):
            x = xf[sub:sub + SUB_ROWS]
            sq = x * x
            ssq = []
            for c0 in range(0, GROUP_W, LANES):
                blk = sq[:, c0:c0 + LANES]
                first = jnp.sum(jnp.where(lane_h < HEAD_DIM, blk, 0.0), axis=-1, keepdims=True)
                both = jnp.sum(blk, axis=-1, keepdims=True)
                ssq.append(jnp.where(lane_h < HEAD_DIM, first, both - first))
            ssq = jnp.concatenate(ssq, axis=1)
            yield sub, x * lax.rsqrt(ssq * (1.0 / HEAD_DIM) + EPS) * gain_row

    tile_rows = slice(0, TILE)
    qf = in_proj(tile_rows, 0, attn_w)
    for rb in range(0, TILE, QBLK):
        for g in range(n_groups):
            cols = slice(g * GROUP_W, (g + 1) * GROUP_W)
            gain = gain_s[0, 0:1, cols]
            for sub, qn in head_rms(qf[rb:rb + QBLK, cols], gain):
                for h in range(HEADS_PER_GROUP):
                    keep = (lane_g >= h * HEAD_DIM) & (lane_g < (h + 1) * HEAD_DIM)
                    qbd_s[rb // QBLK, g, h * QBLK + sub:h * QBLK + sub + SUB_ROWS, :] = (
                        jnp.where(keep, qn, 0.0).astype(BF16))
    kf = in_proj(tile_rows, o_k, attn_w)
    for rb in range(0, TILE, QBLK):
        for g in range(n_groups):
            cols = slice(g * GROUP_W, (g + 1) * GROUP_W)
            for sub, kn in head_rms(kf[rb:rb + QBLK, cols], gain_s[1, 0:1, cols]):
                row = TILE + rb + sub
                k_s[row:row + SUB_ROWS, cols] = kn.astype(BF16)
    vt_s[:, TILE:2 * TILE] = lax.dot_general(
        wvt_s[...], h_s[...], (((1,), (1,)), ((), ())), preferred_element_type=F32).astype(BF16)
    u_s[POOL_OFF:POOL_OFF + TILE, :] = in_proj(tile_rows, o_u, pool_w)
    for c0 in range(0, attn_w + pool_w, 512):
        z_s[:, c0:c0 + 512] = _silu(in_proj(tile_rows, o_z + c0, 512)).astype(BF16)

    step = 88
    levels = ((u_s, t2_s, 1), (t2_s, t4_s, 2), (t4_s, t8_s, 4), (t8_s, None, 8))
    for gi in range(len(POOL_WINDOWS)):
        assert POOL_WINDOWS[gi] == 2 ** (gi + 1)
    for li, (src, dst, sh) in enumerate(levels[:-1]):
        lanes = slice((li + 1) * pool_group, pool_w)
        for r in range(POOL_PAD, pool_rows, step):
            dst[r:r + step, lanes] = src[r:r + step, lanes] + src[r - sh:r - sh + step, lanes]
    tok = t * TILE + lax.broadcasted_iota(jnp.int32, (QBLK, pool_group), 0)
    for r in range(0, TILE, QBLK):
        for pair in range(len(POOL_WINDOWS) // 2):
            mixed = []
            for gi in (2 * pair, 2 * pair + 1):
                lanes = slice(gi * pool_group, (gi + 1) * pool_group)
                src = levels[gi][0]
                sh = levels[gi][2]
                rows = slice(POOL_OFF + r, POOL_OFF + r + QBLK)
                wsum = src[rows, lanes] + src[POOL_OFF + r - sh:POOL_OFF + r - sh + QBLK, lanes]
                cnt = jnp.minimum(tok + (r + 1), POOL_WINDOWS[gi]).astype(F32)
                mixed.append((wsum / cnt - u_s[rows, lanes]).astype(BF16))
            pm = jnp.concatenate(mixed, axis=1)
            c0 = pair * 2 * pool_group
            po = jnp.dot(pm, wp_s[pair], preferred_element_type=F32) * ps_ref[:, c0:c0 + 2 * pool_group]
            gz = z_s[r:r + QBLK, attn_w + c0:attn_w + c0 + 2 * pool_group].astype(F32)
            y_s[r:r + QBLK, attn_w + c0:attn_w + c0 + 2 * pool_group] = (po * gz).astype(BF16)

    row_q = lax.broadcasted_iota(jnp.int32, (QBLK, LANES), 0)
    n_qblk = TILE // QBLK

    def band_blocks(jb):
        n_before = TILE // LANES - jb
        return [jnp.where(t == 0, BAND_BLOCKS, cb) if cb < n_before else cb for cb in range(BAND_BLOCKS)]

    def score_dot(jb, g):
        qrow = jb * QBLK
        return lax.dot_general(k_s[qrow:qrow + BANDW, g * GROUP_W:(g + 1) * GROUP_W], qbd_s[jb, g],
                               (((1,), (1,)), ((), ())), preferred_element_type=F32)

    def probs(jb):
        buf = jb % 2
        blk = band_blocks(jb)
        for g in range(n_groups):
            st = score_dot(jb, g)
            for h in range(HEADS_PER_GROUP):
                hh = g * HEADS_PER_GROUP + h
                lanes = slice(h * QBLK, (h + 1) * QBLK)
                for cb in range(BAND_BLOCKS):
                    for sub in range(0, LANES, SUB_ROWS):
                        rows = slice(cb * LANES + sub, cb * LANES + sub + SUB_ROWS)
                        p = jnp.exp2(st[rows, lanes] + bias_ref[hh, blk[cb], sub:sub + SUB_ROWS, :])
                        p_s[buf, hh // 2, rows, (hh % 2) * QBLK:(hh % 2 + 1) * QBLK] = p.astype(BF16)

    def probs_exact(jb):
        buf = jb % 2
        blk = band_blocks(jb)
        for g in range(n_groups):
            st = score_dot(jb, g)
            for h in range(HEADS_PER_GROUP):
                hh = g * HEADS_PER_GROUP + h
                lanes = slice(h * QBLK, (h + 1) * QBLK)
                macc = None
                for cb in range(BAND_BLOCKS):
                    for sub in range(0, LANES, SUB_ROWS):
                        rows = slice(cb * LANES + sub, cb * LANES + sub + SUB_ROWS)
                        sb = st[rows, lanes] + biasx_ref[hh, blk[cb], sub:sub + SUB_ROWS, :]
                        s_s[hh, rows, :] = sb
                        for r8 in range(0, SUB_ROWS, 8):
                            macc = sb[r8:r8 + 8, :] if macc is None else jnp.maximum(macc, sb[r8:r8 + 8, :])
                m = jnp.max(macc, axis=0, keepdims=True)
                for r in range(0, BANDW, SUB_ROWS):
                    rows = slice(r, r + SUB_ROWS)
                    p = jnp.exp2(s_s[hh, rows, :] - m)
                    p_s[buf, hh // 2, rows, (hh % 2) * QBLK:(hh % 2 + 1) * QBLK] = p.astype(BF16)

    ones_rows = jnp.ones((ONES_ROWS, BANDW), BF16)

    def attend(jb):
        buf = jb % 2
        qrow = jb * QBLK
        lmin = None
        for pair in range(n_heads // 2):
            ha = 2 * pair
            vt1 = jnp.concatenate(
                [vt_s[ha * HEAD_DIM:(ha + 2) * HEAD_DIM, qrow:qrow + BANDW], ones_rows], axis=0)
            ot = jnp.dot(vt1, p_s[buf, pair], preferred_element_type=F32)
            l = ot[2 * HEAD_DIM:2 * HEAD_DIM + 1, :]
            lmin = l if lmin is None else jnp.minimum(lmin, l)
            linv = 1.0 / l
            oa = ot[0:2 * HEAD_DIM, 0:QBLK] * linv[:, 0:QBLK]
            ob = ot[0:2 * HEAD_DIM, QBLK:2 * QBLK] * linv[:, QBLK:2 * QBLK]
            a = jnp.where(row_q < HEAD_DIM, oa, ob).T
            c0 = ha * HEAD_DIM
            gz = z_s[qrow:qrow + QBLK, c0:c0 + LANES].astype(F32)
            y_s[qrow:qrow + QBLK, c0:c0 + LANES] = (a * gz).astype(BF16)
        return jnp.min(lmin)

    probs(0)
    underflowed = []
    for jb in range(n_qblk):
        if jb + 1 < n_qblk:
            probs(jb + 1)
        underflowed.append(jnp.logical_not(attend(jb) >= L_FLOOR))

    for jb in range(n_qblk):
        @pl.when(underflowed[jb])
        def _(jb=jb):
            probs_exact(jb)
            attend(jb)

    o = jnp.dot(y_s[...], wout_ref[...], preferred_element_type=F32)
    never = t < 0
    for r in range(0, TILE, SUB_ROWS):
        tie = sum(norm_rows(xn_ref, modn_ref, rn) for rn in range(r, r + SUB_ROWS, NORM_ROWS))
        gate_r = jnp.concatenate([jnp.where(never, tie, gate[:, 0:LANES]), gate[:, LANES:]], axis=1)
        o_ref[r:r + SUB_ROWS, :] = x_ref[r:r + SUB_ROWS, :] + gate_r * o[r:r + SUB_ROWS, :]


def _layer(x, mod, norm_g, w_in, q_norm_g, k_norm_g, rel_bias, w_pool, pool_scale, w_out):
    bsz, seq, d_model = x.shape
    n_heads = rel_bias.shape[0]
    attn_w = n_heads * HEAD_DIM
    pool_w = w_pool.shape[0] * w_pool.shape[1]
    pool_group = w_pool.shape[1]
    in_w = w_in.shape[1]
    assert seq % TILE == 0 and attn_w % GROUP_W == 0 and TILE == N_LEFT_CHUNKS * CHUNK
    assert in_w == 3 * attn_w + pool_w + attn_w + pool_w and 2 * pool_group == GROUP_W
    assert attn_w == 512 and pool_w == 512
    n_groups = attn_w // GROUP_W

    n_pairs = w_pool.shape[0] // 2
    assert QBLK <= REL_CLIP <= TILE and rel_bias.shape[1] == 2 * REL_CLIP + 1

    tiles_per_seq = seq // TILE

    def next_tile(b, t):
        n = jnp.minimum(b * tiles_per_seq + t + 1, bsz * tiles_per_seq - 1)
        return n // tiles_per_seq, n % tiles_per_seq

    const2 = lambda b, t: (0, 0)
    const3 = lambda b, t: (0, 0, 0)
    kernel = functools.partial(_block_kernel, d_model=d_model, attn_w=attn_w, pool_w=pool_w)
    return pl.pallas_call(
        kernel,
        grid=(bsz, seq // TILE),
        in_specs=[
            pl.BlockSpec((None, TILE, d_model), lambda b, t: (b, t, 0)),
            pl.BlockSpec((None, TILE, d_model), lambda b, t: (*next_tile(b, t), 0)),
            pl.BlockSpec((None, 1, 3 * d_model), lambda b, t: (b, 0, 0)),
            pl.BlockSpec((None, 1, 3 * d_model), lambda b, t: (next_tile(b, t)[0], 0, 0)),
            pl.BlockSpec((1, d_model), const2),
            pl.BlockSpec((d_model, in_w), const2),
            pl.BlockSpec((attn_w + pool_w, d_model), const2),
            pl.BlockSpec((1, HEAD_DIM), const2),
            pl.BlockSpec((1, HEAD_DIM), const2),
            pl.BlockSpec((n_heads, 2 * REL_CLIP + 1), const2),
            pl.BlockSpec((2 * n_pairs, pool_group, pool_group), const3),
            pl.BlockSpec((1, pool_w), const2),
        ],
        out_specs=pl.BlockSpec((None, TILE, d_model), lambda b, t: (b, t, 0)),
        out_shape=jax.ShapeDtypeStruct(x.shape, x.dtype),
        scratch_shapes=[
            pltpu.VMEM((TILE, d_model), BF16),
            pltpu.VMEM((TILE, attn_w + pool_w), BF16),
            pltpu.VMEM((TILE // QBLK, n_groups, HEADS_PER_GROUP * QBLK, GROUP_W), BF16),
            pltpu.VMEM((2 * TILE, attn_w), BF16),
            pltpu.VMEM((attn_w, 2 * TILE), BF16),
            pltpu.VMEM((TILE + POOL_OFF, pool_w), F32),
            pltpu.VMEM((TILE + POOL_OFF, pool_w), F32),
            pltpu.VMEM((TILE + POOL_OFF, pool_w), F32),
            pltpu.VMEM((TILE + POOL_OFF, pool_w), F32),
            pltpu.VMEM((n_heads, BANDW, QBLK), F32),
            pltpu.VMEM((2, n_heads // 2, BANDW, 2 * QBLK), BF16),
            pltpu.VMEM((TILE, attn_w + pool_w), BF16),
            pltpu.VMEM((n_heads, BAND_BLOCKS + 1, QBLK, LANES), F32),
            pltpu.VMEM((n_heads, BAND_BLOCKS + 1, QBLK, LANES), F32),
            pltpu.VMEM((attn_w, d_model), BF16),
            pltpu.VMEM((2, 8, attn_w), F32),
            pltpu.VMEM((n_pairs, GROUP_W, GROUP_W), BF16),
        ],
        compiler_params=pltpu.CompilerParams(
            dimension_semantics=("arbitrary", "arbitrary"),
            vmem_limit_bytes=V7X_VMEM_LIMIT_BYTES,
        ),
        name="hybrid_block",
    )(x, x, mod, mod, norm_g.reshape(1, d_model), w_in.astype(BF16), w_out.astype(BF16),
      q_norm_g.reshape(1, HEAD_DIM), k_norm_g.reshape(1, HEAD_DIM), rel_bias, w_pool,
      pool_scale.reshape(1, pool_w))


def kernel(x, c, norm_g, w_ada, b_ada, w_in, q_norm_g, k_norm_g, rel_bias, w_pool, pool_scale, w_out):
    depth = w_in.shape[0]
    for l in range(depth):
        mod = _adaln_mod(c, w_ada[l], b_ada[l])
        x = _layer(x, mod, norm_g[l], w_in[l], q_norm_g[l], k_norm_g[l], rel_bias[l],
                   w_pool[l], pool_scale[l], w_out[l])
    return x
```

```python
import functools
import math

import jax
import jax.numpy as jnp
from jax import lax
from jax.experimental import pallas as pl
from jax.experimental.pallas import tpu as pltpu

CHUNK = 64
N_LEFT_CHUNKS = 8
HEAD_DIM = 64
REL_CLIP = 256
POOL_WINDOWS = (2, 4, 8, 16)
EPS = 1e-6
NEG_INF = -1e30
LOG2E = math.log2(math.e)

LANES = 128
TILE = 512
HALF = TILE // 2
QBLK = 2 * CHUNK
BANDW = TILE + QBLK
BAND_BLOCKS = BANDW // LANES
BIAS_EXT = BANDW + QBLK
HEADS_PER_GROUP = 4
GROUP_W = HEADS_PER_GROUP * HEAD_DIM
POOL_PAD = 8
POOL_HIST = 16
POOL_OFF = POOL_PAD + POOL_HIST
ONES_ROWS = 16
SUB_ROWS = 32
NORM_ROWS = 16
BOUND_MARGIN = 1.02
L_FLOOR = 2.0 ** -90
MOD_BLOCK_W = 1024
V7X_VMEM_LIMIT_BYTES = 56 * 1024 * 1024

F32 = jnp.float32
BF16 = jnp.bfloat16


def _mod_kernel(c_ref, w_ref, b_ref, o_ref):
    res = jnp.dot(c_ref[...], w_ref[...], preferred_element_type=F32) + b_ref[...]
    for b in range(o_ref.shape[0]):
        o_ref[b] = res[b:b + 1, :]


def _adaln_mod(c, w_ada, b_ada):
    bsz, d = c.shape
    n = w_ada.shape[1]
    return pl.pallas_call(
        _mod_kernel,
        grid=(n // MOD_BLOCK_W,),
        in_specs=[
            pl.BlockSpec((bsz, d), lambda i: (0, 0)),
            pl.BlockSpec((d, MOD_BLOCK_W), lambda i: (0, i)),
            pl.BlockSpec((1, MOD_BLOCK_W), lambda i: (0, i)),
        ],
        out_specs=pl.BlockSpec((bsz, 1, MOD_BLOCK_W), lambda i: (0, 0, i)),
        out_shape=jax.ShapeDtypeStruct((bsz, 1, n), F32),
        name="adaln_mod",
    )(c, w_ada, b_ada.reshape(1, n))


def _silu(z):
    return z * (1.0 / (1.0 + jnp.exp(-z)))


def _block_kernel(x_ref, xn_ref, mod_ref, modn_ref, g_ref, win_ref, wout_ref, gq_ref, gk_ref, rb_ref, wpool_ref, ps_ref, o_ref,
                  h_s, z_s, qbd_s, k_s, vt_s, u_s, t2_s, t4_s, t8_s, s_s, p_s, y_s, bias_ref, biasx_ref,
                  wvt_s, gain_s, wp_s,
                  *, d_model, attn_w, pool_w):
    t = pl.program_id(1)
    n_heads = attn_w // HEAD_DIM
    n_groups = attn_w // GROUP_W
    o_k, o_v, o_u, o_z = attn_w, 2 * attn_w, 3 * attn_w, 3 * attn_w + pool_w
    pool_group = pool_w // len(POOL_WINDOWS)
    pool_rows = TILE + POOL_OFF

    @pl.when((pl.program_id(0) == 0) & (t == 0))
    def _():
        a = lax.broadcasted_iota(jnp.int32, (QBLK, LANES), 0)
        lane = lax.broadcasted_iota(jnp.int32, (QBLK, LANES), 1)
        band_lo = jnp.where(a >= CHUNK, CHUNK, 0)
        gq_max = jnp.max(jnp.abs(gq_ref[...]), axis=-1, keepdims=True)
        gk_max = jnp.max(jnp.abs(gk_ref[...]), axis=-1, keepdims=True)
        dot_bound = gq_max * gk_max * (HEAD_DIM ** 0.5 * LOG2E * BOUND_MARGIN)
        for h in range(n_heads):
            first = rb_ref[h:h + 1, 0:1] * LOG2E
            row_ext = jnp.concatenate([
                jnp.broadcast_to(first, (1, TILE - REL_CLIP)),
                rb_ref[h:h + 1, 0:REL_CLIP + QBLK] * LOG2E,
                jnp.broadcast_to(first, (1, BIAS_EXT - BANDW))], axis=1)
            row = jnp.broadcast_to(row_ext, (QBLK, BIAS_EXT))
            skew = pltpu.roll(row, 0, 1, stride=1, stride_axis=0)
            bound = dot_bound + jnp.max(rb_ref[h:h + 1, :], axis=-1, keepdims=True) * LOG2E + 1.0
            for cb in range(BAND_BLOCKS):
                j = lane + cb * LANES
                visible = (j >= band_lo) & (j < band_lo + (N_LEFT_CHUNKS + 1) * CHUNK)
                blk = skew[:, cb * LANES:(cb + 1) * LANES]
                biasx_ref[h, cb] = jnp.where(visible, blk, NEG_INF).T
                bias_ref[h, cb] = jnp.where(visible, blk - bound, NEG_INF).T
            biasx_ref[h, BAND_BLOCKS] = jnp.full((QBLK, LANES), NEG_INF, F32)
            bias_ref[h, BAND_BLOCKS] = jnp.full((QBLK, LANES), NEG_INF, F32)
        for r in range(0, d_model, LANES):
            for c in range(0, attn_w, LANES):
                blk = win_ref[r:r + LANES, o_v + c:o_v + c + LANES].astype(F32)
                wvt_s[c:c + LANES, r:r + LANES] = blk.T.astype(BF16)
        gain_s[0] = jnp.broadcast_to(jnp.concatenate([gq_ref[...]] * n_heads, axis=1)
                                     * (HEAD_DIM ** -0.5 * LOG2E), (8, attn_w))
        gain_s[1] = jnp.broadcast_to(jnp.concatenate([gk_ref[...]] * n_heads, axis=1), (8, attn_w))
        k_s[...] = jnp.zeros(k_s.shape, BF16)
        vt_s[...] = jnp.zeros(vt_s.shape, BF16)
        wp_s[...] = jnp.zeros(wp_s.shape, BF16)
        for gi in range(len(POOL_WINDOWS)):
            lo = (gi % 2) * pool_group
            wp_s[gi // 2, lo:lo + pool_group, lo:lo + pool_group] = wpool_ref[gi].astype(BF16)

    @pl.when(t > 0)
    def _():
        u_s[POOL_PAD:POOL_OFF, :] = u_s[TILE + POOL_PAD:TILE + POOL_OFF, :]

    @pl.when(t == 0)
    def _():
        u_s[0:POOL_OFF, :] = jnp.zeros((POOL_OFF, pool_w), F32)
        t2_s[0:POOL_PAD, :] = jnp.zeros((POOL_PAD, pool_w), F32)
        t4_s[0:POOL_PAD, :] = jnp.zeros((POOL_PAD, pool_w), F32)
        t8_s[0:POOL_PAD, :] = jnp.zeros((POOL_PAD, pool_w), F32)

    k_s[0:TILE, :] = k_s[TILE:2 * TILE, :]
    vt_s[:, 0:TILE] = vt_s[:, TILE:2 * TILE]

    gate = mod_ref[:, 2 * d_model:3 * d_model]

    def norm_rows(src_ref, m_ref, r):
        a_row = g_ref[...] * (1.0 + m_ref[:, d_model:2 * d_model])
        xc = src_ref[r:r + NORM_ROWS, :]
        ms = jnp.mean(xc * xc, axis=-1, keepdims=True)
        hc = xc * lax.rsqrt(ms + EPS) * a_row + m_ref[:, 0:d_model]
        h_s[r:r + NORM_ROWS, :] = hc.astype(BF16)
        return hc[0:1, 0:LANES]

    @pl.when((pl.program_id(0) == 0) & (t == 0))
    def _():
        for r in range(0, TILE, NORM_ROWS):
            norm_rows(x_ref, mod_ref, r)

    lane_g = lax.broadcasted_iota(jnp.int32, (SUB_ROWS, GROUP_W), 1)
    lane_h = lax.broadcasted_iota(jnp.int32, (SUB_ROWS, LANES), 1)

    def in_proj(rows, c0, width):
        return jnp.dot(h_s[rows, :], win_ref[:, c0:c0 + width], preferred_element_type=F32)

    def head_rms(xf, gain_row):
        for sub in range(0, xf.shape[0], SUB_ROWS):
            x = xf[sub:sub + SUB_ROWS]
            sq = x * x
            ssq = []
            for c0 in range(0, GROUP_W, LANES):
                blk = sq[:, c0:c0 + LANES]
                first = jnp.sum(jnp.where(lane_h < HEAD_DIM, blk, 0.0), axis=-1, keepdims=True)
                both = jnp.sum(blk, axis=-1, keepdims=True)
                ssq.append(jnp.where(lane_h < HEAD_DIM, first, both - first))
            ssq = jnp.concatenate(ssq, axis=1)
            yield sub, x * lax.rsqrt(ssq * (1.0 / HEAD_DIM) + EPS) * gain_row

    tile_rows = slice(0, TILE)
    qf = in_proj(tile_rows, 0, attn_w)
    for rb in range(0, TILE, QBLK):
        for g in range(n_groups):
            cols = slice(g * GROUP_W, (g + 1) * GROUP_W)
            gain = gain_s[0, 0:1, cols]
            for sub, qn in head_rms(qf[rb:rb + QBLK, cols], gain):
                for h in range(HEADS_PER_GROUP):
                    keep = (lane_g >= h * HEAD_DIM) & (lane_g < (h + 1) * HEAD_DIM)
                    qbd_s[rb // QBLK, g, h * QBLK + sub:h * QBLK + sub + SUB_ROWS, :] = (
                        jnp.where(keep, qn, 0.0).astype(BF16))
    kf = in_proj(tile_rows, o_k, attn_w)
    for rb in range(0, TILE, QBLK):
        for g in range(n_groups):
            cols = slice(g * GROUP_W, (g + 1) * GROUP_W)
            for sub, kn in head_rms(kf[rb:rb + QBLK, cols], gain_s[1, 0:1, cols]):
                row = TILE + rb + sub
                k_s[row:row + SUB_ROWS, cols] = kn.astype(BF16)
    vt_s[:, TILE:2 * TILE] = lax.dot_general(
        wvt_s[...], h_s[...], (((1,), (1,)), ((), ())), preferred_element_type=F32).astype(BF16)
    u_s[POOL_OFF:POOL_OFF + TILE, :] = in_proj(tile_rows, o_u, pool_w)
    for c0 in range(0, attn_w + pool_w, 512):
        z_s[:, c0:c0 + 512] = _silu(in_proj(tile_rows, o_z + c0, 512)).astype(BF16)

    step = 88
    levels = ((u_s, t2_s, 1), (t2_s, t4_s, 2), (t4_s, t8_s, 4), (t8_s, None, 8))
    for gi in range(len(POOL_WINDOWS)):
        assert POOL_WINDOWS[gi] == 2 ** (gi + 1)
    for li, (src, dst, sh) in enumerate(levels[:-1]):
        lanes = slice((li + 1) * pool_group, pool_w)
        for r in range(POOL_PAD, pool_rows, step):
            dst[r:r + step, lanes] = src[r:r + step, lanes] + src[r - sh:r - sh + step, lanes]
    tok = t * TILE + lax.broadcasted_iota(jnp.int32, (QBLK, pool_group), 0)
    for r in range(0, TILE, QBLK):
        for pair in range(len(POOL_WINDOWS) // 2):
            mixed = []
            for gi in (2 * pair, 2 * pair + 1):
                lanes = slice(gi * pool_group, (gi + 1) * pool_group)
                src = levels[gi][0]
                sh = levels[gi][2]
                rows = slice(POOL_OFF + r, POOL_OFF + r + QBLK)
                wsum = src[rows, lanes] + src[POOL_OFF + r - sh:POOL_OFF + r - sh + QBLK, lanes]
                cnt = jnp.minimum(tok + (r + 1), POOL_WINDOWS[gi]).astype(F32)
                mixed.append((wsum / cnt - u_s[rows, lanes]).astype(BF16))
            pm = jnp.concatenate(mixed, axis=1)
            c0 = pair * 2 * pool_group
            po = jnp.dot(pm, wp_s[pair], preferred_element_type=F32) * ps_ref[:, c0:c0 + 2 * pool_group]
            gz = z_s[r:r + QBLK, attn_w + c0:attn_w + c0 + 2 * pool_group].astype(F32)
            y_s[r:r + QBLK, attn_w + c0:attn_w + c0 + 2 * pool_group] = (po * gz).astype(BF16)

    row_q = lax.broadcasted_iota(jnp.int32, (QBLK, LANES), 0)
    n_qblk = TILE // QBLK

    def band_blocks(jb):
        n_before = TILE // LANES - jb
        return [jnp.where(t == 0, BAND_BLOCKS, cb) if cb < n_before else cb for cb in range(BAND_BLOCKS)]

    def score_dot(jb, g):
        qrow = jb * QBLK
        return lax.dot_general(k_s[qrow:qrow + BANDW, g * GROUP_W:(g + 1) * GROUP_W], qbd_s[jb, g],
                               (((1,), (1,)), ((), ())), preferred_element_type=F32)

    def probs(jb):
        buf = jb % 2
        blk = band_blocks(jb)
        for g in range(n_groups):
            st = score_dot(jb, g)
            for h in range(HEADS_PER_GROUP):
                hh = g * HEADS_PER_GROUP + h
                lanes = slice(h * QBLK, (h + 1) * QBLK)
                for cb in range(BAND_BLOCKS):
                    for sub in range(0, LANES, SUB_ROWS):
                        rows = slice(cb * LANES + sub, cb * LANES + sub + SUB_ROWS)
                        p = jnp.exp2(st[rows, lanes] + bias_ref[hh, blk[cb], sub:sub + SUB_ROWS, :])
                        p_s[buf, hh // 2, rows, (hh % 2) * QBLK:(hh % 2 + 1) * QBLK] = p.astype(BF16)

    def probs_exact(jb):
        buf = jb % 2
        blk = band_blocks(jb)
        for g in range(n_groups):
            st = score_dot(jb, g)
            for h in range(HEADS_PER_GROUP):
                hh = g * HEADS_PER_GROUP + h
                lanes = slice(h * QBLK, (h + 1) * QBLK)
                macc = None
                for cb in range(BAND_BLOCKS):
                    for sub in range(0, LANES, SUB_ROWS):
                        rows = slice(cb * LANES + sub, cb * LANES + sub + SUB_ROWS)
                        sb = st[rows, lanes] + biasx_ref[hh, blk[cb], sub:sub + SUB_ROWS, :]
                        s_s[hh, rows, :] = sb
                        for r8 in range(0, SUB_ROWS, 8):
                            macc = sb[r8:r8 + 8, :] if macc is None else jnp.maximum(macc, sb[r8:r8 + 8, :])
                m = jnp.max(macc, axis=0, keepdims=True)
                for r in range(0, BANDW, SUB_ROWS):
                    rows = slice(r, r + SUB_ROWS)
                    p = jnp.exp2(s_s[hh, rows, :] - m)
                    p_s[buf, hh // 2, rows, (hh % 2) * QBLK:(hh % 2 + 1) * QBLK] = p.astype(BF16)

    ones_rows = jnp.ones((ONES_ROWS, BANDW), BF16)

    def attend(jb):
        buf = jb % 2
        qrow = jb * QBLK
        lmin = None
        for pair in range(n_heads // 2):
            ha = 2 * pair
            vt1 = jnp.concatenate(
                [vt_s[ha * HEAD_DIM:(ha + 2) * HEAD_DIM, qrow:qrow + BANDW], ones_rows], axis=0)
            ot = jnp.dot(vt1, p_s[buf, pair], preferred_element_type=F32)
            l = ot[2 * HEAD_DIM:2 * HEAD_DIM + 1, :]
            lmin = l if lmin is None else jnp.minimum(lmin, l)
            linv = 1.0 / l
            oa = ot[0:2 * HEAD_DIM, 0:QBLK] * linv[:, 0:QBLK]
            ob = ot[0:2 * HEAD_DIM, QBLK:2 * QBLK] * linv[:, QBLK:2 * QBLK]
            a = jnp.where(row_q < HEAD_DIM, oa, ob).T
            c0 = ha * HEAD_DIM
            gz = z_s[qrow:qrow + QBLK, c0:c0 + LANES].astype(F32)
            y_s[qrow:qrow + QBLK, c0:c0 + LANES] = (a * gz).astype(BF16)
        return jnp.min(lmin)

    probs(0)
    underflowed = []
    for jb in range(n_qblk):
        if jb + 1 < n_qblk:
            probs(jb + 1)
        underflowed.append(jnp.logical_not(attend(jb) >= L_FLOOR))

    o = jnp.dot(y_s[...], wout_ref[...], preferred_element_type=F32)
    never = t < 0
    for r in range(0, TILE, SUB_ROWS):
        tie = sum(norm_rows(xn_ref, modn_ref, rn) for rn in range(r, r + SUB_ROWS, NORM_ROWS))
        gate_r = jnp.concatenate([jnp.where(never, tie, gate[:, 0:LANES]), gate[:, LANES:]], axis=1)
        o_ref[r:r + SUB_ROWS, :] = x_ref[r:r + SUB_ROWS, :] + gate_r * o[r:r + SUB_ROWS, :]

    for jb in range(n_qblk):
        @pl.when(underflowed[jb])
        def _(jb=jb):
            probs_exact(jb)
            attend(jb)
            rows = slice(jb * QBLK, (jb + 1) * QBLK)
            o_jb = jnp.dot(y_s[rows, :], wout_ref[...], preferred_element_type=F32)
            o_ref[rows, :] = x_ref[rows, :] + gate * o_jb


def _layer(x, mod, norm_g, w_in, q_norm_g, k_norm_g, rel_bias, w_pool, pool_scale, w_out):
    bsz, seq, d_model = x.shape
    n_heads = rel_bias.shape[0]
    attn_w = n_heads * HEAD_DIM
    pool_w = w_pool.shape[0] * w_pool.shape[1]
    pool_group = w_pool.shape[1]
    in_w = w_in.shape[1]
    assert seq % TILE == 0 and attn_w % GROUP_W == 0 and TILE == N_LEFT_CHUNKS * CHUNK
    assert in_w == 3 * attn_w + pool_w + attn_w + pool_w and 2 * pool_group == GROUP_W
    assert attn_w == 512 and pool_w == 512
    n_groups = attn_w // GROUP_W

    n_pairs = w_pool.shape[0] // 2
    assert QBLK <= REL_CLIP <= TILE and rel_bias.shape[1] == 2 * REL_CLIP + 1

    tiles_per_seq = seq // TILE

    def next_tile(b, t):
        n = jnp.minimum(b * tiles_per_seq + t + 1, bsz * tiles_per_seq - 1)
        return n // tiles_per_seq, n % tiles_per_seq

    const2 = lambda b, t: (0, 0)
    const3 = lambda b, t: (0, 0, 0)
    kernel = functools.partial(_block_kernel, d_model=d_model, attn_w=attn_w, pool_w=pool_w)
    return pl.pallas_call(
        kernel,
        grid=(bsz, seq // TILE),
        in_specs=[
            pl.BlockSpec((None, TILE, d_model), lambda b, t: (b, t, 0)),
            pl.BlockSpec((None, TILE, d_model), lambda b, t: (*next_tile(b, t), 0)),
            pl.BlockSpec((None, 1, 3 * d_model), lambda b, t: (b, 0, 0)),
            pl.BlockSpec((None, 1, 3 * d_model), lambda b, t: (next_tile(b, t)[0], 0, 0)),
            pl.BlockSpec((1, d_model), const2),
            pl.BlockSpec((d_model, in_w), const2),
            pl.BlockSpec((attn_w + pool_w, d_model), const2),
            pl.BlockSpec((1, HEAD_DIM), const2),
            pl.BlockSpec((1, HEAD_DIM), const2),
            pl.BlockSpec((n_heads, 2 * REL_CLIP + 1), const2),
            pl.BlockSpec((2 * n_pairs, pool_group, pool_group), const3),
            pl.BlockSpec((1, pool_w), const2),
        ],
        out_specs=pl.BlockSpec((None, TILE, d_model), lambda b, t: (b, t, 0)),
        out_shape=jax.ShapeDtypeStruct(x.shape, x.dtype),
        scratch_shapes=[
            pltpu.VMEM((TILE, d_model), BF16),
            pltpu.VMEM((TILE, attn_w + pool_w), BF16),
            pltpu.VMEM((TILE // QBLK, n_groups, HEADS_PER_GROUP * QBLK, GROUP_W), BF16),
            pltpu.VMEM((2 * TILE, attn_w), BF16),
            pltpu.VMEM((attn_w, 2 * TILE), BF16),
            pltpu.VMEM((TILE + POOL_OFF, pool_w), F32),
            pltpu.VMEM((TILE + POOL_OFF, pool_w), F32),
            pltpu.VMEM((TILE + POOL_OFF, pool_w), F32),
            pltpu.VMEM((TILE + POOL_OFF, pool_w), F32),
            pltpu.VMEM((n_heads, BANDW, QBLK), F32),
            pltpu.VMEM((2, n_heads // 2, BANDW, 2 * QBLK), BF16),
            pltpu.VMEM((TILE, attn_w + pool_w), BF16),
            pltpu.VMEM((n_heads, BAND_BLOCKS + 1, QBLK, LANES), F32),
            pltpu.VMEM((n_heads, BAND_BLOCKS + 1, QBLK, LANES), F32),
            pltpu.VMEM((attn_w, d_model), BF16),
            pltpu.VMEM((2, 8, attn_w), F32),
            pltpu.VMEM((n_pairs, GROUP_W, GROUP_W), BF16),
        ],
        compiler_params=pltpu.CompilerParams(
            dimension_semantics=("arbitrary", "arbitrary"),
            vmem_limit_bytes=V7X_VMEM_LIMIT_BYTES,
        ),
        name="hybrid_block",
    )(x, x, mod, mod, norm_g.reshape(1, d_model), w_in.astype(BF16), w_out.astype(BF16),
      q_norm_g.reshape(1, HEAD_DIM), k_norm_g.reshape(1, HEAD_DIM), rel_bias, w_pool,
      pool_scale.reshape(1, pool_w))


def kernel(x, c, norm_g, w_ada, b_ada, w_in, q_norm_g, k_norm_g, rel_bias, w_pool, pool_scale, w_out):
    depth = w_in.shape[0]
    for l in range(depth):
        mod = _adaln_mod(c, w_ada[l], b_ada[l])
        x = _layer(x, mod, norm_g[l], w_in[l], q_norm_g[l], k_norm_g[l], rel_bias[l],
                   w_pool[l], pool_scale[l], w_out[l])
    return x
```

```python
import functools
import math

import jax
import jax.numpy as jnp
from jax import lax
from jax.experimental import pallas as pl
from jax.experimental.pallas import tpu as pltpu

CHUNK = 64
N_LEFT_CHUNKS = 8
HEAD_DIM = 64
REL_CLIP = 256
POOL_WINDOWS = (2, 4, 8, 16)
EPS = 1e-6
NEG_INF = -1e30
LOG2E = math.log2(math.e)

LANES = 128
TILE = 512
HALF = TILE // 2
QBLK = 2 * CHUNK
BANDW = TILE + QBLK
BAND_BLOCKS = BANDW // LANES
BIAS_EXT = BANDW + QBLK
HEADS_PER_GROUP = 4
GROUP_W = HEADS_PER_GROUP * HEAD_DIM
POOL_PAD = 8
POOL_HIST = 16
POOL_OFF = POOL_PAD + POOL_HIST
ONES_ROWS = 16
SUB_ROWS = 32
NORM_ROWS = 16
BOUND_MARGIN = 1.02
L_FLOOR = 2.0 ** -90
MOD_BLOCK_W = 1024
V7X_VMEM_LIMIT_BYTES = 56 * 1024 * 1024

F32 = jnp.float32
BF16 = jnp.bfloat16


def _mod_kernel(c_ref, w_ref, b_ref, o_ref):
    res = jnp.dot(c_ref[...], w_ref[...], preferred_element_type=F32) + b_ref[...]
    for b in range(o_ref.shape[0]):
        o_ref[b] = res[b:b + 1, :]


def _adaln_mod(c, w_ada, b_ada):
    bsz, d = c.shape
    n = w_ada.shape[1]
    return pl.pallas_call(
        _mod_kernel,
        grid=(n // MOD_BLOCK_W,),
        in_specs=[
            pl.BlockSpec((bsz, d), lambda i: (0, 0)),
            pl.BlockSpec((d, MOD_BLOCK_W), lambda i: (0, i)),
            pl.BlockSpec((1, MOD_BLOCK_W), lambda i: (0, i)),
        ],
        out_specs=pl.BlockSpec((bsz, 1, MOD_BLOCK_W), lambda i: (0, 0, i)),
        out_shape=jax.ShapeDtypeStruct((bsz, 1, n), F32),
        name="adaln_mod",
    )(c, w_ada, b_ada.reshape(1, n))


def _silu(z):
    return z * (1.0 / (1.0 + jnp.exp(-z)))


def _block_kernel(x_ref, xn_ref, mod_ref, modn_ref, g_ref, win_ref, wout_ref, gq_ref, gk_ref, rb_ref, wpool_ref, ps_ref, o_ref,
                  h_s, z_s, qbd_s, k_s, vt_s, u_s, t2_s, t4_s, t8_s, s_s, p_s, y_s, bias_ref, biasx_ref,
                  wvt_s, gain_s, wp_s,
                  *, d_model, attn_w, pool_w):
    t = pl.program_id(1)
    n_heads = attn_w // HEAD_DIM
    n_groups = attn_w // GROUP_W
    o_k, o_v, o_u, o_z = attn_w, 2 * attn_w, 3 * attn_w, 3 * attn_w + pool_w
    pool_group = pool_w // len(POOL_WINDOWS)
    pool_rows = TILE + POOL_OFF

    @pl.when((pl.program_id(0) == 0) & (t == 0))
    def _():
        a = lax.broadcasted_iota(jnp.int32, (QBLK, LANES), 0)
        lane = lax.broadcasted_iota(jnp.int32, (QBLK, LANES), 1)
        band_lo = jnp.where(a >= CHUNK, CHUNK, 0)
        gq_max = jnp.max(jnp.abs(gq_ref[...]), axis=-1, keepdims=True)
        gk_max = jnp.max(jnp.abs(gk_ref[...]), axis=-1, keepdims=True)
        dot_bound = gq_max * gk_max * (HEAD_DIM ** 0.5 * LOG2E * BOUND_MARGIN)
        for h in range(n_heads):
            first = rb_ref[h:h + 1, 0:1] * LOG2E
            row_ext = jnp.concatenate([
                jnp.broadcast_to(first, (1, TILE - REL_CLIP)),
                rb_ref[h:h + 1, 0:REL_CLIP + QBLK] * LOG2E,
                jnp.broadcast_to(first, (1, BIAS_EXT - BANDW))], axis=1)
            row = jnp.broadcast_to(row_ext, (QBLK, BIAS_EXT))
            skew = pltpu.roll(row, 0, 1, stride=1, stride_axis=0)
            bound = dot_bound + jnp.max(rb_ref[h:h + 1, :], axis=-1, keepdims=True) * LOG2E + 1.0
            for cb in range(BAND_BLOCKS):
                j = lane + cb * LANES
                visible = (j >= band_lo) & (j < band_lo + (N_LEFT_CHUNKS + 1) * CHUNK)
                blk = skew[:, cb * LANES:(cb + 1) * LANES]
                biasx_ref[h, cb] = jnp.where(visible, blk, NEG_INF).T
                bias_ref[h, cb] = jnp.where(visible, blk - bound, NEG_INF).T
            biasx_ref[h, BAND_BLOCKS] = jnp.full((QBLK, LANES), NEG_INF, F32)
            bias_ref[h, BAND_BLOCKS] = jnp.full((QBLK, LANES), NEG_INF, F32)
        for r in range(0, d_model, LANES):
            for c in range(0, attn_w, LANES):
                blk = win_ref[r:r + LANES, o_v + c:o_v + c + LANES].astype(F32)
                wvt_s[c:c + LANES, r:r + LANES] = blk.T.astype(BF16)
        gain_s[0] = jnp.broadcast_to(jnp.concatenate([gq_ref[...]] * n_heads, axis=1)
                                     * (HEAD_DIM ** -0.5 * LOG2E), (8, attn_w))
        gain_s[1] = jnp.broadcast_to(jnp.concatenate([gk_ref[...]] * n_heads, axis=1), (8, attn_w))
        k_s[...] = jnp.zeros(k_s.shape, BF16)
        vt_s[...] = jnp.zeros(vt_s.shape, BF16)
        wp_s[...] = jnp.zeros(wp_s.shape, BF16)
        for gi in range(len(POOL_WINDOWS)):
            lo = (gi % 2) * pool_group
            wp_s[gi // 2, lo:lo + pool_group, lo:lo + pool_group] = wpool_ref[gi].astype(BF16)

    @pl.when(t > 0)
    def _():
        u_s[POOL_PAD:POOL_OFF, :] = u_s[TILE + POOL_PAD:TILE + POOL_OFF, :]

    @pl.when(t == 0)
    def _():
        u_s[0:POOL_OFF, :] = jnp.zeros((POOL_OFF, pool_w), F32)
        t2_s[0:POOL_PAD, :] = jnp.zeros((POOL_PAD, pool_w), F32)
        t4_s[0:POOL_PAD, :] = jnp.zeros((POOL_PAD, pool_w), F32)
        t8_s[0:POOL_PAD, :] = jnp.zeros((POOL_PAD, pool_w), F32)

    k_s[0:TILE, :] = k_s[TILE:2 * TILE, :]
    vt_s[:, 0:TILE] = vt_s[:, TILE:2 * TILE]

    gate = mod_ref[:, 2 * d_model:3 * d_model]

    def norm_rows(src_ref, m_ref, r):
        a_row = g_ref[...] * (1.0 + m_ref[:, d_model:2 * d_model])
        xc = src_ref[r:r + NORM_ROWS, :]
        ms = jnp.mean(xc * xc, axis=-1, keepdims=True)
        hc = xc * lax.rsqrt(ms + EPS) * a_row + m_ref[:, 0:d_model]
        h_s[r:r + NORM_ROWS, :] = hc.astype(BF16)
        return hc[0:1, 0:LANES]

    @pl.when((pl.program_id(0) == 0) & (t == 0))
    def _():
        for r in range(0, TILE, NORM_ROWS):
            norm_rows(x_ref, mod_ref, r)

    lane_g = lax.broadcasted_iota(jnp.int32, (SUB_ROWS, GROUP_W), 1)
    lane_h = lax.broadcasted_iota(jnp.int32, (SUB_ROWS, LANES), 1)

    def in_proj(rows, c0, width):
        return jnp.dot(h_s[rows, :], win_ref[:, c0:c0 + width], preferred_element_type=F32)

    def head_rms(xf, gain_row):
        for sub in range(0, xf.shape[0], SUB_ROWS):
            x = xf[sub:sub + SUB_ROWS]
            sq = x * x
            ssq = []
            for c0 in range(0, GROUP_W, LANES):
                blk = sq[:, c0:c0 + LANES]
                first = jnp.sum(jnp.where(lane_h < HEAD_DIM, blk, 0.0), axis=-1, keepdims=True)
                both = jnp.sum(blk, axis=-1, keepdims=True)
                ssq.append(jnp.where(lane_h < HEAD_DIM, first, both - first))
            ssq = jnp.concatenate(ssq, axis=1)
            yield sub, x * lax.rsqrt(ssq * (1.0 / HEAD_DIM) + EPS) * gain_row

    tile_rows = slice(0, TILE)
    qf = in_proj(tile_rows, 0, attn_w)
    for rb in range(0, TILE, QBLK):
        for g in range(n_groups):
            cols = slice(g * GROUP_W, (g + 1) * GROUP_W)
            gain = gain_s[0, 0:1, cols]
            for sub, qn in head_rms(qf[rb:rb + QBLK, cols], gain):
                for h in range(HEADS_PER_GROUP):
                    keep = (lane_g >= h * HEAD_DIM) & (lane_g < (h + 1) * HEAD_DIM)
                    qbd_s[rb // QBLK, g, h * QBLK + sub:h * QBLK + sub + SUB_ROWS, :] = (
                        jnp.where(keep, qn, 0.0).astype(BF16))
    kf = in_proj(tile_rows, o_k, attn_w)
    for rb in range(0, TILE, QBLK):
        for g in range(n_groups):
            cols = slice(g * GROUP_W, (g + 1) * GROUP_W)
            for sub, kn in head_rms(kf[rb:rb + QBLK, cols], gain_s[1, 0:1, cols]):
                row = TILE + rb + sub
                k_s[row:row + SUB_ROWS, cols] = kn.astype(BF16)
    vt_s[:, TILE:2 * TILE] = lax.dot_general(
        wvt_s[...], h_s[...], (((1,), (1,)), ((), ())), preferred_element_type=F32).astype(BF16)
    u_s[POOL_OFF:POOL_OFF + TILE, :] = in_proj(tile_rows, o_u, pool_w)
    for c0 in range(0, attn_w + pool_w, 512):
        z_s[:, c0:c0 + 512] = _silu(in_proj(tile_rows, o_z + c0, 512)).astype(BF16)

    def pool():
        step = 88
        levels = ((u_s, t2_s, 1), (t2_s, t4_s, 2), (t4_s, t8_s, 4), (t8_s, None, 8))
        for gi in range(len(POOL_WINDOWS)):
            assert POOL_WINDOWS[gi] == 2 ** (gi + 1)
        for li, (src, dst, sh) in enumerate(levels[:-1]):
            lanes = slice((li + 1) * pool_group, pool_w)
            for r in range(POOL_PAD, pool_rows, step):
                dst[r:r + step, lanes] = src[r:r + step, lanes] + src[r - sh:r - sh + step, lanes]
        tok = t * TILE + lax.broadcasted_iota(jnp.int32, (QBLK, pool_group), 0)
        for r in range(0, TILE, QBLK):
            for pair in range(len(POOL_WINDOWS) // 2):
                mixed = []
                for gi in (2 * pair, 2 * pair + 1):
                    lanes = slice(gi * pool_group, (gi + 1) * pool_group)
                    src = levels[gi][0]
                    sh = levels[gi][2]
                    rows = slice(POOL_OFF + r, POOL_OFF + r + QBLK)
                    wsum = src[rows, lanes] + src[POOL_OFF + r - sh:POOL_OFF + r - sh + QBLK, lanes]
                    cnt = jnp.minimum(tok + (r + 1), POOL_WINDOWS[gi]).astype(F32)
                    mixed.append((wsum / cnt - u_s[rows, lanes]).astype(BF16))
                pm = jnp.concatenate(mixed, axis=1)
                c0 = pair * 2 * pool_group
                po = jnp.dot(pm, wp_s[pair], preferred_element_type=F32) * ps_ref[:, c0:c0 + 2 * pool_group]
                gz = z_s[r:r + QBLK, attn_w + c0:attn_w + c0 + 2 * pool_group].astype(F32)
                y_s[r:r + QBLK, attn_w + c0:attn_w + c0 + 2 * pool_group] = (po * gz).astype(BF16)

    row_q = lax.broadcasted_iota(jnp.int32, (QBLK, LANES), 0)
    n_qblk = TILE // QBLK

    def band_blocks(jb):
        n_before = TILE // LANES - jb
        return [jnp.where(t == 0, BAND_BLOCKS, cb) if cb < n_before else cb for cb in range(BAND_BLOCKS)]

    def score_dot(jb, g):
        qrow = jb * QBLK
        return lax.dot_general(k_s[qrow:qrow + BANDW, g * GROUP_W:(g + 1) * GROUP_W], qbd_s[jb, g],
                               (((1,), (1,)), ((), ())), preferred_element_type=F32)

    def probs(jb):
        buf = jb % 2
        blk = band_blocks(jb)
        for g in range(n_groups):
            st = score_dot(jb, g)
            for h in range(HEADS_PER_GROUP):
                hh = g * HEADS_PER_GROUP + h
                lanes = slice(h * QBLK, (h + 1) * QBLK)
                for cb in range(BAND_BLOCKS):
                    for sub in range(0, LANES, SUB_ROWS):
                        rows = slice(cb * LANES + sub, cb * LANES + sub + SUB_ROWS)
                        p = jnp.exp2(st[rows, lanes] + bias_ref[hh, blk[cb], sub:sub + SUB_ROWS, :])
                        p_s[buf, hh // 2, rows, (hh % 2) * QBLK:(hh % 2 + 1) * QBLK] = p.astype(BF16)

    def probs_exact(jb):
        buf = jb % 2
        blk = band_blocks(jb)
        for g in range(n_groups):
            st = score_dot(jb, g)
            for h in range(HEADS_PER_GROUP):
                hh = g * HEADS_PER_GROUP + h
                lanes = slice(h * QBLK, (h + 1) * QBLK)
                macc = None
                for cb in range(BAND_BLOCKS):
                    for sub in range(0, LANES, SUB_ROWS):
                        rows = slice(cb * LANES + sub, cb * LANES + sub + SUB_ROWS)
                        sb = st[rows, lanes] + biasx_ref[hh, blk[cb], sub:sub + SUB_ROWS, :]
                        s_s[hh, rows, :] = sb
                        for r8 in range(0, SUB_ROWS, 8):
                            macc = sb[r8:r8 + 8, :] if macc is None else jnp.maximum(macc, sb[r8:r8 + 8, :])
                m = jnp.max(macc, axis=0, keepdims=True)
                for r in range(0, BANDW, SUB_ROWS):
                    rows = slice(r, r + SUB_ROWS)
                    p = jnp.exp2(s_s[hh, rows, :] - m)
                    p_s[buf, hh // 2, rows, (hh % 2) * QBLK:(hh % 2 + 1) * QBLK] = p.astype(BF16)

    ones_rows = jnp.ones((ONES_ROWS, BANDW), BF16)

    def attend(jb, anchor=None):
        buf = jb % 2
        qrow = jb * QBLK
        lmin = None
        for pair in range(n_heads // 2):
            ha = 2 * pair
            vt1 = jnp.concatenate(
                [vt_s[ha * HEAD_DIM:(ha + 2) * HEAD_DIM, qrow:qrow + BANDW], ones_rows], axis=0)
            ot = jnp.dot(vt1, p_s[buf, pair], preferred_element_type=F32)
            l = ot[2 * HEAD_DIM:2 * HEAD_DIM + 1, :]
            lmin = l if lmin is None else jnp.minimum(lmin, l)
            linv = 1.0 / l
            if anchor is not None:
                linv = jnp.concatenate([jnp.where(never, anchor(pair), linv[:, 0:LANES]), linv[:, LANES:]], axis=1)
            oa = ot[0:2 * HEAD_DIM, 0:QBLK] * linv[:, 0:QBLK]
            ob = ot[0:2 * HEAD_DIM, QBLK:2 * QBLK] * linv[:, QBLK:2 * QBLK]
            a = jnp.where(row_q < HEAD_DIM, oa, ob).T
            c0 = ha * HEAD_DIM
            gz = z_s[qrow:qrow + QBLK, c0:c0 + LANES].astype(F32)
            y_s[qrow:qrow + QBLK, c0:c0 + LANES] = (a * gz).astype(BF16)
        return jnp.min(lmin)

    never = t < 0
    norm_steps = list(range(0, TILE, NORM_ROWS))
    per_pair = len(norm_steps) // (n_qblk * (n_heads // 2))

    def next_norm(jb):
        def anchor(pair):
            i0 = (jb * (n_heads // 2) + pair) * per_pair
            return sum(norm_rows(xn_ref, modn_ref, rn) for rn in norm_steps[i0:i0 + per_pair])
        return anchor

    probs(0)
    pool()
    underflowed = []
    for jb in range(n_qblk):
        if jb + 1 < n_qblk:
            probs(jb + 1)
        underflowed.append(jnp.logical_not(attend(jb, next_norm(jb)) >= L_FLOOR))

    o = jnp.dot(y_s[...], wout_ref[...], preferred_element_type=F32)
    for c0 in range(0, d_model, GROUP_W):
        cols = slice(c0, c0 + GROUP_W)
        for r in range(0, TILE, SUB_ROWS):
            o_ref[r:r + SUB_ROWS, cols] = x_ref[r:r + SUB_ROWS, cols] + gate[:, cols] * o[r:r + SUB_ROWS, cols]

    for jb in range(n_qblk):
        @pl.when(underflowed[jb])
        def _(jb=jb):
            probs_exact(jb)
            attend(jb)
            rows = slice(jb * QBLK, (jb + 1) * QBLK)
            o_jb = jnp.dot(y_s[rows, :], wout_ref[...], preferred_element_type=F32)
            o_ref[rows, :] = x_ref[rows, :] + gate * o_jb


def _layer(x, mod, norm_g, w_in, q_norm_g, k_norm_g, rel_bias, w_pool, pool_scale, w_out):
    bsz, seq, d_model = x.shape
    n_heads = rel_bias.shape[0]
    attn_w = n_heads * HEAD_DIM
    pool_w = w_pool.shape[0] * w_pool.shape[1]
    pool_group = w_pool.shape[1]
    in_w = w_in.shape[1]
    assert seq % TILE == 0 and attn_w % GROUP_W == 0 and TILE == N_LEFT_CHUNKS * CHUNK
    assert in_w == 3 * attn_w + pool_w + attn_w + pool_w and 2 * pool_group == GROUP_W
    assert attn_w == 512 and pool_w == 512
    n_groups = attn_w // GROUP_W

    n_pairs = w_pool.shape[0] // 2
    assert QBLK <= REL_CLIP <= TILE and rel_bias.shape[1] == 2 * REL_CLIP + 1

    tiles_per_seq = seq // TILE

    def next_tile(b, t):
        n = jnp.minimum(b * tiles_per_seq + t + 1, bsz * tiles_per_seq - 1)
        return n // tiles_per_seq, n % tiles_per_seq

    const2 = lambda b, t: (0, 0)
    const3 = lambda b, t: (0, 0, 0)
    kernel = functools.partial(_block_kernel, d_model=d_model, attn_w=attn_w, pool_w=pool_w)
    return pl.pallas_call(
        kernel,
        grid=(bsz, seq // TILE),
        in_specs=[
            pl.BlockSpec((None, TILE, d_model), lambda b, t: (b, t, 0)),
            pl.BlockSpec((None, TILE, d_model), lambda b, t: (*next_tile(b, t), 0)),
            pl.BlockSpec((None, 1, 3 * d_model), lambda b, t: (b, 0, 0)),
            pl.BlockSpec((None, 1, 3 * d_model), lambda b, t: (next_tile(b, t)[0], 0, 0)),
            pl.BlockSpec((1, d_model), const2),
            pl.BlockSpec((d_model, in_w), const2),
            pl.BlockSpec((attn_w + pool_w, d_model), const2),
            pl.BlockSpec((1, HEAD_DIM), const2),
            pl.BlockSpec((1, HEAD_DIM), const2),
            pl.BlockSpec((n_heads, 2 * REL_CLIP + 1), const2),
            pl.BlockSpec((2 * n_pairs, pool_group, pool_group), const3),
            pl.BlockSpec((1, pool_w), const2),
        ],
        out_specs=pl.BlockSpec((None, TILE, d_model), lambda b, t: (b, t, 0)),
        out_shape=jax.ShapeDtypeStruct(x.shape, x.dtype),
        scratch_shapes=[
            pltpu.VMEM((TILE, d_model), BF16),
            pltpu.VMEM((TILE, attn_w + pool_w), BF16),
            pltpu.VMEM((TILE // QBLK, n_groups, HEADS_PER_GROUP * QBLK, GROUP_W), BF16),
            pltpu.VMEM((2 * TILE, attn_w), BF16),
            pltpu.VMEM((attn_w, 2 * TILE), BF16),
            pltpu.VMEM((TILE + POOL_OFF, pool_w), F32),
            pltpu.VMEM((TILE + POOL_OFF, pool_w), F32),
            pltpu.VMEM((TILE + POOL_OFF, pool_w), F32),
            pltpu.VMEM((TILE + POOL_OFF, pool_w), F32),
            pltpu.VMEM((n_heads, BANDW, QBLK), F32),
            pltpu.VMEM((2, n_heads // 2, BANDW, 2 * QBLK), BF16),
            pltpu.VMEM((TILE, attn_w + pool_w), BF16),
            pltpu.VMEM((n_heads, BAND_BLOCKS + 1, QBLK, LANES), F32),
            pltpu.VMEM((n_heads, BAND_BLOCKS + 1, QBLK, LANES), F32),
            pltpu.VMEM((attn_w, d_model), BF16),
            pltpu.VMEM((2, 8, attn_w), F32),
            pltpu.VMEM((n_pairs, GROUP_W, GROUP_W), BF16),
        ],
        compiler_params=pltpu.CompilerParams(
            dimension_semantics=("arbitrary", "arbitrary"),
            vmem_limit_bytes=V7X_VMEM_LIMIT_BYTES,
        ),
        name="hybrid_block",
    )(x, x, mod, mod, norm_g.reshape(1, d_model), w_in.astype(BF16), w_out.astype(BF16),
      q_norm_g.reshape(1, HEAD_DIM), k_norm_g.reshape(1, HEAD_DIM), rel_bias, w_pool,
      pool_scale.reshape(1, pool_w))


def kernel(x, c, norm_g, w_ada, b_ada, w_in, q_norm_g, k_norm_g, rel_bias, w_pool, pool_scale, w_out):
    depth = w_in.shape[0]
    for l in range(depth):
        mod = _adaln_mod(c, w_ada[l], b_ada[l])
        x = _layer(x, mod, norm_g[l], w_in[l], q_norm_g[l], k_norm_g[l], rel_bias[l],
                   w_pool[l], pool_scale[l], w_out[l])
    return x
```

```python
import functools
import math

import jax
import jax.numpy as jnp
from jax import lax
from jax.experimental import pallas as pl
from jax.experimental.pallas import tpu as pltpu

CHUNK = 64
N_LEFT_CHUNKS = 8
HEAD_DIM = 64
REL_CLIP = 256
POOL_WINDOWS = (2, 4, 8, 16)
EPS = 1e-6
NEG_INF = -1e30
LOG2E = math.log2(math.e)

LANES = 128
TILE = 512
HALF = TILE // 2
QBLK = 2 * CHUNK
BANDW = TILE + QBLK
BAND_BLOCKS = BANDW // LANES
BIAS_EXT = BANDW + QBLK
HEADS_PER_GROUP = 4
GROUP_W = HEADS_PER_GROUP * HEAD_DIM
POOL_PAD = 8
POOL_HIST = 16
POOL_OFF = POOL_PAD + POOL_HIST
SUB_ROWS = 32
NORM_ROWS = 16
BOUND_MARGIN = 1.02
L_FLOOR = 2.0 ** -90
MOD_BLOCK_W = 1024
V7X_VMEM_LIMIT_BYTES = 56 * 1024 * 1024

F32 = jnp.float32
BF16 = jnp.bfloat16


def _mod_kernel(c_ref, w_ref, b_ref, o_ref):
    res = jnp.dot(c_ref[...], w_ref[...], preferred_element_type=F32) + b_ref[...]
    for b in range(o_ref.shape[0]):
        o_ref[b] = res[b:b + 1, :]


def _adaln_mod(c, w_ada, b_ada):
    bsz, d = c.shape
    n = w_ada.shape[1]
    return pl.pallas_call(
        _mod_kernel,
        grid=(n // MOD_BLOCK_W,),
        in_specs=[
            pl.BlockSpec((bsz, d), lambda i: (0, 0)),
            pl.BlockSpec((d, MOD_BLOCK_W), lambda i: (0, i)),
            pl.BlockSpec((1, MOD_BLOCK_W), lambda i: (0, i)),
        ],
        out_specs=pl.BlockSpec((bsz, 1, MOD_BLOCK_W), lambda i: (0, 0, i)),
        out_shape=jax.ShapeDtypeStruct((bsz, 1, n), F32),
        name="adaln_mod",
    )(c, w_ada, b_ada.reshape(1, n))


def _silu(z):
    return z * (1.0 / (1.0 + jnp.exp(-z)))


def _block_kernel(x_ref, xn_ref, mod_ref, modn_ref, g_ref, win_ref, wout_ref, gq_ref, gk_ref, rb_ref, wpool_ref, ps_ref, o_ref,
                  h_s, z_s, qbd_s, k_s, vt_s, u_s, t2_s, t4_s, t8_s, s_s, p_s, l_s, y_s, bias_ref, biasx_ref,
                  wvt_s, gain_s, wp_s,
                  *, d_model, attn_w, pool_w):
    t = pl.program_id(1)
    n_heads = attn_w // HEAD_DIM
    n_groups = attn_w // GROUP_W
    o_k, o_v, o_u, o_z = attn_w, 2 * attn_w, 3 * attn_w, 3 * attn_w + pool_w
    pool_group = pool_w // len(POOL_WINDOWS)
    pool_rows = TILE + POOL_OFF

    @pl.when((pl.program_id(0) == 0) & (t == 0))
    def _():
        a = lax.broadcasted_iota(jnp.int32, (QBLK, LANES), 0)
        lane = lax.broadcasted_iota(jnp.int32, (QBLK, LANES), 1)
        band_lo = jnp.where(a >= CHUNK, CHUNK, 0)
        gq_max = jnp.max(jnp.abs(gq_ref[...]), axis=-1, keepdims=True)
        gk_max = jnp.max(jnp.abs(gk_ref[...]), axis=-1, keepdims=True)
        dot_bound = gq_max * gk_max * (HEAD_DIM ** 0.5 * LOG2E * BOUND_MARGIN)
        for h in range(n_heads):
            first = rb_ref[h:h + 1, 0:1] * LOG2E
            row_ext = jnp.concatenate([
                jnp.broadcast_to(first, (1, TILE - REL_CLIP)),
                rb_ref[h:h + 1, 0:REL_CLIP + QBLK] * LOG2E,
                jnp.broadcast_to(first, (1, BIAS_EXT - BANDW))], axis=1)
            row = jnp.broadcast_to(row_ext, (QBLK, BIAS_EXT))
            skew = pltpu.roll(row, 0, 1, stride=1, stride_axis=0)
            bound = dot_bound + jnp.max(rb_ref[h:h + 1, :], axis=-1, keepdims=True) * LOG2E + 1.0
            for cb in range(BAND_BLOCKS):
                j = lane + cb * LANES
                visible = (j >= band_lo) & (j < band_lo + (N_LEFT_CHUNKS + 1) * CHUNK)
                blk = skew[:, cb * LANES:(cb + 1) * LANES]
                biasx_ref[h, cb] = jnp.where(visible, blk, NEG_INF).T
                bias_ref[h, cb] = jnp.where(visible, blk - bound, NEG_INF).T
            biasx_ref[h, BAND_BLOCKS] = jnp.full((QBLK, LANES), NEG_INF, F32)
            bias_ref[h, BAND_BLOCKS] = jnp.full((QBLK, LANES), NEG_INF, F32)
        for r in range(0, d_model, LANES):
            for c in range(0, attn_w, LANES):
                blk = win_ref[r:r + LANES, o_v + c:o_v + c + LANES].astype(F32)
                wvt_s[c:c + LANES, r:r + LANES] = blk.T.astype(BF16)
        gain_s[0] = jnp.broadcast_to(jnp.concatenate([gq_ref[...]] * n_heads, axis=1)
                                     * (HEAD_DIM ** -0.5 * LOG2E), (8, attn_w))
        gain_s[1] = jnp.broadcast_to(jnp.concatenate([gk_ref[...]] * n_heads, axis=1), (8, attn_w))
        k_s[...] = jnp.zeros(k_s.shape, BF16)
        vt_s[...] = jnp.zeros(vt_s.shape, BF16)
        wp_s[...] = jnp.zeros(wp_s.shape, BF16)
        for gi in range(len(POOL_WINDOWS)):
            lo = (gi % 2) * pool_group
            wp_s[gi // 2, lo:lo + pool_group, lo:lo + pool_group] = wpool_ref[gi].astype(BF16)

    @pl.when(t > 0)
    def _():
        u_s[POOL_PAD:POOL_OFF, :] = u_s[TILE + POOL_PAD:TILE + POOL_OFF, :]

    @pl.when(t == 0)
    def _():
        u_s[0:POOL_OFF, :] = jnp.zeros((POOL_OFF, pool_w), F32)
        t2_s[0:POOL_PAD, :] = jnp.zeros((POOL_PAD, pool_w), F32)
        t4_s[0:POOL_PAD, :] = jnp.zeros((POOL_PAD, pool_w), F32)
        t8_s[0:POOL_PAD, :] = jnp.zeros((POOL_PAD, pool_w), F32)

    k_s[0:TILE, :] = k_s[TILE:2 * TILE, :]
    vt_s[:, 0:TILE] = vt_s[:, TILE:2 * TILE]

    gate = mod_ref[:, 2 * d_model:3 * d_model]

    def norm_rows(src_ref, m_ref, r):
        a_row = g_ref[...] * (1.0 + m_ref[:, d_model:2 * d_model])
        xc = src_ref[r:r + NORM_ROWS, :]
        ms = jnp.mean(xc * xc, axis=-1, keepdims=True)
        hc = xc * lax.rsqrt(ms + EPS) * a_row + m_ref[:, 0:d_model]
        h_s[r:r + NORM_ROWS, :] = hc.astype(BF16)
        return hc[0:1, 0:LANES]

    @pl.when((pl.program_id(0) == 0) & (t == 0))
    def _():
        for r in range(0, TILE, NORM_ROWS):
            norm_rows(x_ref, mod_ref, r)

    lane_g = lax.broadcasted_iota(jnp.int32, (SUB_ROWS, GROUP_W), 1)
    lane_h = lax.broadcasted_iota(jnp.int32, (SUB_ROWS, LANES), 1)

    def in_proj(rows, c0, width):
        return jnp.dot(h_s[rows, :], win_ref[:, c0:c0 + width], preferred_element_type=F32)

    def head_rms(xf, gain_row):
        for sub in range(0, xf.shape[0], SUB_ROWS):
            x = xf[sub:sub + SUB_ROWS]
            sq = x * x
            ssq = []
            for c0 in range(0, GROUP_W, LANES):
                blk = sq[:, c0:c0 + LANES]
                first = jnp.sum(jnp.where(lane_h < HEAD_DIM, blk, 0.0), axis=-1, keepdims=True)
                both = jnp.sum(blk, axis=-1, keepdims=True)
                ssq.append(jnp.where(lane_h < HEAD_DIM, first, both - first))
            ssq = jnp.concatenate(ssq, axis=1)
            yield sub, x * lax.rsqrt(ssq * (1.0 / HEAD_DIM) + EPS) * gain_row

    tile_rows = slice(0, TILE)
    qf = in_proj(tile_rows, 0, attn_w)
    for rb in range(0, TILE, QBLK):
        for g in range(n_groups):
            cols = slice(g * GROUP_W, (g + 1) * GROUP_W)
            gain = gain_s[0, 0:1, cols]
            for sub, qn in head_rms(qf[rb:rb + QBLK, cols], gain):
                for h in range(HEADS_PER_GROUP):
                    keep = (lane_g >= h * HEAD_DIM) & (lane_g < (h + 1) * HEAD_DIM)
                    qbd_s[rb // QBLK, g, h * QBLK + sub:h * QBLK + sub + SUB_ROWS, :] = (
                        jnp.where(keep, qn, 0.0).astype(BF16))
    kf = in_proj(tile_rows, o_k, attn_w)
    for rb in range(0, TILE, QBLK):
        for g in range(n_groups):
            cols = slice(g * GROUP_W, (g + 1) * GROUP_W)
            for sub, kn in head_rms(kf[rb:rb + QBLK, cols], gain_s[1, 0:1, cols]):
                row = TILE + rb + sub
                k_s[row:row + SUB_ROWS, cols] = kn.astype(BF16)
    vt_s[:, TILE:2 * TILE] = lax.dot_general(
        wvt_s[...], h_s[...], (((1,), (1,)), ((), ())), preferred_element_type=F32).astype(BF16)
    u_s[POOL_OFF:POOL_OFF + TILE, :] = in_proj(tile_rows, o_u, pool_w)
    for c0 in range(0, attn_w + pool_w, 512):
        z_s[:, c0:c0 + 512] = _silu(in_proj(tile_rows, o_z + c0, 512)).astype(BF16)

    def pool():
        step = 88
        levels = ((u_s, t2_s, 1), (t2_s, t4_s, 2), (t4_s, t8_s, 4), (t8_s, None, 8))
        for gi in range(len(POOL_WINDOWS)):
            assert POOL_WINDOWS[gi] == 2 ** (gi + 1)
        for li, (src, dst, sh) in enumerate(levels[:-1]):
            lanes = slice((li + 1) * pool_group, pool_w)
            for r in range(POOL_PAD, pool_rows, step):
                dst[r:r + step, lanes] = src[r:r + step, lanes] + src[r - sh:r - sh + step, lanes]
        tok = t * TILE + lax.broadcasted_iota(jnp.int32, (QBLK, pool_group), 0)
        for r in range(0, TILE, QBLK):
            for pair in range(len(POOL_WINDOWS) // 2):
                mixed = []
                for gi in (2 * pair, 2 * pair + 1):
                    lanes = slice(gi * pool_group, (gi + 1) * pool_group)
                    src = levels[gi][0]
                    sh = levels[gi][2]
                    rows = slice(POOL_OFF + r, POOL_OFF + r + QBLK)
                    wsum = src[rows, lanes] + src[POOL_OFF + r - sh:POOL_OFF + r - sh + QBLK, lanes]
                    cnt = jnp.minimum(tok + (r + 1), POOL_WINDOWS[gi]).astype(F32)
                    mixed.append((wsum / cnt - u_s[rows, lanes]).astype(BF16))
                pm = jnp.concatenate(mixed, axis=1)
                c0 = pair * 2 * pool_group
                po = jnp.dot(pm, wp_s[pair], preferred_element_type=F32) * ps_ref[:, c0:c0 + 2 * pool_group]
                gz = z_s[r:r + QBLK, attn_w + c0:attn_w + c0 + 2 * pool_group].astype(F32)
                y_s[r:r + QBLK, attn_w + c0:attn_w + c0 + 2 * pool_group] = (po * gz).astype(BF16)

    row_q = lax.broadcasted_iota(jnp.int32, (QBLK, LANES), 0)
    n_qblk = TILE // QBLK

    def band_blocks(jb):
        n_before = TILE // LANES - jb
        return [jnp.where(t == 0, BAND_BLOCKS, cb) if cb < n_before else cb for cb in range(BAND_BLOCKS)]

    def score_dot(jb, g):
        qrow = jb * QBLK
        return lax.dot_general(k_s[qrow:qrow + BANDW, g * GROUP_W:(g + 1) * GROUP_W], qbd_s[jb, g],
                               (((1,), (1,)), ((), ())), preferred_element_type=F32)

    def row_groups_sum(p, acc):
        for r8 in range(0, p.shape[0], 8):
            acc = p[r8:r8 + 8, :] if acc is None else acc + p[r8:r8 + 8, :]
        return acc

    def probs(jb):
        buf = jb % 2
        blk = band_blocks(jb)
        for g in range(n_groups):
            st = score_dot(jb, g)
            for h in range(HEADS_PER_GROUP):
                hh = g * HEADS_PER_GROUP + h
                lanes = slice(h * QBLK, (h + 1) * QBLK)
                lacc = None
                for cb in range(BAND_BLOCKS):
                    for sub in range(0, LANES, SUB_ROWS):
                        rows = slice(cb * LANES + sub, cb * LANES + sub + SUB_ROWS)
                        p = jnp.exp2(st[rows, lanes] + bias_ref[hh, blk[cb], sub:sub + SUB_ROWS, :])
                        p_s[buf, hh // 2, rows, (hh % 2) * QBLK:(hh % 2 + 1) * QBLK] = p.astype(BF16)
                        lacc = row_groups_sum(p, lacc)
                l_s[buf, hh] = jnp.broadcast_to(jnp.sum(lacc, axis=0, keepdims=True), (8, QBLK))

    def probs_exact(jb):
        buf = jb % 2
        blk = band_blocks(jb)
        for g in range(n_groups):
            st = score_dot(jb, g)
            for h in range(HEADS_PER_GROUP):
                hh = g * HEADS_PER_GROUP + h
                lanes = slice(h * QBLK, (h + 1) * QBLK)
                macc = None
                for cb in range(BAND_BLOCKS):
                    for sub in range(0, LANES, SUB_ROWS):
                        rows = slice(cb * LANES + sub, cb * LANES + sub + SUB_ROWS)
                        sb = st[rows, lanes] + biasx_ref[hh, blk[cb], sub:sub + SUB_ROWS, :]
                        s_s[hh, rows, :] = sb
                        for r8 in range(0, SUB_ROWS, 8):
                            macc = sb[r8:r8 + 8, :] if macc is None else jnp.maximum(macc, sb[r8:r8 + 8, :])
                m = jnp.max(macc, axis=0, keepdims=True)
                lacc = None
                for r in range(0, BANDW, SUB_ROWS):
                    rows = slice(r, r + SUB_ROWS)
                    p = jnp.exp2(s_s[hh, rows, :] - m)
                    p_s[buf, hh // 2, rows, (hh % 2) * QBLK:(hh % 2 + 1) * QBLK] = p.astype(BF16)
                    lacc = row_groups_sum(p, lacc)
                l_s[buf, hh] = jnp.broadcast_to(jnp.sum(lacc, axis=0, keepdims=True), (8, QBLK))

    def attend(jb, anchor=None):
        buf = jb % 2
        qrow = jb * QBLK
        lmin = None
        for pair in range(n_heads // 2):
            ha = 2 * pair
            ot = jnp.dot(vt_s[ha * HEAD_DIM:(ha + 2) * HEAD_DIM, qrow:qrow + BANDW], p_s[buf, pair],
                         preferred_element_type=F32)
            l = jnp.concatenate([l_s[buf, ha, 0:1, :], l_s[buf, ha + 1, 0:1, :]], axis=1)
            lmin = l if lmin is None else jnp.minimum(lmin, l)
            linv = 1.0 / l
            if anchor is not None:
                linv = jnp.concatenate([jnp.where(never, anchor(pair), linv[:, 0:LANES]), linv[:, LANES:]], axis=1)
            oa = ot[:, 0:QBLK] * linv[:, 0:QBLK]
            ob = ot[:, QBLK:2 * QBLK] * linv[:, QBLK:2 * QBLK]
            a = jnp.where(row_q < HEAD_DIM, oa, ob).T
            c0 = ha * HEAD_DIM
            gz = z_s[qrow:qrow + QBLK, c0:c0 + LANES].astype(F32)
            y_s[qrow:qrow + QBLK, c0:c0 + LANES] = (a * gz).astype(BF16)
        return jnp.min(lmin)

    never = t < 0
    norm_steps = list(range(0, TILE, NORM_ROWS))
    per_pair = len(norm_steps) // (n_qblk * (n_heads // 2))

    def next_norm(jb):
        def anchor(pair):
            i0 = (jb * (n_heads // 2) + pair) * per_pair
            return sum(norm_rows(xn_ref, modn_ref, rn) for rn in norm_steps[i0:i0 + per_pair])
        return anchor

    probs(0)
    pool()
    underflowed = []
    for jb in range(n_qblk):
        if jb + 1 < n_qblk:
            probs(jb + 1)
        underflowed.append(jnp.logical_not(attend(jb, next_norm(jb)) >= L_FLOOR))

    o = jnp.dot(y_s[...], wout_ref[...], preferred_element_type=F32)
    for c0 in range(0, d_model, GROUP_W):
        cols = slice(c0, c0 + GROUP_W)
        for r in range(0, TILE, SUB_ROWS):
            o_ref[r:r + SUB_ROWS, cols] = x_ref[r:r + SUB_ROWS, cols] + gate[:, cols] * o[r:r + SUB_ROWS, cols]

    for jb in range(n_qblk):
        @pl.when(underflowed[jb])
        def _(jb=jb):
            probs_exact(jb)
            attend(jb)
            rows = slice(jb * QBLK, (jb + 1) * QBLK)
            o_jb = jnp.dot(y_s[rows, :], wout_ref[...], preferred_element_type=F32)
            o_ref[rows, :] = x_ref[rows, :] + gate * o_jb


def _layer(x, mod, norm_g, w_in, q_norm_g, k_norm_g, rel_bias, w_pool, pool_scale, w_out):
    bsz, seq, d_model = x.shape
    n_heads = rel_bias.shape[0]
    attn_w = n_heads * HEAD_DIM
    pool_w = w_pool.shape[0] * w_pool.shape[1]
    pool_group = w_pool.shape[1]
    in_w = w_in.shape[1]
    assert seq % TILE == 0 and attn_w % GROUP_W == 0 and TILE == N_LEFT_CHUNKS * CHUNK
    assert in_w == 3 * attn_w + pool_w + attn_w + pool_w and 2 * pool_group == GROUP_W
    assert attn_w == 512 and pool_w == 512
    n_groups = attn_w // GROUP_W

    n_pairs = w_pool.shape[0] // 2
    assert QBLK <= REL_CLIP <= TILE and rel_bias.shape[1] == 2 * REL_CLIP + 1

    tiles_per_seq = seq // TILE

    def next_tile(b, t):
        n = jnp.minimum(b * tiles_per_seq + t + 1, bsz * tiles_per_seq - 1)
        return n // tiles_per_seq, n % tiles_per_seq

    const2 = lambda b, t: (0, 0)
    const3 = lambda b, t: (0, 0, 0)
    kernel = functools.partial(_block_kernel, d_model=d_model, attn_w=attn_w, pool_w=pool_w)
    return pl.pallas_call(
        kernel,
        grid=(bsz, seq // TILE),
        in_specs=[
            pl.BlockSpec((None, TILE, d_model), lambda b, t: (b, t, 0)),
            pl.BlockSpec((None, TILE, d_model), lambda b, t: (*next_tile(b, t), 0)),
            pl.BlockSpec((None, 1, 3 * d_model), lambda b, t: (b, 0, 0)),
            pl.BlockSpec((None, 1, 3 * d_model), lambda b, t: (next_tile(b, t)[0], 0, 0)),
            pl.BlockSpec((1, d_model), const2),
            pl.BlockSpec((d_model, in_w), const2),
            pl.BlockSpec((attn_w + pool_w, d_model), const2),
            pl.BlockSpec((1, HEAD_DIM), const2),
            pl.BlockSpec((1, HEAD_DIM), const2),
            pl.BlockSpec((n_heads, 2 * REL_CLIP + 1), const2),
            pl.BlockSpec((2 * n_pairs, pool_group, pool_group), const3),
            pl.BlockSpec((1, pool_w), const2),
        ],
        out_specs=pl.BlockSpec((None, TILE, d_model), lambda b, t: (b, t, 0)),
        out_shape=jax.ShapeDtypeStruct(x.shape, x.dtype),
        scratch_shapes=[
            pltpu.VMEM((TILE, d_model), BF16),
            pltpu.VMEM((TILE, attn_w + pool_w), BF16),
            pltpu.VMEM((TILE // QBLK, n_groups, HEADS_PER_GROUP * QBLK, GROUP_W), BF16),
            pltpu.VMEM((2 * TILE, attn_w), BF16),
            pltpu.VMEM((attn_w, 2 * TILE), BF16),
            pltpu.VMEM((TILE + POOL_OFF, pool_w), F32),
            pltpu.VMEM((TILE + POOL_OFF, pool_w), F32),
            pltpu.VMEM((TILE + POOL_OFF, pool_w), F32),
            pltpu.VMEM((TILE + POOL_OFF, pool_w), F32),
            pltpu.VMEM((n_heads, BANDW, QBLK), F32),
            pltpu.VMEM((2, n_heads // 2, BANDW, 2 * QBLK), BF16),
            pltpu.VMEM((2, n_heads, 8, QBLK), F32),
            pltpu.VMEM((TILE, attn_w + pool_w), BF16),
            pltpu.VMEM((n_heads, BAND_BLOCKS + 1, QBLK, LANES), F32),
            pltpu.VMEM((n_heads, BAND_BLOCKS + 1, QBLK, LANES), F32),
            pltpu.VMEM((attn_w, d_model), BF16),
            pltpu.VMEM((2, 8, attn_w), F32),
            pltpu.VMEM((n_pairs, GROUP_W, GROUP_W), BF16),
        ],
        compiler_params=pltpu.CompilerParams(
            dimension_semantics=("arbitrary", "arbitrary"),
            vmem_limit_bytes=V7X_VMEM_LIMIT_BYTES,
        ),
        name="hybrid_block",
    )(x, x, mod, mod, norm_g.reshape(1, d_model), w_in.astype(BF16), w_out.astype(BF16),
      q_norm_g.reshape(1, HEAD_DIM), k_norm_g.reshape(1, HEAD_DIM), rel_bias, w_pool,
      pool_scale.reshape(1, pool_w))


def kernel(x, c, norm_g, w_ada, b_ada, w_in, q_norm_g, k_norm_g, rel_bias, w_pool, pool_scale, w_out):
    depth = w_in.shape[0]
    for l in range(depth):
        mod = _adaln_mod(c, w_ada[l], b_ada[l])
        x = _layer(x, mod, norm_g[l], w_in[l], q_norm_g[l], k_norm_g[l], rel_bias[l],
                   w_pool[l], pool_scale[l], w_out[l])
    return x
```

```python
import functools
import math

import jax
import jax.numpy as jnp
from jax import lax
from jax.experimental import pallas as pl
from jax.experimental.pallas import tpu as pltpu

CHUNK = 64
N_LEFT_CHUNKS = 8
HEAD_DIM = 64
REL_CLIP = 256
POOL_WINDOWS = (2, 4, 8, 16)
EPS = 1e-6
NEG_INF = -1e30
LOG2E = math.log2(math.e)

LANES = 128
TILE = 512
QBLK = 2 * CHUNK
BANDW = TILE + QBLK
BAND_BLOCKS = BANDW // LANES
BIAS_EXT = BANDW + QBLK
HEADS_PER_GROUP = 4
GROUP_W = HEADS_PER_GROUP * HEAD_DIM
POOL_PAD = 8
POOL_HIST = 16
POOL_OFF = POOL_PAD + POOL_HIST
ONES_ROWS = 16
SUB_ROWS = 32
NORM_ROWS = 16
BOUND_MARGIN = 1.02
L_FLOOR = 2.0 ** -90
MOD_BLOCK_W = 1024
V7X_VMEM_LIMIT_BYTES = 56 * 1024 * 1024

F32 = jnp.float32
BF16 = jnp.bfloat16


def _mod_kernel(c_ref, w_ref, b_ref, o_ref):
    res = jnp.dot(c_ref[...], w_ref[...], preferred_element_type=F32) + b_ref[...]
    for b in range(o_ref.shape[0]):
        o_ref[b] = res[b:b + 1, :]


def _adaln_mod(c, w_ada, b_ada):
    bsz, d = c.shape
    n = w_ada.shape[1]
    return pl.pallas_call(
        _mod_kernel,
        grid=(n // MOD_BLOCK_W,),
        in_specs=[
            pl.BlockSpec((bsz, d), lambda i: (0, 0)),
            pl.BlockSpec((d, MOD_BLOCK_W), lambda i: (0, i)),
            pl.BlockSpec((1, MOD_BLOCK_W), lambda i: (0, i)),
        ],
        out_specs=pl.BlockSpec((bsz, 1, MOD_BLOCK_W), lambda i: (0, 0, i)),
        out_shape=jax.ShapeDtypeStruct((bsz, 1, n), F32),
        name="adaln_mod",
    )(c, w_ada, b_ada.reshape(1, n))


def _silu(z):
    return z * (1.0 / (1.0 + jnp.exp(-z)))


def _block_kernel(x_ref, xn_ref, mod_ref, modn_ref, g_ref, win_ref, wout_ref, gq_ref, gk_ref, rb_ref, wpool_ref, ps_ref, o_ref,
                  h_s, z_s, qbd_s, k_s, vt_s, u_s, t2_s, t4_s, t8_s, s_s, p_s, y_s, bias_ref, biasx_ref,
                  wvt_s, gain_s, wp_s,
                  *, d_model, attn_w, pool_w):
    t = pl.program_id(1)
    n_heads = attn_w // HEAD_DIM
    n_groups = attn_w // GROUP_W
    o_k, o_v, o_u, o_z = attn_w, 2 * attn_w, 3 * attn_w, 3 * attn_w + pool_w
    pool_group = pool_w // len(POOL_WINDOWS)
    pool_rows = TILE + POOL_OFF

    @pl.when((pl.program_id(0) == 0) & (t == 0))
    def _():
        a = lax.broadcasted_iota(jnp.int32, (QBLK, LANES), 0)
        lane = lax.broadcasted_iota(jnp.int32, (QBLK, LANES), 1)
        band_lo = jnp.where(a >= CHUNK, CHUNK, 0)
        gq_max = jnp.max(jnp.abs(gq_ref[...]), axis=-1, keepdims=True)
        gk_max = jnp.max(jnp.abs(gk_ref[...]), axis=-1, keepdims=True)
        dot_bound = gq_max * gk_max * (HEAD_DIM ** 0.5 * LOG2E * BOUND_MARGIN)
        for h in range(n_heads):
            first = rb_ref[h:h + 1, 0:1] * LOG2E
            row_ext = jnp.concatenate([
                jnp.broadcast_to(first, (1, TILE - REL_CLIP)),
                rb_ref[h:h + 1, 0:REL_CLIP + QBLK] * LOG2E,
                jnp.broadcast_to(first, (1, BIAS_EXT - BANDW))], axis=1)
            row = jnp.broadcast_to(row_ext, (QBLK, BIAS_EXT))
            skew = pltpu.roll(row, 0, 1, stride=1, stride_axis=0)
            reachable = rb_ref[h:h + 1, 0:REL_CLIP + CHUNK]
            bound = dot_bound + jnp.max(reachable, axis=-1, keepdims=True) * LOG2E + 1.0
            for cb in range(BAND_BLOCKS):
                j = lane + cb * LANES
                visible = (j >= band_lo) & (j < band_lo + (N_LEFT_CHUNKS + 1) * CHUNK)
                blk = skew[:, cb * LANES:(cb + 1) * LANES]
                biasx_ref[h, cb] = jnp.where(visible, blk, NEG_INF).T
                bias_ref[h, cb] = jnp.where(visible, blk - bound, NEG_INF).T
            biasx_ref[h, BAND_BLOCKS] = jnp.full((QBLK, LANES), NEG_INF, F32)
            bias_ref[h, BAND_BLOCKS] = jnp.full((QBLK, LANES), NEG_INF, F32)
        for r in range(0, d_model, LANES):
            for c in range(0, attn_w, LANES):
                blk = win_ref[r:r + LANES, o_v + c:o_v + c + LANES].astype(F32)
                wvt_s[c:c + LANES, r:r + LANES] = blk.T.astype(BF16)
        gain_s[0] = jnp.broadcast_to(jnp.concatenate([gq_ref[...]] * n_heads, axis=1)
                                     * (HEAD_DIM ** -0.5 * LOG2E), (8, attn_w))
        gain_s[1] = jnp.broadcast_to(jnp.concatenate([gk_ref[...]] * n_heads, axis=1), (8, attn_w))
        k_s[...] = jnp.zeros(k_s.shape, BF16)
        vt_s[...] = jnp.zeros(vt_s.shape, BF16)
        wp_s[...] = jnp.zeros(wp_s.shape, BF16)
        for gi in range(len(POOL_WINDOWS)):
            lo = (gi % 2) * pool_group
            wp_s[gi // 2, lo:lo + pool_group, lo:lo + pool_group] = wpool_ref[gi].astype(BF16)

    @pl.when(t > 0)
    def _():
        u_s[POOL_PAD:POOL_OFF, :] = u_s[TILE + POOL_PAD:TILE + POOL_OFF, :]

    @pl.when(t == 0)
    def _():
        u_s[0:POOL_OFF, :] = jnp.zeros((POOL_OFF, pool_w), F32)
        t2_s[0:POOL_PAD, :] = jnp.zeros((POOL_PAD, pool_w), F32)
        t4_s[0:POOL_PAD, :] = jnp.zeros((POOL_PAD, pool_w), F32)
        t8_s[0:POOL_PAD, :] = jnp.zeros((POOL_PAD, pool_w), F32)

    k_s[0:TILE, :] = k_s[TILE:2 * TILE, :]
    vt_s[:, 0:TILE] = vt_s[:, TILE:2 * TILE]

    gate = mod_ref[:, 2 * d_model:3 * d_model]

    def norm_rows(src_ref, m_ref, r):
        a_row = g_ref[...] * (1.0 + m_ref[:, d_model:2 * d_model])
        xc = src_ref[r:r + NORM_ROWS, :]
        ms = jnp.mean(xc * xc, axis=-1, keepdims=True)
        hc = xc * lax.rsqrt(ms + EPS) * a_row + m_ref[:, 0:d_model]
        h_s[r:r + NORM_ROWS, :] = hc.astype(BF16)
        return hc[0:1, 0:LANES]

    @pl.when((pl.program_id(0) == 0) & (t == 0))
    def _():
        for r in range(0, TILE, NORM_ROWS):
            norm_rows(x_ref, mod_ref, r)

    lane_g = lax.broadcasted_iota(jnp.int32, (SUB_ROWS, GROUP_W), 1)
    lane_h = lax.broadcasted_iota(jnp.int32, (SUB_ROWS, LANES), 1)

    def in_proj(rows, c0, width):
        return jnp.dot(h_s[rows, :], win_ref[:, c0:c0 + width], preferred_element_type=F32)

    def head_rms(xf, gain_row):
        for sub in range(0, xf.shape[0], SUB_ROWS):
            x = xf[sub:sub + SUB_ROWS]
            sq = x * x
            ssq = []
            for c0 in range(0, GROUP_W, LANES):
                blk = sq[:, c0:c0 + LANES]
                first = jnp.sum(jnp.where(lane_h < HEAD_DIM, blk, 0.0), axis=-1, keepdims=True)
                both = jnp.sum(blk, axis=-1, keepdims=True)
                ssq.append(jnp.where(lane_h < HEAD_DIM, first, both - first))
            ssq = jnp.concatenate(ssq, axis=1)
            yield sub, x * lax.rsqrt(ssq * (1.0 / HEAD_DIM) + EPS) * gain_row

    tile_rows = slice(0, TILE)
    qf = in_proj(tile_rows, 0, attn_w)
    for rb in range(0, TILE, QBLK):
        for g in range(n_groups):
            cols = slice(g * GROUP_W, (g + 1) * GROUP_W)
            gain = gain_s[0, 0:1, cols]
            for sub, qn in head_rms(qf[rb:rb + QBLK, cols], gain):
                for h in range(HEADS_PER_GROUP):
                    keep = (lane_g >= h * HEAD_DIM) & (lane_g < (h + 1) * HEAD_DIM)
                    qbd_s[rb // QBLK, g, h * QBLK + sub:h * QBLK + sub + SUB_ROWS, :] = (
                        jnp.where(keep, qn, 0.0).astype(BF16))
    kf = in_proj(tile_rows, o_k, attn_w)
    for rb in range(0, TILE, QBLK):
        for g in range(n_groups):
            cols = slice(g * GROUP_W, (g + 1) * GROUP_W)
            for sub, kn in head_rms(kf[rb:rb + QBLK, cols], gain_s[1, 0:1, cols]):
                row = TILE + rb + sub
                k_s[row:row + SUB_ROWS, cols] = kn.astype(BF16)
    vt_s[:, TILE:2 * TILE] = lax.dot_general(
        wvt_s[...], h_s[...], (((1,), (1,)), ((), ())), preferred_element_type=F32).astype(BF16)
    u_s[POOL_OFF:POOL_OFF + TILE, :] = in_proj(tile_rows, o_u, pool_w)
    for c0 in range(0, attn_w + pool_w, 512):
        z_s[:, c0:c0 + 512] = _silu(in_proj(tile_rows, o_z + c0, 512)).astype(BF16)

    def pool():
        step = 88
        levels = ((u_s, t2_s, 1), (t2_s, t4_s, 2), (t4_s, t8_s, 4), (t8_s, None, 8))
        for gi in range(len(POOL_WINDOWS)):
            assert POOL_WINDOWS[gi] == 2 ** (gi + 1)
        for li, (src, dst, sh) in enumerate(levels[:-1]):
            lanes = slice((li + 1) * pool_group, pool_w)
            for r in range(POOL_PAD, pool_rows, step):
                dst[r:r + step, lanes] = src[r:r + step, lanes] + src[r - sh:r - sh + step, lanes]
        tok = t * TILE + lax.broadcasted_iota(jnp.int32, (QBLK, pool_group), 0)
        for r in range(0, TILE, QBLK):
            for pair in range(len(POOL_WINDOWS) // 2):
                mixed = []
                for gi in (2 * pair, 2 * pair + 1):
                    lanes = slice(gi * pool_group, (gi + 1) * pool_group)
                    src = levels[gi][0]
                    sh = levels[gi][2]
                    rows = slice(POOL_OFF + r, POOL_OFF + r + QBLK)
                    wsum = src[rows, lanes] + src[POOL_OFF + r - sh:POOL_OFF + r - sh + QBLK, lanes]
                    cnt = jnp.minimum(tok + (r + 1), POOL_WINDOWS[gi]).astype(F32)
                    mixed.append((wsum / cnt - u_s[rows, lanes]).astype(BF16))
                pm = jnp.concatenate(mixed, axis=1)
                c0 = pair * 2 * pool_group
                po = jnp.dot(pm, wp_s[pair], preferred_element_type=F32) * ps_ref[:, c0:c0 + 2 * pool_group]
                gz = z_s[r:r + QBLK, attn_w + c0:attn_w + c0 + 2 * pool_group].astype(F32)
                y_s[r:r + QBLK, attn_w + c0:attn_w + c0 + 2 * pool_group] = (po * gz).astype(BF16)

    row_q = lax.broadcasted_iota(jnp.int32, (QBLK, LANES), 0)
    n_qblk = TILE // QBLK

    def band_blocks(jb):
        n_before = TILE // LANES - jb
        return [jnp.where(t == 0, BAND_BLOCKS, cb) if cb < n_before else cb for cb in range(BAND_BLOCKS)]

    def score_dot(jb, g):
        qrow = jb * QBLK
        return lax.dot_general(k_s[qrow:qrow + BANDW, g * GROUP_W:(g + 1) * GROUP_W], qbd_s[jb, g],
                               (((1,), (1,)), ((), ())), preferred_element_type=F32)

    def probs(jb):
        buf = jb % 2
        blk = band_blocks(jb)
        for g in range(n_groups):
            st = score_dot(jb, g)
            for h in range(HEADS_PER_GROUP):
                hh = g * HEADS_PER_GROUP + h
                lanes = slice(h * QBLK, (h + 1) * QBLK)
                for cb in range(BAND_BLOCKS):
                    for sub in range(0, LANES, SUB_ROWS):
                        rows = slice(cb * LANES + sub, cb * LANES + sub + SUB_ROWS)
                        p = jnp.exp2(st[rows, lanes] + bias_ref[hh, blk[cb], sub:sub + SUB_ROWS, :])
                        p_s[buf, hh // 2, rows, (hh % 2) * QBLK:(hh % 2 + 1) * QBLK] = p.astype(BF16)

    def probs_exact(jb):
        buf = jb % 2
        blk = band_blocks(jb)
        for g in range(n_groups):
            st = score_dot(jb, g)
            for h in range(HEADS_PER_GROUP):
                hh = g * HEADS_PER_GROUP + h
                lanes = slice(h * QBLK, (h + 1) * QBLK)
                macc = None
                for cb in range(BAND_BLOCKS):
                    for sub in range(0, LANES, SUB_ROWS):
                        rows = slice(cb * LANES + sub, cb * LANES + sub + SUB_ROWS)
                        sb = st[rows, lanes] + biasx_ref[hh, blk[cb], sub:sub + SUB_ROWS, :]
                        s_s[hh, rows, :] = sb
                        for r8 in range(0, SUB_ROWS, 8):
                            macc = sb[r8:r8 + 8, :] if macc is None else jnp.maximum(macc, sb[r8:r8 + 8, :])
                m = jnp.max(macc, axis=0, keepdims=True)
                for r in range(0, BANDW, SUB_ROWS):
                    rows = slice(r, r + SUB_ROWS)
                    p = jnp.exp2(s_s[hh, rows, :] - m)
                    p_s[buf, hh // 2, rows, (hh % 2) * QBLK:(hh % 2 + 1) * QBLK] = p.astype(BF16)

    ones_rows = jnp.ones((ONES_ROWS, BANDW), BF16)

    def attend(jb, anchor=None):
        buf = jb % 2
        qrow = jb * QBLK
        lmin = None
        for pair in range(n_heads // 2):
            ha = 2 * pair
            vt1 = jnp.concatenate(
                [vt_s[ha * HEAD_DIM:(ha + 2) * HEAD_DIM, qrow:qrow + BANDW], ones_rows], axis=0)
            ot = jnp.dot(vt1, p_s[buf, pair], preferred_element_type=F32)
            l = ot[2 * HEAD_DIM:2 * HEAD_DIM + 1, :]
            lmin = l if lmin is None else jnp.minimum(lmin, l)
            linv = 1.0 / l
            if anchor is not None:
                linv = jnp.concatenate([jnp.where(never, anchor(pair), linv[:, 0:LANES]), linv[:, LANES:]], axis=1)
            oa = ot[0:2 * HEAD_DIM, 0:QBLK] * linv[:, 0:QBLK]
            ob = ot[0:2 * HEAD_DIM, QBLK:2 * QBLK] * linv[:, QBLK:2 * QBLK]
            a = jnp.where(row_q < HEAD_DIM, oa, ob).T
            c0 = ha * HEAD_DIM
            gz = z_s[qrow:qrow + QBLK, c0:c0 + LANES].astype(F32)
            y_s[qrow:qrow + QBLK, c0:c0 + LANES] = (a * gz).astype(BF16)
        return jnp.min(lmin)

    never = t < 0
    norm_steps = list(range(0, TILE, NORM_ROWS))
    per_pair = len(norm_steps) // (n_qblk * (n_heads // 2))

    def next_norm(jb):
        def anchor(pair):
            i0 = (jb * (n_heads // 2) + pair) * per_pair
            return sum(norm_rows(xn_ref, modn_ref, rn) for rn in norm_steps[i0:i0 + per_pair])
        return anchor

    probs(0)
    pool()
    underflowed = []
    for jb in range(n_qblk):
        if jb + 1 < n_qblk:
            probs(jb + 1)
        underflowed.append(jnp.logical_not(attend(jb, next_norm(jb)) >= L_FLOOR))

    o = jnp.dot(y_s[...], wout_ref[...], preferred_element_type=F32)
    for c0 in range(0, d_model, GROUP_W):
        cols = slice(c0, c0 + GROUP_W)
        for r in range(0, TILE, SUB_ROWS):
            o_ref[r:r + SUB_ROWS, cols] = x_ref[r:r + SUB_ROWS, cols] + gate[:, cols] * o[r:r + SUB_ROWS, cols]

    for jb in range(n_qblk):
        @pl.when(underflowed[jb])
        def _(jb=jb):
            probs_exact(jb)
            attend(jb)
            rows = slice(jb * QBLK, (jb + 1) * QBLK)
            o_jb = jnp.dot(y_s[rows, :], wout_ref[...], preferred_element_type=F32)
            o_ref[rows, :] = x_ref[rows, :] + gate * o_jb


def _layer(x, mod, norm_g, w_in, q_norm_g, k_norm_g, rel_bias, w_pool, pool_scale, w_out):
    bsz, seq, d_model = x.shape
    n_heads = rel_bias.shape[0]
    attn_w = n_heads * HEAD_DIM
    pool_w = w_pool.shape[0] * w_pool.shape[1]
    pool_group = w_pool.shape[1]
    in_w = w_in.shape[1]
    assert seq % TILE == 0 and attn_w % GROUP_W == 0 and TILE == N_LEFT_CHUNKS * CHUNK
    assert in_w == 3 * attn_w + pool_w + attn_w + pool_w and 2 * pool_group == GROUP_W
    assert attn_w == 512 and pool_w == 512
    n_groups = attn_w // GROUP_W

    n_pairs = w_pool.shape[0] // 2
    assert QBLK <= REL_CLIP <= TILE and rel_bias.shape[1] == 2 * REL_CLIP + 1

    tiles_per_seq = seq // TILE

    def next_tile(b, t):
        n = jnp.minimum(b * tiles_per_seq + t + 1, bsz * tiles_per_seq - 1)
        return n // tiles_per_seq, n % tiles_per_seq

    const2 = lambda b, t: (0, 0)
    const3 = lambda b, t: (0, 0, 0)
    kernel = functools.partial(_block_kernel, d_model=d_model, attn_w=attn_w, pool_w=pool_w)
    return pl.pallas_call(
        kernel,
        grid=(bsz, seq // TILE),
        in_specs=[
            pl.BlockSpec((None, TILE, d_model), lambda b, t: (b, t, 0)),
            pl.BlockSpec((None, TILE, d_model), lambda b, t: (*next_tile(b, t), 0)),
            pl.BlockSpec((None, 1, 3 * d_model), lambda b, t: (b, 0, 0)),
            pl.BlockSpec((None, 1, 3 * d_model), lambda b, t: (next_tile(b, t)[0], 0, 0)),
            pl.BlockSpec((1, d_model), const2),
            pl.BlockSpec((d_model, in_w), const2),
            pl.BlockSpec((attn_w + pool_w, d_model), const2),
            pl.BlockSpec((1, HEAD_DIM), const2),
            pl.BlockSpec((1, HEAD_DIM), const2),
            pl.BlockSpec((n_heads, 2 * REL_CLIP + 1), const2),
            pl.BlockSpec((2 * n_pairs, pool_group, pool_group), const3),
            pl.BlockSpec((1, pool_w), const2),
        ],
        out_specs=pl.BlockSpec((None, TILE, d_model), lambda b, t: (b, t, 0)),
        out_shape=jax.ShapeDtypeStruct(x.shape, x.dtype),
        scratch_shapes=[
            pltpu.VMEM((TILE, d_model), BF16),
            pltpu.VMEM((TILE, attn_w + pool_w), BF16),
            pltpu.VMEM((TILE // QBLK, n_groups, HEADS_PER_GROUP * QBLK, GROUP_W), BF16),
            pltpu.VMEM((2 * TILE, attn_w), BF16),
            pltpu.VMEM((attn_w, 2 * TILE), BF16),
            pltpu.VMEM((TILE + POOL_OFF, pool_w), F32),
            pltpu.VMEM((TILE + POOL_OFF, pool_w), F32),
            pltpu.VMEM((TILE + POOL_OFF, pool_w), F32),
            pltpu.VMEM((TILE + POOL_OFF, pool_w), F32),
            pltpu.VMEM((n_heads, BANDW, QBLK), F32),
            pltpu.VMEM((2, n_heads // 2, BANDW, 2 * QBLK), BF16),
            pltpu.VMEM((TILE, attn_w + pool_w), BF16),
            pltpu.VMEM((n_heads, BAND_BLOCKS + 1, QBLK, LANES), F32),
            pltpu.VMEM((n_heads, BAND_BLOCKS + 1, QBLK, LANES), F32),
            pltpu.VMEM((attn_w, d_model), BF16),
            pltpu.VMEM((2, 8, attn_w), F32),
            pltpu.VMEM((n_pairs, GROUP_W, GROUP_W), BF16),
        ],
        compiler_params=pltpu.CompilerParams(
            dimension_semantics=("arbitrary", "arbitrary"),
            vmem_limit_bytes=V7X_VMEM_LIMIT_BYTES,
        ),
        name="hybrid_block",
    )(x, x, mod, mod, norm_g.reshape(1, d_model), w_in.astype(BF16), w_out.astype(BF16),
      q_norm_g.reshape(1, HEAD_DIM), k_norm_g.reshape(1, HEAD_DIM), rel_bias, w_pool,
      pool_scale.reshape(1, pool_w))


def kernel(x, c, norm_g, w_ada, b_ada, w_in, q_norm_g, k_norm_g, rel_bias, w_pool, pool_scale, w_out):
    depth = w_in.shape[0]
    for l in range(depth):
        mod = _adaln_mod(c, w_ada[l], b_ada[l])
        x = _layer(x, mod, norm_g[l], w_in[l], q_norm_g[l], k_norm_g[l], rel_bias[l],
                   w_pool[l], pool_scale[l], w_out[l])
    return x
```

```python
import functools
import math

import jax
import jax.numpy as jnp
from jax import lax
from jax.experimental import pallas as pl
from jax.experimental.pallas import tpu as pltpu

CHUNK = 64
N_LEFT_CHUNKS = 8
HEAD_DIM = 64
REL_CLIP = 256
POOL_WINDOWS = (2, 4, 8, 16)
EPS = 1e-6
NEG_INF = -1e30
LOG2E = math.log2(math.e)

LANES = 128
TILE = 512
QBLK = 2 * CHUNK
BANDW = TILE + QBLK
BAND_BLOCKS = BANDW // LANES
CBAND = (N_LEFT_CHUNKS + 1) * CHUNK
CBLOCKS = CBAND // CHUNK
BIAS_EXT = BANDW + QBLK
HEADS_PER_GROUP = 4
GROUP_W = HEADS_PER_GROUP * HEAD_DIM
POOL_PAD = 8
POOL_HIST = 16
POOL_OFF = POOL_PAD + POOL_HIST
ONES_ROWS = 16
SUB_ROWS = 32
NORM_ROWS = 16
BOUND_MARGIN = 1.02
L_FLOOR = 2.0 ** -90
MOD_BLOCK_W = 1024
V7X_VMEM_LIMIT_BYTES = 56 * 1024 * 1024

F32 = jnp.float32
BF16 = jnp.bfloat16


def _mod_kernel(c_ref, w_ref, b_ref, o_ref):
    res = jnp.dot(c_ref[...], w_ref[...], preferred_element_type=F32) + b_ref[...]
    for b in range(o_ref.shape[0]):
        o_ref[b] = res[b:b + 1, :]


def _adaln_mod(c, w_ada, b_ada):
    bsz, d = c.shape
    n = w_ada.shape[1]
    return pl.pallas_call(
        _mod_kernel,
        grid=(n // MOD_BLOCK_W,),
        in_specs=[
            pl.BlockSpec((bsz, d), lambda i: (0, 0)),
            pl.BlockSpec((d, MOD_BLOCK_W), lambda i: (0, i)),
            pl.BlockSpec((1, MOD_BLOCK_W), lambda i: (0, i)),
        ],
        out_specs=pl.BlockSpec((bsz, 1, MOD_BLOCK_W), lambda i: (0, 0, i)),
        out_shape=jax.ShapeDtypeStruct((bsz, 1, n), F32),
        name="adaln_mod",
    )(c, w_ada, b_ada.reshape(1, n))


def _silu(z):
    return z * (1.0 / (1.0 + jnp.exp(-z)))


def _block_kernel(x_ref, xn_ref, mod_ref, modn_ref, g_ref, win_ref, wout_ref, gq_ref, gk_ref, rb_ref, wpool_ref, ps_ref, o_ref,
                  h_s, z_s, qbd_s, k_s, vt_s, u_s, t2_s, t4_s, t8_s, s_s, p_s, y_s, bias_ref, biasx_ref,
                  wvt_s, gain_s, wp_s,
                  *, d_model, attn_w, pool_w):
    t = pl.program_id(1)
    n_heads = attn_w // HEAD_DIM
    n_groups = attn_w // GROUP_W
    o_k, o_v, o_u, o_z = attn_w, 2 * attn_w, 3 * attn_w, 3 * attn_w + pool_w
    pool_group = pool_w // len(POOL_WINDOWS)
    pool_rows = TILE + POOL_OFF

    @pl.when((pl.program_id(0) == 0) & (t == 0))
    def _():
        gq_max = jnp.max(jnp.abs(gq_ref[...]), axis=-1, keepdims=True)
        gk_max = jnp.max(jnp.abs(gk_ref[...]), axis=-1, keepdims=True)
        dot_bound = gq_max * gk_max * (HEAD_DIM ** 0.5 * LOG2E * BOUND_MARGIN)
        for g in range(n_groups):
            plain = [[None] * HEADS_PER_GROUP for _ in range(CBLOCKS)]
            shifted = [[None] * HEADS_PER_GROUP for _ in range(CBLOCKS)]
            for hh in range(HEADS_PER_GROUP):
                h = g * HEADS_PER_GROUP + hh
                first = rb_ref[h:h + 1, 0:1] * LOG2E
                row_ext = jnp.concatenate([
                    jnp.broadcast_to(first, (1, TILE - REL_CLIP)),
                    rb_ref[h:h + 1, 0:REL_CLIP + QBLK] * LOG2E,
                    jnp.broadcast_to(first, (1, BIAS_EXT - BANDW))], axis=1)
                row = jnp.broadcast_to(row_ext, (QBLK, BIAS_EXT))
                skew = pltpu.roll(row, 0, 1, stride=1, stride_axis=0)
                reachable = rb_ref[h:h + 1, 0:REL_CLIP + CHUNK]
                bound = dot_bound + jnp.max(reachable, axis=-1, keepdims=True) * LOG2E + 1.0
                for m in range(BAND_BLOCKS):
                    keys_by_query = skew[:, m * LANES:(m + 1) * LANES].T[:, 0:CHUNK]
                    for half in range(LANES // CHUNK):
                        kb = m * (LANES // CHUNK) + half
                        if kb < CBLOCKS:
                            piece = keys_by_query[half * CHUNK:(half + 1) * CHUNK, :]
                            plain[kb][hh] = piece
                            shifted[kb][hh] = piece - bound
            for kb in range(CBLOCKS):
                biasx_ref[g, kb] = jnp.concatenate(plain[kb], axis=1)
                bias_ref[g, kb] = jnp.concatenate(shifted[kb], axis=1)
            biasx_ref[g, CBLOCKS] = jnp.full((CHUNK, GROUP_W), NEG_INF, F32)
            bias_ref[g, CBLOCKS] = jnp.full((CHUNK, GROUP_W), NEG_INF, F32)
        p_s[...] = jnp.zeros(p_s.shape, BF16)
        for r in range(0, d_model, LANES):
            for c in range(0, attn_w, LANES):
                blk = win_ref[r:r + LANES, o_v + c:o_v + c + LANES].astype(F32)
                wvt_s[c:c + LANES, r:r + LANES] = blk.T.astype(BF16)
        gain_s[0] = jnp.broadcast_to(jnp.concatenate([gq_ref[...]] * n_heads, axis=1)
                                     * (HEAD_DIM ** -0.5 * LOG2E), (8, attn_w))
        gain_s[1] = jnp.broadcast_to(jnp.concatenate([gk_ref[...]] * n_heads, axis=1), (8, attn_w))
        k_s[...] = jnp.zeros(k_s.shape, BF16)
        vt_s[...] = jnp.zeros(vt_s.shape, BF16)
        wp_s[...] = jnp.zeros(wp_s.shape, BF16)
        for gi in range(len(POOL_WINDOWS)):
            lo = (gi % 2) * pool_group
            wp_s[gi // 2, lo:lo + pool_group, lo:lo + pool_group] = wpool_ref[gi].astype(BF16)

    @pl.when(t > 0)
    def _():
        u_s[POOL_PAD:POOL_OFF, :] = u_s[TILE + POOL_PAD:TILE + POOL_OFF, :]

    @pl.when(t == 0)
    def _():
        u_s[0:POOL_OFF, :] = jnp.zeros((POOL_OFF, pool_w), F32)
        t2_s[0:POOL_PAD, :] = jnp.zeros((POOL_PAD, pool_w), F32)
        t4_s[0:POOL_PAD, :] = jnp.zeros((POOL_PAD, pool_w), F32)
        t8_s[0:POOL_PAD, :] = jnp.zeros((POOL_PAD, pool_w), F32)

    k_s[0:TILE, :] = k_s[TILE:2 * TILE, :]
    vt_s[:, 0:TILE] = vt_s[:, TILE:2 * TILE]

    gate = mod_ref[:, 2 * d_model:3 * d_model]

    def norm_rows(src_ref, m_ref, r):
        a_row = g_ref[...] * (1.0 + m_ref[:, d_model:2 * d_model])
        xc = src_ref[r:r + NORM_ROWS, :]
        ms = jnp.mean(xc * xc, axis=-1, keepdims=True)
        hc = xc * lax.rsqrt(ms + EPS) * a_row + m_ref[:, 0:d_model]
        h_s[r:r + NORM_ROWS, :] = hc.astype(BF16)
        return hc[0:1, 0:LANES]

    @pl.when((pl.program_id(0) == 0) & (t == 0))
    def _():
        for r in range(0, TILE, NORM_ROWS):
            norm_rows(x_ref, mod_ref, r)

    lane_g = lax.broadcasted_iota(jnp.int32, (SUB_ROWS, GROUP_W), 1)
    lane_h = lax.broadcasted_iota(jnp.int32, (SUB_ROWS, LANES), 1)

    def in_proj(rows, c0, width):
        return jnp.dot(h_s[rows, :], win_ref[:, c0:c0 + width], preferred_element_type=F32)

    def head_rms(xf, gain_row):
        for sub in range(0, xf.shape[0], SUB_ROWS):
            x = xf[sub:sub + SUB_ROWS]
            sq = x * x
            ssq = []
            for c0 in range(0, GROUP_W, LANES):
                blk = sq[:, c0:c0 + LANES]
                first = jnp.sum(jnp.where(lane_h < HEAD_DIM, blk, 0.0), axis=-1, keepdims=True)
                both = jnp.sum(blk, axis=-1, keepdims=True)
                ssq.append(jnp.where(lane_h < HEAD_DIM, first, both - first))
            ssq = jnp.concatenate(ssq, axis=1)
            yield sub, x * lax.rsqrt(ssq * (1.0 / HEAD_DIM) + EPS) * gain_row

    tile_rows = slice(0, TILE)
    qf = in_proj(tile_rows, 0, attn_w)
    for rb in range(0, TILE, QBLK):
        for g in range(n_groups):
            cols = slice(g * GROUP_W, (g + 1) * GROUP_W)
            gain = gain_s[0, 0:1, cols]
            for sub, qn in head_rms(qf[rb:rb + QBLK, cols], gain):
                c, qq = divmod(sub, CHUNK)
                for h in range(HEADS_PER_GROUP):
                    keep = (lane_g >= h * HEAD_DIM) & (lane_g < (h + 1) * HEAD_DIM)
                    qbd_s[rb // QBLK, g, c, h * CHUNK + qq:h * CHUNK + qq + SUB_ROWS, :] = (
                        jnp.where(keep, qn, 0.0).astype(BF16))
    kf = in_proj(tile_rows, o_k, attn_w)
    for rb in range(0, TILE, QBLK):
        for g in range(n_groups):
            cols = slice(g * GROUP_W, (g + 1) * GROUP_W)
            for sub, kn in head_rms(kf[rb:rb + QBLK, cols], gain_s[1, 0:1, cols]):
                row = TILE + rb + sub
                k_s[row:row + SUB_ROWS, cols] = kn.astype(BF16)
    vt_s[:, TILE:2 * TILE] = lax.dot_general(
        wvt_s[...], h_s[...], (((1,), (1,)), ((), ())), preferred_element_type=F32).astype(BF16)
    u_s[POOL_OFF:POOL_OFF + TILE, :] = in_proj(tile_rows, o_u, pool_w)
    for c0 in range(0, attn_w + pool_w, 512):
        z_s[:, c0:c0 + 512] = _silu(in_proj(tile_rows, o_z + c0, 512)).astype(BF16)

    def pool():
        step = 88
        levels = ((u_s, t2_s, 1), (t2_s, t4_s, 2), (t4_s, t8_s, 4), (t8_s, None, 8))
        for gi in range(len(POOL_WINDOWS)):
            assert POOL_WINDOWS[gi] == 2 ** (gi + 1)
        for li, (src, dst, sh) in enumerate(levels[:-1]):
            lanes = slice((li + 1) * pool_group, pool_w)
            for r in range(POOL_PAD, pool_rows, step):
                dst[r:r + step, lanes] = src[r:r + step, lanes] + src[r - sh:r - sh + step, lanes]
        tok = t * TILE + lax.broadcasted_iota(jnp.int32, (QBLK, pool_group), 0)
        for r in range(0, TILE, QBLK):
            for pair in range(len(POOL_WINDOWS) // 2):
                mixed = []
                for gi in (2 * pair, 2 * pair + 1):
                    lanes = slice(gi * pool_group, (gi + 1) * pool_group)
                    src = levels[gi][0]
                    sh = levels[gi][2]
                    rows = slice(POOL_OFF + r, POOL_OFF + r + QBLK)
                    wsum = src[rows, lanes] + src[POOL_OFF + r - sh:POOL_OFF + r - sh + QBLK, lanes]
                    cnt = jnp.minimum(tok + (r + 1), POOL_WINDOWS[gi]).astype(F32)
                    mixed.append((wsum / cnt - u_s[rows, lanes]).astype(BF16))
                pm = jnp.concatenate(mixed, axis=1)
                c0 = pair * 2 * pool_group
                po = jnp.dot(pm, wp_s[pair], preferred_element_type=F32) * ps_ref[:, c0:c0 + 2 * pool_group]
                gz = z_s[r:r + QBLK, attn_w + c0:attn_w + c0 + 2 * pool_group].astype(F32)
                y_s[r:r + QBLK, attn_w + c0:attn_w + c0 + 2 * pool_group] = (po * gz).astype(BF16)

    row_q = lax.broadcasted_iota(jnp.int32, (QBLK, LANES), 0)
    lane_q = lax.broadcasted_iota(jnp.int32, (QBLK, LANES), 1)
    n_qblk = TILE // QBLK
    n_chunks = QBLK // CHUNK

    def key_blocks(jb, c):
        n_before = (TILE - jb * QBLK - c * CHUNK) // CHUNK
        return [jnp.where(t == 0, CBLOCKS, kb) if kb < n_before else kb for kb in range(CBLOCKS)]

    def score_dot(jb, g, c):
        k0 = jb * QBLK + c * CHUNK
        return lax.dot_general(k_s[k0:k0 + CBAND, g * GROUP_W:(g + 1) * GROUP_W], qbd_s[jb, g, c],
                               (((1,), (1,)), ((), ())), preferred_element_type=F32)

    def put_probs(jb, g, c, kb, j, p):
        rows = slice((c + kb) * CHUNK, (c + kb + 1) * CHUNK)
        p_s[jb % 2, g * (GROUP_W // LANES) + j, rows, c * LANES:(c + 1) * LANES] = p.astype(BF16)

    def probs(jb):
        for g in range(n_groups):
            for c in range(n_chunks):
                st = score_dot(jb, g, c)
                blk = key_blocks(jb, c)
                for kb in range(CBLOCKS):
                    rows = slice(kb * CHUNK, (kb + 1) * CHUNK)
                    for j in range(GROUP_W // LANES):
                        lanes = slice(j * LANES, (j + 1) * LANES)
                        put_probs(jb, g, c, kb, j, jnp.exp2(st[rows, lanes] + bias_ref[g, blk[kb], :, lanes]))

    def probs_exact(jb):
        for g in range(n_groups):
            for c in range(n_chunks):
                st = score_dot(jb, g, c)
                blk = key_blocks(jb, c)
                for j in range(GROUP_W // LANES):
                    lanes = slice(j * LANES, (j + 1) * LANES)
                    macc = None
                    for kb in range(CBLOCKS):
                        rows = slice(kb * CHUNK, (kb + 1) * CHUNK)
                        sb = st[rows, lanes] + biasx_ref[g, blk[kb], :, lanes]
                        s_s[rows, lanes] = sb
                        for r8 in range(0, CHUNK, 8):
                            macc = sb[r8:r8 + 8, :] if macc is None else jnp.maximum(macc, sb[r8:r8 + 8, :])
                    m = jnp.max(macc, axis=0, keepdims=True)
                    for kb in range(CBLOCKS):
                        rows = slice(kb * CHUNK, (kb + 1) * CHUNK)
                        put_probs(jb, g, c, kb, j, jnp.exp2(s_s[rows, lanes] - m))

    ones_rows = jnp.ones((ONES_ROWS, BANDW), BF16)

    def attend(jb, anchor=None):
        buf = jb % 2
        qrow = jb * QBLK
        lmin = None
        for pair in range(n_heads // 2):
            ha = 2 * pair
            vt1 = jnp.concatenate(
                [vt_s[ha * HEAD_DIM:(ha + 2) * HEAD_DIM, qrow:qrow + BANDW], ones_rows], axis=0)
            ot = jnp.dot(vt1, p_s[buf, pair], preferred_element_type=F32)
            l = ot[2 * HEAD_DIM:2 * HEAD_DIM + 1, :]
            lmin = l if lmin is None else jnp.minimum(lmin, l)
            linv = 1.0 / l
            if anchor is not None:
                linv = jnp.concatenate([jnp.where(never, anchor(pair), linv[:, 0:LANES]), linv[:, LANES:]], axis=1)
            c0q = ot[0:2 * HEAD_DIM, 0:LANES] * linv[:, 0:LANES]
            c1q = ot[0:2 * HEAD_DIM, LANES:2 * LANES] * linv[:, LANES:]
            first_head = lane_q < HEAD_DIM
            top = jnp.where(first_head, c0q, pltpu.roll(c1q, HEAD_DIM, 1))
            bot = jnp.where(first_head, pltpu.roll(c0q, HEAD_DIM, 1), c1q)
            a = jnp.where(row_q < HEAD_DIM, top, bot).T
            c0 = ha * HEAD_DIM
            gz = z_s[qrow:qrow + QBLK, c0:c0 + LANES].astype(F32)
            y_s[qrow:qrow + QBLK, c0:c0 + LANES] = (a * gz).astype(BF16)
        return jnp.min(lmin)

    never = t < 0
    norm_steps = list(range(0, TILE, NORM_ROWS))
    per_pair = len(norm_steps) // (n_qblk * (n_heads // 2))

    def next_norm(jb):
        def anchor(pair):
            i0 = (jb * (n_heads // 2) + pair) * per_pair
            return sum(norm_rows(xn_ref, modn_ref, rn) for rn in norm_steps[i0:i0 + per_pair])
        return anchor

    probs(0)
    pool()
    underflowed = []
    for jb in range(n_qblk):
        if jb + 1 < n_qblk:
            probs(jb + 1)
        underflowed.append(jnp.logical_not(attend(jb, next_norm(jb)) >= L_FLOOR))

    o = jnp.dot(y_s[...], wout_ref[...], preferred_element_type=F32)
    for c0 in range(0, d_model, GROUP_W):
        cols = slice(c0, c0 + GROUP_W)
        for r in range(0, TILE, SUB_ROWS):
            o_ref[r:r + SUB_ROWS, cols] = x_ref[r:r + SUB_ROWS, cols] + gate[:, cols] * o[r:r + SUB_ROWS, cols]

    for jb in range(n_qblk):
        @pl.when(underflowed[jb])
        def _(jb=jb):
            probs_exact(jb)
            attend(jb)
            rows = slice(jb * QBLK, (jb + 1) * QBLK)
            o_jb = jnp.dot(y_s[rows, :], wout_ref[...], preferred_element_type=F32)
            o_ref[rows, :] = x_ref[rows, :] + gate * o_jb


def _layer(x, mod, norm_g, w_in, q_norm_g, k_norm_g, rel_bias, w_pool, pool_scale, w_out):
    bsz, seq, d_model = x.shape
    n_heads = rel_bias.shape[0]
    attn_w = n_heads * HEAD_DIM
    pool_w = w_pool.shape[0] * w_pool.shape[1]
    pool_group = w_pool.shape[1]
    in_w = w_in.shape[1]
    assert seq % TILE == 0 and attn_w % GROUP_W == 0 and TILE == N_LEFT_CHUNKS * CHUNK
    assert in_w == 3 * attn_w + pool_w + attn_w + pool_w and 2 * pool_group == GROUP_W
    assert attn_w == 512 and pool_w == 512
    n_groups = attn_w // GROUP_W

    n_pairs = w_pool.shape[0] // 2
    assert QBLK <= REL_CLIP <= TILE and rel_bias.shape[1] == 2 * REL_CLIP + 1

    tiles_per_seq = seq // TILE

    def next_tile(b, t):
        n = jnp.minimum(b * tiles_per_seq + t + 1, bsz * tiles_per_seq - 1)
        return n // tiles_per_seq, n % tiles_per_seq

    const2 = lambda b, t: (0, 0)
    const3 = lambda b, t: (0, 0, 0)
    kernel = functools.partial(_block_kernel, d_model=d_model, attn_w=attn_w, pool_w=pool_w)
    return pl.pallas_call(
        kernel,
        grid=(bsz, seq // TILE),
        in_specs=[
            pl.BlockSpec((None, TILE, d_model), lambda b, t: (b, t, 0)),
            pl.BlockSpec((None, TILE, d_model), lambda b, t: (*next_tile(b, t), 0)),
            pl.BlockSpec((None, 1, 3 * d_model), lambda b, t: (b, 0, 0)),
            pl.BlockSpec((None, 1, 3 * d_model), lambda b, t: (next_tile(b, t)[0], 0, 0)),
            pl.BlockSpec((1, d_model), const2),
            pl.BlockSpec((d_model, in_w), const2),
            pl.BlockSpec((attn_w + pool_w, d_model), const2),
            pl.BlockSpec((1, HEAD_DIM), const2),
            pl.BlockSpec((1, HEAD_DIM), const2),
            pl.BlockSpec((n_heads, 2 * REL_CLIP + 1), const2),
            pl.BlockSpec((2 * n_pairs, pool_group, pool_group), const3),
            pl.BlockSpec((1, pool_w), const2),
        ],
        out_specs=pl.BlockSpec((None, TILE, d_model), lambda b, t: (b, t, 0)),
        out_shape=jax.ShapeDtypeStruct(x.shape, x.dtype),
        scratch_shapes=[
            pltpu.VMEM((TILE, d_model), BF16),
            pltpu.VMEM((TILE, attn_w + pool_w), BF16),
            pltpu.VMEM((TILE // QBLK, n_groups, QBLK // CHUNK, HEADS_PER_GROUP * CHUNK, GROUP_W), BF16),
            pltpu.VMEM((2 * TILE, attn_w), BF16),
            pltpu.VMEM((attn_w, 2 * TILE), BF16),
            pltpu.VMEM((TILE + POOL_OFF, pool_w), F32),
            pltpu.VMEM((TILE + POOL_OFF, pool_w), F32),
            pltpu.VMEM((TILE + POOL_OFF, pool_w), F32),
            pltpu.VMEM((TILE + POOL_OFF, pool_w), F32),
            pltpu.VMEM((CBAND, GROUP_W), F32),
            pltpu.VMEM((2, n_heads // 2, BANDW, 2 * QBLK), BF16),
            pltpu.VMEM((TILE, attn_w + pool_w), BF16),
            pltpu.VMEM((n_groups, CBLOCKS + 1, CHUNK, GROUP_W), F32),
            pltpu.VMEM((n_groups, CBLOCKS + 1, CHUNK, GROUP_W), F32),
            pltpu.VMEM((attn_w, d_model), BF16),
            pltpu.VMEM((2, 8, attn_w), F32),
            pltpu.VMEM((n_pairs, GROUP_W, GROUP_W), BF16),
        ],
        compiler_params=pltpu.CompilerParams(
            dimension_semantics=("arbitrary", "arbitrary"),
            vmem_limit_bytes=V7X_VMEM_LIMIT_BYTES,
        ),
        name="hybrid_block",
    )(x, x, mod, mod, norm_g.reshape(1, d_model), w_in.astype(BF16), w_out.astype(BF16),
      q_norm_g.reshape(1, HEAD_DIM), k_norm_g.reshape(1, HEAD_DIM), rel_bias, w_pool,
      pool_scale.reshape(1, pool_w))


def kernel(x, c, norm_g, w_ada, b_ada, w_in, q_norm_g, k_norm_g, rel_bias, w_pool, pool_scale, w_out):
    depth = w_in.shape[0]
    for l in range(depth):
        mod = _adaln_mod(c, w_ada[l], b_ada[l])
        x = _layer(x, mod, norm_g[l], w_in[l], q_norm_g[l], k_norm_g[l], rel_bias[l],
                   w_pool[l], pool_scale[l], w_out[l])
    return x
```

```python
import functools
import math

import jax
import jax.numpy as jnp
from jax import lax
from jax.experimental import pallas as pl
from jax.experimental.pallas import tpu as pltpu

CHUNK = 64
N_LEFT_CHUNKS = 8
HEAD_DIM = 64
REL_CLIP = 256
POOL_WINDOWS = (2, 4, 8, 16)
EPS = 1e-6
NEG_INF = -1e30
LOG2E = math.log2(math.e)

LANES = 128
TILE = 512
QBLK = 2 * CHUNK
BANDW = TILE + QBLK
BAND_BLOCKS = BANDW // LANES
BIAS_EXT = BANDW + QBLK
HEADS_PER_GROUP = 4
GROUP_W = HEADS_PER_GROUP * HEAD_DIM
POOL_PAD = 8
POOL_HIST = 16
POOL_OFF = POOL_PAD + POOL_HIST
ONES_ROWS = 16
SUB_ROWS = 32
NORM_ROWS = 16
BOUND_MARGIN = 1.02
L_FLOOR = 2.0 ** -90
MOD_BLOCK_W = 1024
V7X_VMEM_LIMIT_BYTES = 56 * 1024 * 1024

F32 = jnp.float32
BF16 = jnp.bfloat16


def _mod_kernel(c_ref, w_ref, b_ref, o_ref):
    res = jnp.dot(c_ref[...], w_ref[...], preferred_element_type=F32) + b_ref[...]
    for b in range(o_ref.shape[0]):
        o_ref[b] = res[b:b + 1, :]


def _adaln_mod(c, w_ada, b_ada):
    bsz, d = c.shape
    n = w_ada.shape[1]
    return pl.pallas_call(
        _mod_kernel,
        grid=(n // MOD_BLOCK_W,),
        in_specs=[
            pl.BlockSpec((bsz, d), lambda i: (0, 0)),
            pl.BlockSpec((d, MOD_BLOCK_W), lambda i: (0, i)),
            pl.BlockSpec((1, MOD_BLOCK_W), lambda i: (0, i)),
        ],
        out_specs=pl.BlockSpec((bsz, 1, MOD_BLOCK_W), lambda i: (0, 0, i)),
        out_shape=jax.ShapeDtypeStruct((bsz, 1, n), F32),
        name="adaln_mod",
    )(c, w_ada, b_ada.reshape(1, n))


def _silu(z):
    return z * (1.0 / (1.0 + jnp.exp(-z)))


def _block_kernel(x_ref, xn_ref, mod_ref, modn_ref, g_ref, win_ref, wout_ref, gq_ref, gk_ref, rb_ref, wpool_ref, ps_ref, o_ref,
                  h_s, z_s, qbd_s, k_s, vt_s, u_s, t2_s, t4_s, t8_s, s_s, p_s, y_s, bias_ref, biasx_ref,
                  wvt_s, gain_s, wp_s,
                  *, d_model, attn_w, pool_w):
    t = pl.program_id(1)
    n_heads = attn_w // HEAD_DIM
    n_groups = attn_w // GROUP_W
    o_k, o_v, o_u, o_z = attn_w, 2 * attn_w, 3 * attn_w, 3 * attn_w + pool_w
    pool_group = pool_w // len(POOL_WINDOWS)
    pool_rows = TILE + POOL_OFF

    gate = mod_ref[:, 2 * d_model:3 * d_model]

    def norm_rows(src_ref, m_ref, r):
        a_row = g_ref[...] * (1.0 + m_ref[:, d_model:2 * d_model])
        xc = src_ref[r:r + NORM_ROWS, :]
        ms = jnp.mean(xc * xc, axis=-1, keepdims=True)
        hc = xc * lax.rsqrt(ms + EPS) * a_row + m_ref[:, 0:d_model]
        h_s[r:r + NORM_ROWS, :] = hc.astype(BF16)
        return hc[0:1, 0:LANES]

    @pl.when((pl.program_id(0) == 0) & (t == 0))
    def _():
        a = lax.broadcasted_iota(jnp.int32, (QBLK, LANES), 0)
        lane = lax.broadcasted_iota(jnp.int32, (QBLK, LANES), 1)
        band_lo = jnp.where(a >= CHUNK, CHUNK, 0)
        gq_max = jnp.max(jnp.abs(gq_ref[...]), axis=-1, keepdims=True)
        gk_max = jnp.max(jnp.abs(gk_ref[...]), axis=-1, keepdims=True)
        dot_bound = gq_max * gk_max * (HEAD_DIM ** 0.5 * LOG2E * BOUND_MARGIN)
        for h in range(n_heads):
            first = rb_ref[h:h + 1, 0:1] * LOG2E
            row_ext = jnp.concatenate([
                jnp.broadcast_to(first, (1, TILE - REL_CLIP)),
                rb_ref[h:h + 1, 0:REL_CLIP + QBLK] * LOG2E,
                jnp.broadcast_to(first, (1, BIAS_EXT - BANDW))], axis=1)
            row = jnp.broadcast_to(row_ext, (QBLK, BIAS_EXT))
            skew = pltpu.roll(row, 0, 1, stride=1, stride_axis=0)
            reachable = rb_ref[h:h + 1, 0:REL_CLIP + CHUNK]
            bound = dot_bound + jnp.max(reachable, axis=-1, keepdims=True) * LOG2E + 1.0
            for cb in range(BAND_BLOCKS):
                j = lane + cb * LANES
                visible = (j >= band_lo) & (j < band_lo + (N_LEFT_CHUNKS + 1) * CHUNK)
                blk = skew[:, cb * LANES:(cb + 1) * LANES]
                biasx_ref[h, cb] = jnp.where(visible, blk, NEG_INF).T
                bias_ref[h, cb] = jnp.where(visible, blk - bound, NEG_INF).T
            biasx_ref[h, BAND_BLOCKS] = jnp.full((QBLK, LANES), NEG_INF, F32)
            bias_ref[h, BAND_BLOCKS] = jnp.full((QBLK, LANES), NEG_INF, F32)
        for r in range(0, d_model, LANES):
            for c in range(0, attn_w, LANES):
                blk = win_ref[r:r + LANES, o_v + c:o_v + c + LANES].astype(F32)
                wvt_s[c:c + LANES, r:r + LANES] = blk.T.astype(BF16)
        gain_s[0] = jnp.broadcast_to(jnp.concatenate([gq_ref[...]] * n_heads, axis=1)
                                     * (HEAD_DIM ** -0.5 * LOG2E), (8, attn_w))
        gain_s[1] = jnp.broadcast_to(jnp.concatenate([gk_ref[...]] * n_heads, axis=1), (8, attn_w))
        k_s[...] = jnp.zeros(k_s.shape, BF16)
        vt_s[...] = jnp.zeros(vt_s.shape, BF16)
        wp_s[...] = jnp.zeros(wp_s.shape, BF16)
        for gi in range(len(POOL_WINDOWS)):
            lo = (gi % 2) * pool_group
            wp_s[gi // 2, lo:lo + pool_group, lo:lo + pool_group] = wpool_ref[gi].astype(BF16)
        u_s[...] = jnp.zeros(u_s.shape, F32)
        t2_s[0:POOL_PAD, :] = jnp.zeros((POOL_PAD, pool_w), F32)
        t4_s[0:POOL_PAD, :] = jnp.zeros((POOL_PAD, pool_w), F32)
        t8_s[0:POOL_PAD, :] = jnp.zeros((POOL_PAD, pool_w), F32)
        for r in range(0, TILE, NORM_ROWS):
            norm_rows(x_ref, mod_ref, r)

    u_s[POOL_PAD:POOL_OFF, :] = jnp.where(t == 0, 0.0, u_s[TILE + POOL_PAD:TILE + POOL_OFF, :])

    k_s[0:TILE, :] = k_s[TILE:2 * TILE, :]
    vt_s[:, 0:TILE] = vt_s[:, TILE:2 * TILE]

    lane_g = lax.broadcasted_iota(jnp.int32, (SUB_ROWS, GROUP_W), 1)
    lane_h = lax.broadcasted_iota(jnp.int32, (SUB_ROWS, LANES), 1)

    def in_proj(rows, c0, width):
        return jnp.dot(h_s[rows, :], win_ref[:, c0:c0 + width], preferred_element_type=F32)

    def head_rms(xf, gain_row):
        for sub in range(0, xf.shape[0], SUB_ROWS):
            x = xf[sub:sub + SUB_ROWS]
            sq = x * x
            ssq = []
            for c0 in range(0, GROUP_W, LANES):
                blk = sq[:, c0:c0 + LANES]
                first = jnp.sum(jnp.where(lane_h < HEAD_DIM, blk, 0.0), axis=-1, keepdims=True)
                both = jnp.sum(blk, axis=-1, keepdims=True)
                ssq.append(jnp.where(lane_h < HEAD_DIM, first, both - first))
            ssq = jnp.concatenate(ssq, axis=1)
            yield sub, x * lax.rsqrt(ssq * (1.0 / HEAD_DIM) + EPS) * gain_row

    tile_rows = slice(0, TILE)
    qf = in_proj(tile_rows, 0, attn_w)
    for rb in range(0, TILE, QBLK):
        for g in range(n_groups):
            cols = slice(g * GROUP_W, (g + 1) * GROUP_W)
            gain = gain_s[0, 0:1, cols]
            for sub, qn in head_rms(qf[rb:rb + QBLK, cols], gain):
                for h in range(HEADS_PER_GROUP):
                    keep = (lane_g >= h * HEAD_DIM) & (lane_g < (h + 1) * HEAD_DIM)
                    qbd_s[rb // QBLK, g, h * QBLK + sub:h * QBLK + sub + SUB_ROWS, :] = (
                        jnp.where(keep, qn, 0.0).astype(BF16))
    kf = in_proj(tile_rows, o_k, attn_w)
    for rb in range(0, TILE, QBLK):
        for g in range(n_groups):
            cols = slice(g * GROUP_W, (g + 1) * GROUP_W)
            for sub, kn in head_rms(kf[rb:rb + QBLK, cols], gain_s[1, 0:1, cols]):
                row = TILE + rb + sub
                k_s[row:row + SUB_ROWS, cols] = kn.astype(BF16)
    vt_s[:, TILE:2 * TILE] = lax.dot_general(
        wvt_s[...], h_s[...], (((1,), (1,)), ((), ())), preferred_element_type=F32).astype(BF16)
    u_s[POOL_OFF:POOL_OFF + TILE, :] = in_proj(tile_rows, o_u, pool_w)
    for c0 in range(0, attn_w + pool_w, 512):
        z_s[:, c0:c0 + 512] = _silu(in_proj(tile_rows, o_z + c0, 512)).astype(BF16)

    def pool():
        step = 88
        levels = ((u_s, t2_s, 1), (t2_s, t4_s, 2), (t4_s, t8_s, 4), (t8_s, None, 8))
        for gi in range(len(POOL_WINDOWS)):
            assert POOL_WINDOWS[gi] == 2 ** (gi + 1)
        for li, (src, dst, sh) in enumerate(levels[:-1]):
            lanes = slice((li + 1) * pool_group, pool_w)
            for r in range(POOL_PAD, pool_rows, step):
                dst[r:r + step, lanes] = src[r:r + step, lanes] + src[r - sh:r - sh + step, lanes]
        tok = t * TILE + lax.broadcasted_iota(jnp.int32, (QBLK, pool_group), 0)
        for r in range(0, TILE, QBLK):
            for pair in range(len(POOL_WINDOWS) // 2):
                mixed = []
                for gi in (2 * pair, 2 * pair + 1):
                    lanes = slice(gi * pool_group, (gi + 1) * pool_group)
                    src = levels[gi][0]
                    sh = levels[gi][2]
                    rows = slice(POOL_OFF + r, POOL_OFF + r + QBLK)
                    wsum = src[rows, lanes] + src[POOL_OFF + r - sh:POOL_OFF + r - sh + QBLK, lanes]
                    cnt = jnp.minimum(tok + (r + 1), POOL_WINDOWS[gi]).astype(F32)
                    mixed.append((wsum / cnt - u_s[rows, lanes]).astype(BF16))
                pm = jnp.concatenate(mixed, axis=1)
                c0 = pair * 2 * pool_group
                po = jnp.dot(pm, wp_s[pair], preferred_element_type=F32) * ps_ref[:, c0:c0 + 2 * pool_group]
                gz = z_s[r:r + QBLK, attn_w + c0:attn_w + c0 + 2 * pool_group].astype(F32)
                y_s[r:r + QBLK, attn_w + c0:attn_w + c0 + 2 * pool_group] = (po * gz).astype(BF16)

    row_q = lax.broadcasted_iota(jnp.int32, (QBLK, LANES), 0)
    n_qblk = TILE // QBLK

    def band_blocks(jb):
        n_before = TILE // LANES - jb
        return [jnp.where(t == 0, BAND_BLOCKS, cb) if cb < n_before else cb for cb in range(BAND_BLOCKS)]

    def score_dot(jb, g):
        qrow = jb * QBLK
        return lax.dot_general(k_s[qrow:qrow + BANDW, g * GROUP_W:(g + 1) * GROUP_W], qbd_s[jb, g],
                               (((1,), (1,)), ((), ())), preferred_element_type=F32)

    def probs(jb):
        buf = jb % 2
        blk = band_blocks(jb)
        for g in range(n_groups):
            st = score_dot(jb, g)
            for h in range(HEADS_PER_GROUP):
                hh = g * HEADS_PER_GROUP + h
                lanes = slice(h * QBLK, (h + 1) * QBLK)
                for cb in range(BAND_BLOCKS):
                    for sub in range(0, LANES, SUB_ROWS):
                        rows = slice(cb * LANES + sub, cb * LANES + sub + SUB_ROWS)
                        p = jnp.exp2(st[rows, lanes] + bias_ref[hh, blk[cb], sub:sub + SUB_ROWS, :])
                        p_s[buf, hh // 2, rows, (hh % 2) * QBLK:(hh % 2 + 1) * QBLK] = p.astype(BF16)

    def probs_exact(jb):
        buf = jb % 2
        blk = band_blocks(jb)
        for g in range(n_groups):
            st = score_dot(jb, g)
            for h in range(HEADS_PER_GROUP):
                hh = g * HEADS_PER_GROUP + h
                lanes = slice(h * QBLK, (h + 1) * QBLK)
                macc = None
                for cb in range(BAND_BLOCKS):
                    for sub in range(0, LANES, SUB_ROWS):
                        rows = slice(cb * LANES + sub, cb * LANES + sub + SUB_ROWS)
                        sb = st[rows, lanes] + biasx_ref[hh, blk[cb], sub:sub + SUB_ROWS, :]
                        s_s[hh, rows, :] = sb
                        for r8 in range(0, SUB_ROWS, 8):
                            macc = sb[r8:r8 + 8, :] if macc is None else jnp.maximum(macc, sb[r8:r8 + 8, :])
                m = jnp.max(macc, axis=0, keepdims=True)
                for r in range(0, BANDW, SUB_ROWS):
                    rows = slice(r, r + SUB_ROWS)
                    p = jnp.exp2(s_s[hh, rows, :] - m)
                    p_s[buf, hh // 2, rows, (hh % 2) * QBLK:(hh % 2 + 1) * QBLK] = p.astype(BF16)

    ones_rows = jnp.ones((ONES_ROWS, BANDW), BF16)

    def attend(jb, anchor=None):
        buf = jb % 2
        qrow = jb * QBLK
        lmin = None
        for pair in range(n_heads // 2):
            ha = 2 * pair
            vt1 = jnp.concatenate(
                [vt_s[ha * HEAD_DIM:(ha + 2) * HEAD_DIM, qrow:qrow + BANDW], ones_rows], axis=0)
            ot = jnp.dot(vt1, p_s[buf, pair], preferred_element_type=F32)
            l = ot[2 * HEAD_DIM:2 * HEAD_DIM + 1, :]
            lmin = l if lmin is None else jnp.minimum(lmin, l)
            linv = 1.0 / l
            if anchor is not None:
                linv = jnp.concatenate([jnp.where(never, anchor(pair), linv[:, 0:LANES]), linv[:, LANES:]], axis=1)
            oa = ot[0:2 * HEAD_DIM, 0:QBLK] * linv[:, 0:QBLK]
            ob = ot[0:2 * HEAD_DIM, QBLK:2 * QBLK] * linv[:, QBLK:2 * QBLK]
            a = jnp.where(row_q < HEAD_DIM, oa, ob).T
            c0 = ha * HEAD_DIM
            gz = z_s[qrow:qrow + QBLK, c0:c0 + LANES].astype(F32)
            y_s[qrow:qrow + QBLK, c0:c0 + LANES] = (a * gz).astype(BF16)
        return jnp.min(lmin)

    never = t < 0
    norm_steps = list(range(0, TILE, NORM_ROWS))
    per_pair = len(norm_steps) // (n_qblk * (n_heads // 2))

    def next_norm(jb):
        def anchor(pair):
            i0 = (jb * (n_heads // 2) + pair) * per_pair
            return sum(norm_rows(xn_ref, modn_ref, rn) for rn in norm_steps[i0:i0 + per_pair])
        return anchor

    probs(0)
    pool()
    underflowed = []
    for jb in range(n_qblk):
        if jb + 1 < n_qblk:
            probs(jb + 1)
        underflowed.append(jnp.logical_not(attend(jb, next_norm(jb)) >= L_FLOOR))

    o = jnp.dot(y_s[...], wout_ref[...], preferred_element_type=F32)
    for c0 in range(0, d_model, GROUP_W):
        cols = slice(c0, c0 + GROUP_W)
        for r in range(0, TILE, SUB_ROWS):
            o_ref[r:r + SUB_ROWS, cols] = x_ref[r:r + SUB_ROWS, cols] + gate[:, cols] * o[r:r + SUB_ROWS, cols]

    @pl.when(functools.reduce(jnp.logical_or, underflowed))
    def _():
        for jb in range(n_qblk):
            @pl.when(underflowed[jb])
            def _(jb=jb):
                probs_exact(jb)
                attend(jb)
                rows = slice(jb * QBLK, (jb + 1) * QBLK)
                o_jb = jnp.dot(y_s[rows, :], wout_ref[...], preferred_element_type=F32)
                o_ref[rows, :] = x_ref[rows, :] + gate * o_jb


def _layer(x, mod, norm_g, w_in, q_norm_g, k_norm_g, rel_bias, w_pool, pool_scale, w_out):
    bsz, seq, d_model = x.shape
    n_heads = rel_bias.shape[0]
    attn_w = n_heads * HEAD_DIM
    pool_w = w_pool.shape[0] * w_pool.shape[1]
    pool_group = w_pool.shape[1]
    in_w = w_in.shape[1]
    assert seq % TILE == 0 and attn_w % GROUP_W == 0 and TILE == N_LEFT_CHUNKS * CHUNK
    assert in_w == 3 * attn_w + pool_w + attn_w + pool_w and 2 * pool_group == GROUP_W
    assert attn_w == 512 and pool_w == 512
    n_groups = attn_w // GROUP_W

    n_pairs = w_pool.shape[0] // 2
    assert QBLK <= REL_CLIP <= TILE and rel_bias.shape[1] == 2 * REL_CLIP + 1

    tiles_per_seq = seq // TILE

    def next_tile(b, t):
        n = jnp.minimum(b * tiles_per_seq + t + 1, bsz * tiles_per_seq - 1)
        return n // tiles_per_seq, n % tiles_per_seq

    const2 = lambda b, t: (0, 0)
    const3 = lambda b, t: (0, 0, 0)
    kernel = functools.partial(_block_kernel, d_model=d_model, attn_w=attn_w, pool_w=pool_w)
    return pl.pallas_call(
        kernel,
        grid=(bsz, seq // TILE),
        in_specs=[
            pl.BlockSpec((None, TILE, d_model), lambda b, t: (b, t, 0)),
            pl.BlockSpec((None, TILE, d_model), lambda b, t: (*next_tile(b, t), 0)),
            pl.BlockSpec((None, 1, 3 * d_model), lambda b, t: (b, 0, 0)),
            pl.BlockSpec((None, 1, 3 * d_model), lambda b, t: (next_tile(b, t)[0], 0, 0)),
            pl.BlockSpec((1, d_model), const2),
            pl.BlockSpec((d_model, in_w), const2),
            pl.BlockSpec((attn_w + pool_w, d_model), const2),
            pl.BlockSpec((1, HEAD_DIM), const2),
            pl.BlockSpec((1, HEAD_DIM), const2),
            pl.BlockSpec((n_heads, 2 * REL_CLIP + 1), const2),
            pl.BlockSpec((2 * n_pairs, pool_group, pool_group), const3),
            pl.BlockSpec((1, pool_w), const2),
        ],
        out_specs=pl.BlockSpec((None, TILE, d_model), lambda b, t: (b, t, 0)),
        out_shape=jax.ShapeDtypeStruct(x.shape, x.dtype),
        scratch_shapes=[
            pltpu.VMEM((TILE, d_model), BF16),
            pltpu.VMEM((TILE, attn_w + pool_w), BF16),
            pltpu.VMEM((TILE // QBLK, n_groups, HEADS_PER_GROUP * QBLK, GROUP_W), BF16),
            pltpu.VMEM((2 * TILE, attn_w), BF16),
            pltpu.VMEM((attn_w, 2 * TILE), BF16),
            pltpu.VMEM((TILE + POOL_OFF, pool_w), F32),
            pltpu.VMEM((TILE + POOL_OFF, pool_w), F32),
            pltpu.VMEM((TILE + POOL_OFF, pool_w), F32),
            pltpu.VMEM((TILE + POOL_OFF, pool_w), F32),
            pltpu.VMEM((n_heads, BANDW, QBLK), F32),
            pltpu.VMEM((2, n_heads // 2, BANDW, 2 * QBLK), BF16),
            pltpu.VMEM((TILE, attn_w + pool_w), BF16),
            pltpu.VMEM((n_heads, BAND_BLOCKS + 1, QBLK, LANES), F32),
            pltpu.VMEM((n_heads, BAND_BLOCKS + 1, QBLK, LANES), F32),
            pltpu.VMEM((attn_w, d_model), BF16),
            pltpu.VMEM((2, 8, attn_w), F32),
            pltpu.VMEM((n_pairs, GROUP_W, GROUP_W), BF16),
        ],
        compiler_params=pltpu.CompilerParams(
            dimension_semantics=("arbitrary", "arbitrary"),
            vmem_limit_bytes=V7X_VMEM_LIMIT_BYTES,
        ),
        name="hybrid_block",
    )(x, x, mod, mod, norm_g.reshape(1, d_model), w_in.astype(BF16), w_out.astype(BF16),
      q_norm_g.reshape(1, HEAD_DIM), k_norm_g.reshape(1, HEAD_DIM), rel_bias, w_pool,
      pool_scale.reshape(1, pool_w))


def kernel(x, c, norm_g, w_ada, b_ada, w_in, q_norm_g, k_norm_g, rel_bias, w_pool, pool_scale, w_out):
    depth = w_in.shape[0]
    for l in range(depth):
        mod = _adaln_mod(c, w_ada[l], b_ada[l])
        x = _layer(x, mod, norm_g[l], w_in[l], q_norm_g[l], k_norm_g[l], rel_bias[l],
                   w_pool[l], pool_scale[l], w_out[l])
    return x
```

```python
import functools
import math

import jax
import jax.numpy as jnp
from jax import lax
from jax.experimental import pallas as pl
from jax.experimental.pallas import tpu as pltpu

CHUNK = 64
N_LEFT_CHUNKS = 8
HEAD_DIM = 64
REL_CLIP = 256
POOL_WINDOWS = (2, 4, 8, 16)
EPS = 1e-6
NEG_INF = -1e30
LOG2E = math.log2(math.e)

LANES = 128
TILE = 512
QBLK = 2 * CHUNK
BANDW = TILE + QBLK
BAND_BLOCKS = BANDW // LANES
BIAS_EXT = BANDW + QBLK
HEADS_PER_GROUP = 4
GROUP_W = HEADS_PER_GROUP * HEAD_DIM
POOL_PAD = 8
POOL_HIST = 16
POOL_OFF = POOL_PAD + POOL_HIST
ONES_ROWS = 16
SUB_ROWS = 32
NORM_ROWS = 16
BOUND_MARGIN = 1.02
L_FLOOR = 2.0 ** -90
STAGE_ROWS = 128
MOD_BLOCK_W = 1024
V7X_VMEM_LIMIT_BYTES = 56 * 1024 * 1024

F32 = jnp.float32
BF16 = jnp.bfloat16


def _mod_kernel(c_ref, w_ref, b_ref, o_ref):
    res = jnp.dot(c_ref[...], w_ref[...], preferred_element_type=F32) + b_ref[...]
    for b in range(o_ref.shape[0]):
        o_ref[b] = res[b:b + 1, :]


def _adaln_mod(c, w_ada, b_ada):
    bsz, d = c.shape
    n = w_ada.shape[1]
    return pl.pallas_call(
        _mod_kernel,
        grid=(n // MOD_BLOCK_W,),
        in_specs=[
            pl.BlockSpec((bsz, d), lambda i: (0, 0)),
            pl.BlockSpec((d, MOD_BLOCK_W), lambda i: (0, i)),
            pl.BlockSpec((1, MOD_BLOCK_W), lambda i: (0, i)),
        ],
        out_specs=pl.BlockSpec((bsz, 1, MOD_BLOCK_W), lambda i: (0, 0, i)),
        out_shape=jax.ShapeDtypeStruct((bsz, 1, n), F32),
        name="adaln_mod",
    )(c, w_ada, b_ada.reshape(1, n))


def _silu(z):
    return z * (1.0 / (1.0 + jnp.exp(-z)))


def _block_kernel(x_ref, xn_ref, mod_ref, modn_ref, g_ref, win_ref, wout_ref, gq_ref, gk_ref, rb_ref, wpool_ref, ps_ref, o_ref,
                  h_s, z_s, qbd_s, k_s, vt_s, u_s, t2_s, t4_s, t8_s, s_s, p_s, y_s, bias_ref, biasx_ref,
                  wvt_s, gain_s, wp_s, win_s, wout_s, stage_s, stage_sem,
                  *, d_model, attn_w, pool_w):
    t = pl.program_id(1)
    n_heads = attn_w // HEAD_DIM
    n_groups = attn_w // GROUP_W
    o_k, o_v, o_u, o_z = attn_w, 2 * attn_w, 3 * attn_w, 3 * attn_w + pool_w
    pool_group = pool_w // len(POOL_WINDOWS)
    pool_rows = TILE + POOL_OFF

    gate = mod_ref[:, 2 * d_model:3 * d_model]

    def norm_rows(src_ref, m_ref, r):
        a_row = g_ref[...] * (1.0 + m_ref[:, d_model:2 * d_model])
        xc = src_ref[r:r + NORM_ROWS, :]
        ms = jnp.mean(xc * xc, axis=-1, keepdims=True)
        hc = xc * lax.rsqrt(ms + EPS) * a_row + m_ref[:, 0:d_model]
        h_s[r:r + NORM_ROWS, :] = hc.astype(BF16)
        return hc[0:1, 0:LANES]

    @pl.when((pl.program_id(0) == 0) & (t == 0))
    def _():
        chunks = ([(win_ref, win_s, r) for r in range(0, win_s.shape[0], STAGE_ROWS)]
                  + [(wout_ref, wout_s, r) for r in range(0, wout_s.shape[0], STAGE_ROWS)])

        def chunk_copy(i):
            src, dst, r = chunks[i]
            return pltpu.make_async_copy(src.at[pl.ds(r, STAGE_ROWS), :],
                                         stage_s.at[i % 2, :, pl.ds(0, dst.shape[1])], stage_sem.at[i % 2])

        chunk_copy(0).start()
        for i, (_, dst, r) in enumerate(chunks):
            if i + 1 < len(chunks):
                chunk_copy(i + 1).start()
            chunk_copy(i).wait()
            dst[r:r + STAGE_ROWS, :] = stage_s[i % 2, :, 0:dst.shape[1]].astype(BF16)

        a = lax.broadcasted_iota(jnp.int32, (QBLK, LANES), 0)
        lane = lax.broadcasted_iota(jnp.int32, (QBLK, LANES), 1)
        band_lo = jnp.where(a >= CHUNK, CHUNK, 0)
        gq_max = jnp.max(jnp.abs(gq_ref[...]), axis=-1, keepdims=True)
        gk_max = jnp.max(jnp.abs(gk_ref[...]), axis=-1, keepdims=True)
        dot_bound = gq_max * gk_max * (HEAD_DIM ** 0.5 * LOG2E * BOUND_MARGIN)
        for h in range(n_heads):
            first = rb_ref[h:h + 1, 0:1] * LOG2E
            row_ext = jnp.concatenate([
                jnp.broadcast_to(first, (1, TILE - REL_CLIP)),
                rb_ref[h:h + 1, 0:REL_CLIP + QBLK] * LOG2E,
                jnp.broadcast_to(first, (1, BIAS_EXT - BANDW))], axis=1)
            row = jnp.broadcast_to(row_ext, (QBLK, BIAS_EXT))
            skew = pltpu.roll(row, 0, 1, stride=1, stride_axis=0)
            reachable = rb_ref[h:h + 1, 0:REL_CLIP + CHUNK]
            bound = dot_bound + jnp.max(reachable, axis=-1, keepdims=True) * LOG2E + 1.0
            for cb in range(BAND_BLOCKS):
                j = lane + cb * LANES
                visible = (j >= band_lo) & (j < band_lo + (N_LEFT_CHUNKS + 1) * CHUNK)
                blk = skew[:, cb * LANES:(cb + 1) * LANES]
                biasx_ref[h, cb] = jnp.where(visible, blk, NEG_INF).T
                bias_ref[h, cb] = jnp.where(visible, blk - bound, NEG_INF).T
            biasx_ref[h, BAND_BLOCKS] = jnp.full((QBLK, LANES), NEG_INF, F32)
            bias_ref[h, BAND_BLOCKS] = jnp.full((QBLK, LANES), NEG_INF, F32)
        for r in range(0, d_model, LANES):
            for c in range(0, attn_w, LANES):
                blk = win_s[r:r + LANES, o_v + c:o_v + c + LANES].astype(F32)
                wvt_s[c:c + LANES, r:r + LANES] = blk.T.astype(BF16)
        gain_s[0] = jnp.broadcast_to(jnp.concatenate([gq_ref[...]] * n_heads, axis=1)
                                     * (HEAD_DIM ** -0.5 * LOG2E), (8, attn_w))
        gain_s[1] = jnp.broadcast_to(jnp.concatenate([gk_ref[...]] * n_heads, axis=1), (8, attn_w))
        k_s[...] = jnp.zeros(k_s.shape, BF16)
        vt_s[...] = jnp.zeros(vt_s.shape, BF16)
        wp_s[...] = jnp.zeros(wp_s.shape, BF16)
        for gi in range(len(POOL_WINDOWS)):
            lo = (gi % 2) * pool_group
            wp_s[gi // 2, lo:lo + pool_group, lo:lo + pool_group] = wpool_ref[gi].astype(BF16)
        u_s[...] = jnp.zeros(u_s.shape, F32)
        t2_s[0:POOL_PAD, :] = jnp.zeros((POOL_PAD, pool_w), F32)
        t4_s[0:POOL_PAD, :] = jnp.zeros((POOL_PAD, pool_w), F32)
        t8_s[0:POOL_PAD, :] = jnp.zeros((POOL_PAD, pool_w), F32)
        for r in range(0, TILE, NORM_ROWS):
            norm_rows(x_ref, mod_ref, r)

    u_s[POOL_PAD:POOL_OFF, :] = jnp.where(t == 0, 0.0, u_s[TILE + POOL_PAD:TILE + POOL_OFF, :])

    k_s[0:TILE, :] = k_s[TILE:2 * TILE, :]
    vt_s[:, 0:TILE] = vt_s[:, TILE:2 * TILE]

    lane_g = lax.broadcasted_iota(jnp.int32, (SUB_ROWS, GROUP_W), 1)
    lane_h = lax.broadcasted_iota(jnp.int32, (SUB_ROWS, LANES), 1)

    def in_proj(rows, c0, width):
        return jnp.dot(h_s[rows, :], win_s[:, c0:c0 + width], preferred_element_type=F32)

    def head_rms(xf, gain_row):
        for sub in range(0, xf.shape[0], SUB_ROWS):
            x = xf[sub:sub + SUB_ROWS]
            sq = x * x
            ssq = []
            for c0 in range(0, GROUP_W, LANES):
                blk = sq[:, c0:c0 + LANES]
                first = jnp.sum(jnp.where(lane_h < HEAD_DIM, blk, 0.0), axis=-1, keepdims=True)
                both = jnp.sum(blk, axis=-1, keepdims=True)
                ssq.append(jnp.where(lane_h < HEAD_DIM, first, both - first))
            ssq = jnp.concatenate(ssq, axis=1)
            yield sub, x * lax.rsqrt(ssq * (1.0 / HEAD_DIM) + EPS) * gain_row

    tile_rows = slice(0, TILE)
    qf = in_proj(tile_rows, 0, attn_w)
    for rb in range(0, TILE, QBLK):
        for g in range(n_groups):
            cols = slice(g * GROUP_W, (g + 1) * GROUP_W)
            gain = gain_s[0, 0:1, cols]
            for sub, qn in head_rms(qf[rb:rb + QBLK, cols], gain):
                for h in range(HEADS_PER_GROUP):
                    keep = (lane_g >= h * HEAD_DIM) & (lane_g < (h + 1) * HEAD_DIM)
                    qbd_s[rb // QBLK, g, h * QBLK + sub:h * QBLK + sub + SUB_ROWS, :] = (
                        jnp.where(keep, qn, 0.0).astype(BF16))
    kf = in_proj(tile_rows, o_k, attn_w)
    for rb in range(0, TILE, QBLK):
        for g in range(n_groups):
            cols = slice(g * GROUP_W, (g + 1) * GROUP_W)
            for sub, kn in head_rms(kf[rb:rb + QBLK, cols], gain_s[1, 0:1, cols]):
                row = TILE + rb + sub
                k_s[row:row + SUB_ROWS, cols] = kn.astype(BF16)
    vt_s[:, TILE:2 * TILE] = lax.dot_general(
        wvt_s[...], h_s[...], (((1,), (1,)), ((), ())), preferred_element_type=F32).astype(BF16)
    u_s[POOL_OFF:POOL_OFF + TILE, :] = in_proj(tile_rows, o_u, pool_w)
    for c0 in range(0, attn_w + pool_w, 512):
        z_s[:, c0:c0 + 512] = _silu(in_proj(tile_rows, o_z + c0, 512)).astype(BF16)

    def pool():
        step = 88
        levels = ((u_s, t2_s, 1), (t2_s, t4_s, 2), (t4_s, t8_s, 4), (t8_s, None, 8))
        for gi in range(len(POOL_WINDOWS)):
            assert POOL_WINDOWS[gi] == 2 ** (gi + 1)
        for li, (src, dst, sh) in enumerate(levels[:-1]):
            lanes = slice((li + 1) * pool_group, pool_w)
            for r in range(POOL_PAD, pool_rows, step):
                dst[r:r + step, lanes] = src[r:r + step, lanes] + src[r - sh:r - sh + step, lanes]
        tok = t * TILE + lax.broadcasted_iota(jnp.int32, (QBLK, pool_group), 0)
        for r in range(0, TILE, QBLK):
            for pair in range(len(POOL_WINDOWS) // 2):
                mixed = []
                for gi in (2 * pair, 2 * pair + 1):
                    lanes = slice(gi * pool_group, (gi + 1) * pool_group)
                    src = levels[gi][0]
                    sh = levels[gi][2]
                    rows = slice(POOL_OFF + r, POOL_OFF + r + QBLK)
                    wsum = src[rows, lanes] + src[POOL_OFF + r - sh:POOL_OFF + r - sh + QBLK, lanes]
                    cnt = jnp.minimum(tok + (r + 1), POOL_WINDOWS[gi]).astype(F32)
                    mixed.append((wsum / cnt - u_s[rows, lanes]).astype(BF16))
                pm = jnp.concatenate(mixed, axis=1)
                c0 = pair * 2 * pool_group
                po = jnp.dot(pm, wp_s[pair], preferred_element_type=F32) * ps_ref[:, c0:c0 + 2 * pool_group]
                gz = z_s[r:r + QBLK, attn_w + c0:attn_w + c0 + 2 * pool_group].astype(F32)
                y_s[r:r + QBLK, attn_w + c0:attn_w + c0 + 2 * pool_group] = (po * gz).astype(BF16)

    row_q = lax.broadcasted_iota(jnp.int32, (QBLK, LANES), 0)
    n_qblk = TILE // QBLK

    def band_blocks(jb):
        n_before = TILE // LANES - jb
        return [jnp.where(t == 0, BAND_BLOCKS, cb) if cb < n_before else cb for cb in range(BAND_BLOCKS)]

    def score_dot(jb, g):
        qrow = jb * QBLK
        return lax.dot_general(k_s[qrow:qrow + BANDW, g * GROUP_W:(g + 1) * GROUP_W], qbd_s[jb, g],
                               (((1,), (1,)), ((), ())), preferred_element_type=F32)

    def probs(jb):
        buf = jb % 2
        blk = band_blocks(jb)
        for g in range(n_groups):
            st = score_dot(jb, g)
            for h in range(HEADS_PER_GROUP):
                hh = g * HEADS_PER_GROUP + h
                lanes = slice(h * QBLK, (h + 1) * QBLK)
                for cb in range(BAND_BLOCKS):
                    for sub in range(0, LANES, SUB_ROWS):
                        rows = slice(cb * LANES + sub, cb * LANES + sub + SUB_ROWS)
                        p = jnp.exp2(st[rows, lanes] + bias_ref[hh, blk[cb], sub:sub + SUB_ROWS, :])
                        p_s[buf, hh // 2, rows, (hh % 2) * QBLK:(hh % 2 + 1) * QBLK] = p.astype(BF16)

    def probs_exact(jb):
        buf = jb % 2
        blk = band_blocks(jb)
        for g in range(n_groups):
            st = score_dot(jb, g)
            for h in range(HEADS_PER_GROUP):
                hh = g * HEADS_PER_GROUP + h
                lanes = slice(h * QBLK, (h + 1) * QBLK)
                macc = None
                for cb in range(BAND_BLOCKS):
                    for sub in range(0, LANES, SUB_ROWS):
                        rows = slice(cb * LANES + sub, cb * LANES + sub + SUB_ROWS)
                        sb = st[rows, lanes] + biasx_ref[hh, blk[cb], sub:sub + SUB_ROWS, :]
                        s_s[hh, rows, :] = sb
                        for r8 in range(0, SUB_ROWS, 8):
                            macc = sb[r8:r8 + 8, :] if macc is None else jnp.maximum(macc, sb[r8:r8 + 8, :])
                m = jnp.max(macc, axis=0, keepdims=True)
                for r in range(0, BANDW, SUB_ROWS):
                    rows = slice(r, r + SUB_ROWS)
                    p = jnp.exp2(s_s[hh, rows, :] - m)
                    p_s[buf, hh // 2, rows, (hh % 2) * QBLK:(hh % 2 + 1) * QBLK] = p.astype(BF16)

    ones_rows = jnp.ones((ONES_ROWS, BANDW), BF16)

    def attend(jb, anchor=None):
        buf = jb % 2
        qrow = jb * QBLK
        lmin = None
        for pair in range(n_heads // 2):
            ha = 2 * pair
            vt1 = jnp.concatenate(
                [vt_s[ha * HEAD_DIM:(ha + 2) * HEAD_DIM, qrow:qrow + BANDW], ones_rows], axis=0)
            ot = jnp.dot(vt1, p_s[buf, pair], preferred_element_type=F32)
            l = ot[2 * HEAD_DIM:2 * HEAD_DIM + 1, :]
            lmin = l if lmin is None else jnp.minimum(lmin, l)
            linv = 1.0 / l
            if anchor is not None:
                linv = jnp.concatenate([jnp.where(never, anchor(pair), linv[:, 0:LANES]), linv[:, LANES:]], axis=1)
            oa = ot[0:2 * HEAD_DIM, 0:QBLK] * linv[:, 0:QBLK]
            ob = ot[0:2 * HEAD_DIM, QBLK:2 * QBLK] * linv[:, QBLK:2 * QBLK]
            a = jnp.where(row_q < HEAD_DIM, oa, ob).T
            c0 = ha * HEAD_DIM
            gz = z_s[qrow:qrow + QBLK, c0:c0 + LANES].astype(F32)
            y_s[qrow:qrow + QBLK, c0:c0 + LANES] = (a * gz).astype(BF16)
        return jnp.min(lmin)

    never = t < 0
    norm_steps = list(range(0, TILE, NORM_ROWS))
    per_pair = len(norm_steps) // (n_qblk * (n_heads // 2))

    def next_norm(jb):
        def anchor(pair):
            i0 = (jb * (n_heads // 2) + pair) * per_pair
            return sum(norm_rows(xn_ref, modn_ref, rn) for rn in norm_steps[i0:i0 + per_pair])
        return anchor

    probs(0)
    pool()
    underflowed = []
    for jb in range(n_qblk):
        if jb + 1 < n_qblk:
            probs(jb + 1)
        underflowed.append(jnp.logical_not(attend(jb, next_norm(jb)) >= L_FLOOR))

    o = jnp.dot(y_s[...], wout_s[...], preferred_element_type=F32)
    for c0 in range(0, d_model, GROUP_W):
        cols = slice(c0, c0 + GROUP_W)
        for r in range(0, TILE, SUB_ROWS):
            o_ref[r:r + SUB_ROWS, cols] = x_ref[r:r + SUB_ROWS, cols] + gate[:, cols] * o[r:r + SUB_ROWS, cols]

    @pl.when(functools.reduce(jnp.logical_or, underflowed))
    def _():
        for jb in range(n_qblk):
            @pl.when(underflowed[jb])
            def _(jb=jb):
                probs_exact(jb)
                attend(jb)
                rows = slice(jb * QBLK, (jb + 1) * QBLK)
                o_jb = jnp.dot(y_s[rows, :], wout_s[...], preferred_element_type=F32)
                o_ref[rows, :] = x_ref[rows, :] + gate * o_jb


def _layer(x, mod, norm_g, w_in, q_norm_g, k_norm_g, rel_bias, w_pool, pool_scale, w_out):
    bsz, seq, d_model = x.shape
    n_heads = rel_bias.shape[0]
    attn_w = n_heads * HEAD_DIM
    pool_w = w_pool.shape[0] * w_pool.shape[1]
    pool_group = w_pool.shape[1]
    in_w = w_in.shape[1]
    assert seq % TILE == 0 and attn_w % GROUP_W == 0 and TILE == N_LEFT_CHUNKS * CHUNK
    assert in_w == 3 * attn_w + pool_w + attn_w + pool_w and 2 * pool_group == GROUP_W
    assert attn_w == 512 and pool_w == 512
    n_groups = attn_w // GROUP_W

    n_pairs = w_pool.shape[0] // 2
    assert QBLK <= REL_CLIP <= TILE and rel_bias.shape[1] == 2 * REL_CLIP + 1

    tiles_per_seq = seq // TILE

    def next_tile(b, t):
        n = jnp.minimum(b * tiles_per_seq + t + 1, bsz * tiles_per_seq - 1)
        return n // tiles_per_seq, n % tiles_per_seq

    const2 = lambda b, t: (0, 0)
    const3 = lambda b, t: (0, 0, 0)
    kernel = functools.partial(_block_kernel, d_model=d_model, attn_w=attn_w, pool_w=pool_w)
    return pl.pallas_call(
        kernel,
        grid=(bsz, seq // TILE),
        in_specs=[
            pl.BlockSpec((None, TILE, d_model), lambda b, t: (b, t, 0)),
            pl.BlockSpec((None, TILE, d_model), lambda b, t: (*next_tile(b, t), 0)),
            pl.BlockSpec((None, 1, 3 * d_model), lambda b, t: (b, 0, 0)),
            pl.BlockSpec((None, 1, 3 * d_model), lambda b, t: (next_tile(b, t)[0], 0, 0)),
            pl.BlockSpec((1, d_model), const2),
            pl.BlockSpec(memory_space=pl.ANY),
            pl.BlockSpec(memory_space=pl.ANY),
            pl.BlockSpec((1, HEAD_DIM), const2),
            pl.BlockSpec((1, HEAD_DIM), const2),
            pl.BlockSpec((n_heads, 2 * REL_CLIP + 1), const2),
            pl.BlockSpec((2 * n_pairs, pool_group, pool_group), const3),
            pl.BlockSpec((1, pool_w), const2),
        ],
        out_specs=pl.BlockSpec((None, TILE, d_model), lambda b, t: (b, t, 0)),
        out_shape=jax.ShapeDtypeStruct(x.shape, x.dtype),
        scratch_shapes=[
            pltpu.VMEM((TILE, d_model), BF16),
            pltpu.VMEM((TILE, attn_w + pool_w), BF16),
            pltpu.VMEM((TILE // QBLK, n_groups, HEADS_PER_GROUP * QBLK, GROUP_W), BF16),
            pltpu.VMEM((2 * TILE, attn_w), BF16),
            pltpu.VMEM((attn_w, 2 * TILE), BF16),
            pltpu.VMEM((TILE + POOL_OFF, pool_w), F32),
            pltpu.VMEM((TILE + POOL_OFF, pool_w), F32),
            pltpu.VMEM((TILE + POOL_OFF, pool_w), F32),
            pltpu.VMEM((TILE + POOL_OFF, pool_w), F32),
            pltpu.VMEM((n_heads, BANDW, QBLK), F32),
            pltpu.VMEM((2, n_heads // 2, BANDW, 2 * QBLK), BF16),
            pltpu.VMEM((TILE, attn_w + pool_w), BF16),
            pltpu.VMEM((n_heads, BAND_BLOCKS + 1, QBLK, LANES), F32),
            pltpu.VMEM((n_heads, BAND_BLOCKS + 1, QBLK, LANES), F32),
            pltpu.VMEM((attn_w, d_model), BF16),
            pltpu.VMEM((2, 8, attn_w), F32),
            pltpu.VMEM((n_pairs, GROUP_W, GROUP_W), BF16),
            pltpu.VMEM((d_model, in_w), BF16),
            pltpu.VMEM((attn_w + pool_w, d_model), BF16),
            pltpu.VMEM((2, STAGE_ROWS, in_w), F32),
            pltpu.SemaphoreType.DMA((2,)),
        ],
        compiler_params=pltpu.CompilerParams(
            dimension_semantics=("arbitrary", "arbitrary"),
            vmem_limit_bytes=V7X_VMEM_LIMIT_BYTES,
        ),
        name="hybrid_block",
    )(x, x, mod, mod, norm_g.reshape(1, d_model), w_in, w_out,
      q_norm_g.reshape(1, HEAD_DIM), k_norm_g.reshape(1, HEAD_DIM), rel_bias, w_pool,
      pool_scale.reshape(1, pool_w))


def kernel(x, c, norm_g, w_ada, b_ada, w_in, q_norm_g, k_norm_g, rel_bias, w_pool, pool_scale, w_out):
    depth = w_in.shape[0]
    for l in range(depth):
        mod = _adaln_mod(c, w_ada[l], b_ada[l])
        x = _layer(x, mod, norm_g[l], w_in[l], q_norm_g[l], k_norm_g[l], rel_bias[l],
                   w_pool[l], pool_scale[l], w_out[l])
    return x
```

```python
import functools
import math

import jax
import jax.numpy as jnp
from jax import lax
from jax.experimental import pallas as pl
from jax.experimental.pallas import tpu as pltpu

CHUNK = 64
N_LEFT_CHUNKS = 8
HEAD_DIM = 64
REL_CLIP = 256
POOL_WINDOWS = (2, 4, 8, 16)
EPS = 1e-6
NEG_INF = -1e30
LOG2E = math.log2(math.e)

LANES = 128
TILE = 512
QBLK = 2 * CHUNK
BANDW = TILE + QBLK
BAND_BLOCKS = BANDW // LANES
BIAS_EXT = BANDW + QBLK
HEADS_PER_GROUP = 4
GROUP_W = HEADS_PER_GROUP * HEAD_DIM
POOL_PAD = 8
POOL_HIST = 16
POOL_OFF = POOL_PAD + POOL_HIST
ONES_ROWS = 16
SUB_ROWS = 32
NORM_ROWS = 16
BOUND_MARGIN = 1.02
L_FLOOR = 2.0 ** -90
STAGE_ROWS = 256
MOD_BLOCK_W = 1024
V7X_VMEM_LIMIT_BYTES = 56 * 1024 * 1024

F32 = jnp.float32
BF16 = jnp.bfloat16


def _mod_kernel(c_ref, w_ref, b_ref, o_ref):
    res = jnp.dot(c_ref[...], w_ref[...], preferred_element_type=F32) + b_ref[...]
    for b in range(o_ref.shape[0]):
        o_ref[b] = res[b:b + 1, :]


def _adaln_mod(c, w_ada, b_ada):
    bsz, d = c.shape
    n = w_ada.shape[1]
    return pl.pallas_call(
        _mod_kernel,
        grid=(n // MOD_BLOCK_W,),
        in_specs=[
            pl.BlockSpec((bsz, d), lambda i: (0, 0)),
            pl.BlockSpec((d, MOD_BLOCK_W), lambda i: (0, i)),
            pl.BlockSpec((1, MOD_BLOCK_W), lambda i: (0, i)),
        ],
        out_specs=pl.BlockSpec((bsz, 1, MOD_BLOCK_W), lambda i: (0, 0, i)),
        out_shape=jax.ShapeDtypeStruct((bsz, 1, n), F32),
        name="adaln_mod",
    )(c, w_ada, b_ada.reshape(1, n))


def _silu(z):
    return z * (1.0 / (1.0 + jnp.exp(-z)))


def _block_kernel(x_ref, xn_ref, mod_ref, modn_ref, g_ref, win_ref, wout_ref, gq_ref, gk_ref, rb_ref, wpool_ref, ps_ref, o_ref,
                  h_s, z_s, qbd_s, k_s, vt_s, u_s, t2_s, t4_s, t8_s, s_s, p_s, y_s, bias_ref, biasx_ref,
                  wvt_s, gain_s, wp_s, win_s, wout_s, stage_s, stage_sem,
                  *, d_model, attn_w, pool_w):
    t = pl.program_id(1)
    n_heads = attn_w // HEAD_DIM
    n_groups = attn_w // GROUP_W
    o_k, o_v, o_u, o_z = attn_w, 2 * attn_w, 3 * attn_w, 3 * attn_w + pool_w
    pool_group = pool_w // len(POOL_WINDOWS)
    pool_rows = TILE + POOL_OFF

    gate = mod_ref[:, 2 * d_model:3 * d_model]

    def norm_rows(src_ref, m_ref, r):
        a_row = g_ref[...] * (1.0 + m_ref[:, d_model:2 * d_model])
        xc = src_ref[r:r + NORM_ROWS, :]
        ms = jnp.mean(xc * xc, axis=-1, keepdims=True)
        hc = xc * lax.rsqrt(ms + EPS) * a_row + m_ref[:, 0:d_model]
        h_s[r:r + NORM_ROWS, :] = hc.astype(BF16)
        return hc[0:1, 0:LANES]

    @pl.when((pl.program_id(0) == 0) & (t == 0))
    def _():
        chunks = ([(win_ref, win_s, r) for r in range(0, win_s.shape[0], STAGE_ROWS)]
                  + [(wout_ref, wout_s, r) for r in range(0, wout_s.shape[0], STAGE_ROWS)])

        def chunk_copy(i):
            src, dst, r = chunks[i]
            return pltpu.make_async_copy(src.at[pl.ds(r, STAGE_ROWS), :],
                                         stage_s.at[i % 2, :, pl.ds(0, dst.shape[1])], stage_sem.at[i % 2])

        chunk_copy(0).start()
        for i, (_, dst, r) in enumerate(chunks):
            if i + 1 < len(chunks):
                chunk_copy(i + 1).start()
            chunk_copy(i).wait()
            dst[r:r + STAGE_ROWS, :] = stage_s[i % 2, :, 0:dst.shape[1]].astype(BF16)

        a = lax.broadcasted_iota(jnp.int32, (QBLK, LANES), 0)
        lane = lax.broadcasted_iota(jnp.int32, (QBLK, LANES), 1)
        band_lo = jnp.where(a >= CHUNK, CHUNK, 0)
        gq_max = jnp.max(jnp.abs(gq_ref[...]), axis=-1, keepdims=True)
        gk_max = jnp.max(jnp.abs(gk_ref[...]), axis=-1, keepdims=True)
        dot_bound = gq_max * gk_max * (HEAD_DIM ** 0.5 * LOG2E * BOUND_MARGIN)
        for h in range(n_heads):
            first = rb_ref[h:h + 1, 0:1] * LOG2E
            row_ext = jnp.concatenate([
                jnp.broadcast_to(first, (1, TILE - REL_CLIP)),
                rb_ref[h:h + 1, 0:REL_CLIP + QBLK] * LOG2E,
                jnp.broadcast_to(first, (1, BIAS_EXT - BANDW))], axis=1)
            row = jnp.broadcast_to(row_ext, (QBLK, BIAS_EXT))
            skew = pltpu.roll(row, 0, 1, stride=1, stride_axis=0)
            reachable = rb_ref[h:h + 1, 0:REL_CLIP + CHUNK]
            bound = dot_bound + jnp.max(reachable, axis=-1, keepdims=True) * LOG2E + 1.0
            for cb in range(BAND_BLOCKS):
                j = lane + cb * LANES
                visible = (j >= band_lo) & (j < band_lo + (N_LEFT_CHUNKS + 1) * CHUNK)
                blk = skew[:, cb * LANES:(cb + 1) * LANES]
                biasx_ref[h, cb] = jnp.where(visible, blk, NEG_INF).T
                bias_ref[h, cb] = jnp.where(visible, blk - bound, NEG_INF).T
            biasx_ref[h, BAND_BLOCKS] = jnp.full((QBLK, LANES), NEG_INF, F32)
            bias_ref[h, BAND_BLOCKS] = jnp.full((QBLK, LANES), NEG_INF, F32)
        for r in range(0, d_model, LANES):
            for c in range(0, attn_w, LANES):
                blk = win_s[r:r + LANES, o_v + c:o_v + c + LANES].astype(F32)
                wvt_s[c:c + LANES, r:r + LANES] = blk.T.astype(BF16)
        gain_s[0] = jnp.broadcast_to(jnp.concatenate([gq_ref[...]] * n_heads, axis=1)
                                     * (HEAD_DIM ** -0.5 * LOG2E), (8, attn_w))
        gain_s[1] = jnp.broadcast_to(jnp.concatenate([gk_ref[...]] * n_heads, axis=1), (8, attn_w))
        k_s[...] = jnp.zeros(k_s.shape, BF16)
        vt_s[...] = jnp.zeros(vt_s.shape, BF16)
        wp_s[...] = jnp.zeros(wp_s.shape, BF16)
        for gi in range(len(POOL_WINDOWS)):
            lo = (gi % 2) * pool_group
            wp_s[gi // 2, lo:lo + pool_group, lo:lo + pool_group] = wpool_ref[gi].astype(BF16)
        u_s[...] = jnp.zeros(u_s.shape, F32)
        t2_s[0:POOL_PAD, :] = jnp.zeros((POOL_PAD, pool_w), F32)
        t4_s[0:POOL_PAD, :] = jnp.zeros((POOL_PAD, pool_w), F32)
        t8_s[0:POOL_PAD, :] = jnp.zeros((POOL_PAD, pool_w), F32)
        for r in range(0, TILE, NORM_ROWS):
            norm_rows(x_ref, mod_ref, r)

    u_s[POOL_PAD:POOL_OFF, :] = jnp.where(t == 0, 0.0, u_s[TILE + POOL_PAD:TILE + POOL_OFF, :])

    k_s[0:TILE, :] = k_s[TILE:2 * TILE, :]
    vt_s[:, 0:TILE] = vt_s[:, TILE:2 * TILE]

    lane_g = lax.broadcasted_iota(jnp.int32, (SUB_ROWS, GROUP_W), 1)
    lane_h = lax.broadcasted_iota(jnp.int32, (SUB_ROWS, LANES), 1)

    def in_proj(rows, c0, width):
        return jnp.dot(h_s[rows, :], win_s[:, c0:c0 + width], preferred_element_type=F32)

    def head_rms(xf, gain_row):
        for sub in range(0, xf.shape[0], SUB_ROWS):
            x = xf[sub:sub + SUB_ROWS]
            sq = x * x
            ssq = []
            for c0 in range(0, GROUP_W, LANES):
                blk = sq[:, c0:c0 + LANES]
                first = jnp.sum(jnp.where(lane_h < HEAD_DIM, blk, 0.0), axis=-1, keepdims=True)
                both = jnp.sum(blk, axis=-1, keepdims=True)
                ssq.append(jnp.where(lane_h < HEAD_DIM, first, both - first))
            ssq = jnp.concatenate(ssq, axis=1)
            yield sub, x * lax.rsqrt(ssq * (1.0 / HEAD_DIM) + EPS) * gain_row

    tile_rows = slice(0, TILE)
    qf = in_proj(tile_rows, 0, attn_w)
    for rb in range(0, TILE, QBLK):
        for g in range(n_groups):
            cols = slice(g * GROUP_W, (g + 1) * GROUP_W)
            gain = gain_s[0, 0:1, cols]
            for sub, qn in head_rms(qf[rb:rb + QBLK, cols], gain):
                for h in range(HEADS_PER_GROUP):
                    keep = (lane_g >= h * HEAD_DIM) & (lane_g < (h + 1) * HEAD_DIM)
                    qbd_s[rb // QBLK, g, h * QBLK + sub:h * QBLK + sub + SUB_ROWS, :] = (
                        jnp.where(keep, qn, 0.0).astype(BF16))
    kf = in_proj(tile_rows, o_k, attn_w)
    for rb in range(0, TILE, QBLK):
        for g in range(n_groups):
            cols = slice(g * GROUP_W, (g + 1) * GROUP_W)
            for sub, kn in head_rms(kf[rb:rb + QBLK, cols], gain_s[1, 0:1, cols]):
                row = TILE + rb + sub
                k_s[row:row + SUB_ROWS, cols] = kn.astype(BF16)
    vt_s[:, TILE:2 * TILE] = lax.dot_general(
        wvt_s[...], h_s[...], (((1,), (1,)), ((), ())), preferred_element_type=F32).astype(BF16)
    u_s[POOL_OFF:POOL_OFF + TILE, :] = in_proj(tile_rows, o_u, pool_w)
    for c0 in range(0, attn_w + pool_w, 512):
        z_s[:, c0:c0 + 512] = _silu(in_proj(tile_rows, o_z + c0, 512)).astype(BF16)

    def pool():
        step = 88
        levels = ((u_s, t2_s, 1), (t2_s, t4_s, 2), (t4_s, t8_s, 4), (t8_s, None, 8))
        for gi in range(len(POOL_WINDOWS)):
            assert POOL_WINDOWS[gi] == 2 ** (gi + 1)
        for li, (src, dst, sh) in enumerate(levels[:-1]):
            lanes = slice((li + 1) * pool_group, pool_w)
            for r in range(POOL_PAD, pool_rows, step):
                dst[r:r + step, lanes] = src[r:r + step, lanes] + src[r - sh:r - sh + step, lanes]
        tok = t * TILE + lax.broadcasted_iota(jnp.int32, (QBLK, pool_group), 0)
        for r in range(0, TILE, QBLK):
            for pair in range(len(POOL_WINDOWS) // 2):
                mixed = []
                for gi in (2 * pair, 2 * pair + 1):
                    lanes = slice(gi * pool_group, (gi + 1) * pool_group)
                    src = levels[gi][0]
                    sh = levels[gi][2]
                    rows = slice(POOL_OFF + r, POOL_OFF + r + QBLK)
                    wsum = src[rows, lanes] + src[POOL_OFF + r - sh:POOL_OFF + r - sh + QBLK, lanes]
                    cnt = jnp.minimum(tok + (r + 1), POOL_WINDOWS[gi]).astype(F32)
                    mixed.append((wsum / cnt - u_s[rows, lanes]).astype(BF16))
                pm = jnp.concatenate(mixed, axis=1)
                c0 = pair * 2 * pool_group
                po = jnp.dot(pm, wp_s[pair], preferred_element_type=F32) * ps_ref[:, c0:c0 + 2 * pool_group]
                gz = z_s[r:r + QBLK, attn_w + c0:attn_w + c0 + 2 * pool_group].astype(F32)
                y_s[r:r + QBLK, attn_w + c0:attn_w + c0 + 2 * pool_group] = (po * gz).astype(BF16)

    row_q = lax.broadcasted_iota(jnp.int32, (QBLK, LANES), 0)
    n_qblk = TILE // QBLK

    def band_blocks(jb):
        n_before = TILE // LANES - jb
        return [jnp.where(t == 0, BAND_BLOCKS, cb) if cb < n_before else cb for cb in range(BAND_BLOCKS)]

    def score_dot(jb, g):
        qrow = jb * QBLK
        return lax.dot_general(k_s[qrow:qrow + BANDW, g * GROUP_W:(g + 1) * GROUP_W], qbd_s[jb, g],
                               (((1,), (1,)), ((), ())), preferred_element_type=F32)

    def probs(jb):
        buf = jb % 2
        blk = band_blocks(jb)
        for g in range(n_groups):
            st = score_dot(jb, g)
            for h in range(HEADS_PER_GROUP):
                hh = g * HEADS_PER_GROUP + h
                lanes = slice(h * QBLK, (h + 1) * QBLK)
                for cb in range(BAND_BLOCKS):
                    for sub in range(0, LANES, SUB_ROWS):
                        rows = slice(cb * LANES + sub, cb * LANES + sub + SUB_ROWS)
                        p = jnp.exp2(st[rows, lanes] + bias_ref[hh, blk[cb], sub:sub + SUB_ROWS, :])
                        p_s[buf, hh // 2, rows, (hh % 2) * QBLK:(hh % 2 + 1) * QBLK] = p.astype(BF16)

    def probs_exact(jb):
        buf = jb % 2
        blk = band_blocks(jb)
        for g in range(n_groups):
            st = score_dot(jb, g)
            for h in range(HEADS_PER_GROUP):
                hh = g * HEADS_PER_GROUP + h
                lanes = slice(h * QBLK, (h + 1) * QBLK)
                macc = None
                for cb in range(BAND_BLOCKS):
                    for sub in range(0, LANES, SUB_ROWS):
                        rows = slice(cb * LANES + sub, cb * LANES + sub + SUB_ROWS)
                        sb = st[rows, lanes] + biasx_ref[hh, blk[cb], sub:sub + SUB_ROWS, :]
                        s_s[hh, rows, :] = sb
                        for r8 in range(0, SUB_ROWS, 8):
                            macc = sb[r8:r8 + 8, :] if macc is None else jnp.maximum(macc, sb[r8:r8 + 8, :])
                m = jnp.max(macc, axis=0, keepdims=True)
                for r in range(0, BANDW, SUB_ROWS):
                    rows = slice(r, r + SUB_ROWS)
                    p = jnp.exp2(s_s[hh, rows, :] - m)
                    p_s[buf, hh // 2, rows, (hh % 2) * QBLK:(hh % 2 + 1) * QBLK] = p.astype(BF16)

    ones_rows = jnp.ones((ONES_ROWS, BANDW), BF16)

    def attend(jb, anchor=None):
        buf = jb % 2
        qrow = jb * QBLK
        lmin = None
        for pair in range(n_heads // 2):
            ha = 2 * pair
            vt1 = jnp.concatenate(
                [vt_s[ha * HEAD_DIM:(ha + 2) * HEAD_DIM, qrow:qrow + BANDW], ones_rows], axis=0)
            ot = jnp.dot(vt1, p_s[buf, pair], preferred_element_type=F32)
            l = ot[2 * HEAD_DIM:2 * HEAD_DIM + 1, :]
            lmin = l if lmin is None else jnp.minimum(lmin, l)
            linv = 1.0 / l
            if anchor is not None:
                linv = jnp.concatenate([jnp.where(never, anchor(pair), linv[:, 0:LANES]), linv[:, LANES:]], axis=1)
            oa = ot[0:2 * HEAD_DIM, 0:QBLK] * linv[:, 0:QBLK]
            ob = ot[0:2 * HEAD_DIM, QBLK:2 * QBLK] * linv[:, QBLK:2 * QBLK]
            a = jnp.where(row_q < HEAD_DIM, oa, ob).T
            c0 = ha * HEAD_DIM
            gz = z_s[qrow:qrow + QBLK, c0:c0 + LANES].astype(F32)
            y_s[qrow:qrow + QBLK, c0:c0 + LANES] = (a * gz).astype(BF16)
        return jnp.min(lmin)

    never = t < 0
    norm_steps = list(range(0, TILE, NORM_ROWS))
    per_pair = len(norm_steps) // (n_qblk * (n_heads // 2))

    def next_norm(jb):
        def anchor(pair):
            i0 = (jb * (n_heads // 2) + pair) * per_pair
            return sum(norm_rows(xn_ref, modn_ref, rn) for rn in norm_steps[i0:i0 + per_pair])
        return anchor

    probs(0)
    pool()
    underflowed = []
    for jb in range(n_qblk):
        if jb + 1 < n_qblk:
            probs(jb + 1)
        underflowed.append(jnp.logical_not(attend(jb, next_norm(jb)) >= L_FLOOR))

    o = jnp.dot(y_s[...], wout_s[...], preferred_element_type=F32)
    for c0 in range(0, d_model, GROUP_W):
        cols = slice(c0, c0 + GROUP_W)
        for r in range(0, TILE, SUB_ROWS):
            o_ref[r:r + SUB_ROWS, cols] = x_ref[r:r + SUB_ROWS, cols] + gate[:, cols] * o[r:r + SUB_ROWS, cols]

    @pl.when(functools.reduce(jnp.logical_or, underflowed))
    def _():
        for jb in range(n_qblk):
            @pl.when(underflowed[jb])
            def _(jb=jb):
                probs_exact(jb)
                attend(jb)
                rows = slice(jb * QBLK, (jb + 1) * QBLK)
                o_jb = jnp.dot(y_s[rows, :], wout_s[...], preferred_element_type=F32)
                o_ref[rows, :] = x_ref[rows, :] + gate * o_jb


def _layer(x, mod, norm_g, w_in, q_norm_g, k_norm_g, rel_bias, w_pool, pool_scale, w_out):
    bsz, seq, d_model = x.shape
    n_heads = rel_bias.shape[0]
    attn_w = n_heads * HEAD_DIM
    pool_w = w_pool.shape[0] * w_pool.shape[1]
    pool_group = w_pool.shape[1]
    in_w = w_in.shape[1]
    assert seq % TILE == 0 and attn_w % GROUP_W == 0 and TILE == N_LEFT_CHUNKS * CHUNK
    assert in_w == 3 * attn_w + pool_w + attn_w + pool_w and 2 * pool_group == GROUP_W
    assert attn_w == 512 and pool_w == 512
    n_groups = attn_w // GROUP_W

    n_pairs = w_pool.shape[0] // 2
    assert QBLK <= REL_CLIP <= TILE and rel_bias.shape[1] == 2 * REL_CLIP + 1

    tiles_per_seq = seq // TILE

    def next_tile(b, t):
        n = jnp.minimum(b * tiles_per_seq + t + 1, bsz * tiles_per_seq - 1)
        return n // tiles_per_seq, n % tiles_per_seq

    const2 = lambda b, t: (0, 0)
    const3 = lambda b, t: (0, 0, 0)
    kernel = functools.partial(_block_kernel, d_model=d_model, attn_w=attn_w, pool_w=pool_w)
    return pl.pallas_call(
        kernel,
        grid=(bsz, seq // TILE),
        in_specs=[
            pl.BlockSpec((None, TILE, d_model), lambda b, t: (b, t, 0)),
            pl.BlockSpec((None, TILE, d_model), lambda b, t: (*next_tile(b, t), 0)),
            pl.BlockSpec((None, 1, 3 * d_model), lambda b, t: (b, 0, 0)),
            pl.BlockSpec((None, 1, 3 * d_model), lambda b, t: (next_tile(b, t)[0], 0, 0)),
            pl.BlockSpec((1, d_model), const2),
            pl.BlockSpec(memory_space=pl.ANY),
            pl.BlockSpec(memory_space=pl.ANY),
            pl.BlockSpec((1, HEAD_DIM), const2),
            pl.BlockSpec((1, HEAD_DIM), const2),
            pl.BlockSpec((n_heads, 2 * REL_CLIP + 1), const2),
            pl.BlockSpec((2 * n_pairs, pool_group, pool_group), const3),
            pl.BlockSpec((1, pool_w), const2),
        ],
        out_specs=pl.BlockSpec((None, TILE, d_model), lambda b, t: (b, t, 0)),
        out_shape=jax.ShapeDtypeStruct(x.shape, x.dtype),
        scratch_shapes=[
            pltpu.VMEM((TILE, d_model), BF16),
            pltpu.VMEM((TILE, attn_w + pool_w), BF16),
            pltpu.VMEM((TILE // QBLK, n_groups, HEADS_PER_GROUP * QBLK, GROUP_W), BF16),
            pltpu.VMEM((2 * TILE, attn_w), BF16),
            pltpu.VMEM((attn_w, 2 * TILE), BF16),
            pltpu.VMEM((TILE + POOL_OFF, pool_w), F32),
            pltpu.VMEM((TILE + POOL_OFF, pool_w), F32),
            pltpu.VMEM((TILE + POOL_OFF, pool_w), F32),
            pltpu.VMEM((TILE + POOL_OFF, pool_w), F32),
            pltpu.VMEM((n_heads, BANDW, QBLK), F32),
            pltpu.VMEM((2, n_heads // 2, BANDW, 2 * QBLK), BF16),
            pltpu.VMEM((TILE, attn_w + pool_w), BF16),
            pltpu.VMEM((n_heads, BAND_BLOCKS + 1, QBLK, LANES), F32),
            pltpu.VMEM((n_heads, BAND_BLOCKS + 1, QBLK, LANES), F32),
            pltpu.VMEM((attn_w, d_model), BF16),
            pltpu.VMEM((2, 8, attn_w), F32),
            pltpu.VMEM((n_pairs, GROUP_W, GROUP_W), BF16),
            pltpu.VMEM((d_model, in_w), BF16),
            pltpu.VMEM((attn_w + pool_w, d_model), BF16),
            pltpu.VMEM((2, STAGE_ROWS, in_w), F32),
            pltpu.SemaphoreType.DMA((2,)),
        ],
        compiler_params=pltpu.CompilerParams(
            dimension_semantics=("arbitrary", "arbitrary"),
            vmem_limit_bytes=V7X_VMEM_LIMIT_BYTES,
        ),
        name="hybrid_block",
    )(x, x, mod, mod, norm_g.reshape(1, d_model), w_in, w_out,
      q_norm_g.reshape(1, HEAD_DIM), k_norm_g.reshape(1, HEAD_DIM), rel_bias, w_pool,
      pool_scale.reshape(1, pool_w))


def kernel(x, c, norm_g, w_ada, b_ada, w_in, q_norm_g, k_norm_g, rel_bias, w_pool, pool_scale, w_out):
    depth = w_in.shape[0]
    for l in range(depth):
        mod = _adaln_mod(c, w_ada[l], b_ada[l])
        x = _layer(x, mod, norm_g[l], w_in[l], q_norm_g[l], k_norm_g[l], rel_bias[l],
                   w_pool[l], pool_scale[l], w_out[l])
    return x
```

```python
import functools
import math

import jax
import jax.numpy as jnp
from jax import lax
from jax.experimental import pallas as pl
from jax.experimental.pallas import tpu as pltpu

CHUNK = 64
N_LEFT_CHUNKS = 8
HEAD_DIM = 64
REL_CLIP = 256
POOL_WINDOWS = (2, 4, 8, 16)
EPS = 1e-6
NEG_INF = -1e30
LOG2E = math.log2(math.e)

LANES = 128
TILE = 512
QBLK = 2 * CHUNK
BANDW = TILE + QBLK
BAND_BLOCKS = BANDW // LANES
BIAS_EXT = BANDW + QBLK
HEADS_PER_GROUP = 4
GROUP_W = HEADS_PER_GROUP * HEAD_DIM
POOL_PAD = 8
POOL_HIST = 16
POOL_OFF = POOL_PAD + POOL_HIST
ONES_ROWS = 16
SUB_ROWS = 32
NORM_ROWS = 16
BOUND_MARGIN = 1.02
L_FLOOR = 2.0 ** -90
STAGE_ROWS = 256
V7X_VMEM_LIMIT_BYTES = 56 * 1024 * 1024

F32 = jnp.float32
BF16 = jnp.bfloat16


def _silu(z):
    return z * (1.0 / (1.0 + jnp.exp(-z)))


def _block_kernel(x_ref, xn_ref, c_ref, wada_ref, bada_ref, g_ref, win_ref, wout_ref, gq_ref, gk_ref, rb_ref, wpool_ref,
                  ps_ref, o_ref,
                  h_s, z_s, qbd_s, k_s, vt_s, u_s, t2_s, t4_s, t8_s, s_s, p_s, y_s, bias_ref, biasx_ref,
                  wvt_s, gain_s, wp_s, win_s, wout_s, mod_s, stage_s, stage_sem,
                  *, d_model, attn_w, pool_w, tiles_per_seq, n_tiles):
    t = pl.program_id(1)
    n_heads = attn_w // HEAD_DIM
    n_groups = attn_w // GROUP_W
    o_k, o_v, o_u, o_z = attn_w, 2 * attn_w, 3 * attn_w, 3 * attn_w + pool_w
    pool_group = pool_w // len(POOL_WINDOWS)
    pool_rows = TILE + POOL_OFF

    seq = pl.program_id(0)
    seq_next = jnp.minimum(seq * tiles_per_seq + t + 1, n_tiles - 1) // tiles_per_seq

    def norm_rows(src_ref, which_seq, r):
        shift = mod_s[pl.ds(which_seq, 1), 0:d_model]
        scale = mod_s[pl.ds(which_seq, 1), d_model:2 * d_model]
        xc = src_ref[r:r + NORM_ROWS, :]
        ms = jnp.mean(xc * xc, axis=-1, keepdims=True)
        hc = xc * lax.rsqrt(ms + EPS) * (g_ref[...] * (1.0 + scale)) + shift
        h_s[r:r + NORM_ROWS, :] = hc.astype(BF16)
        return hc[0:1, 0:LANES]

    @pl.when((seq == 0) & (t == 0))
    def _():
        chunks = ([(wada_ref, None, r) for r in range(0, d_model, STAGE_ROWS)]
                  + [(win_ref, win_s, r) for r in range(0, win_s.shape[0], STAGE_ROWS)]
                  + [(wout_ref, wout_s, r) for r in range(0, wout_s.shape[0], STAGE_ROWS)])

        def chunk_copy(i):
            src = chunks[i][0]
            return pltpu.make_async_copy(src.at[pl.ds(chunks[i][2], STAGE_ROWS), :],
                                         stage_s.at[i % 2, :, pl.ds(0, src.shape[1])], stage_sem.at[i % 2])

        mod = bada_ref[...]
        chunk_copy(0).start()
        for i, (src, dst, r) in enumerate(chunks):
            if i + 1 < len(chunks):
                chunk_copy(i + 1).start()
            chunk_copy(i).wait()
            staged = stage_s[i % 2, :, 0:src.shape[1]]
            if dst is None:
                mod = mod + jnp.dot(c_ref[:, r:r + STAGE_ROWS], staged, preferred_element_type=F32)
            else:
                dst[r:r + STAGE_ROWS, :] = staged.astype(BF16)
            if i + 1 == d_model // STAGE_ROWS:
                mod_s[0:c_ref.shape[0], :] = mod

        a = lax.broadcasted_iota(jnp.int32, (QBLK, LANES), 0)
        lane = lax.broadcasted_iota(jnp.int32, (QBLK, LANES), 1)
        band_lo = jnp.where(a >= CHUNK, CHUNK, 0)
        gq_max = jnp.max(jnp.abs(gq_ref[...]), axis=-1, keepdims=True)
        gk_max = jnp.max(jnp.abs(gk_ref[...]), axis=-1, keepdims=True)
        dot_bound = gq_max * gk_max * (HEAD_DIM ** 0.5 * LOG2E * BOUND_MARGIN)
        for h in range(n_heads):
            first = rb_ref[h:h + 1, 0:1] * LOG2E
            row_ext = jnp.concatenate([
                jnp.broadcast_to(first, (1, TILE - REL_CLIP)),
                rb_ref[h:h + 1, 0:REL_CLIP + QBLK] * LOG2E,
                jnp.broadcast_to(first, (1, BIAS_EXT - BANDW))], axis=1)
            row = jnp.broadcast_to(row_ext, (QBLK, BIAS_EXT))
            skew = pltpu.roll(row, 0, 1, stride=1, stride_axis=0)
            reachable = rb_ref[h:h + 1, 0:REL_CLIP + CHUNK]
            bound = dot_bound + jnp.max(reachable, axis=-1, keepdims=True) * LOG2E + 1.0
            for cb in range(BAND_BLOCKS):
                j = lane + cb * LANES
                visible = (j >= band_lo) & (j < band_lo + (N_LEFT_CHUNKS + 1) * CHUNK)
                blk = skew[:, cb * LANES:(cb + 1) * LANES]
                biasx_ref[h, cb] = jnp.where(visible, blk, NEG_INF).T
                bias_ref[h, cb] = jnp.where(visible, blk - bound, NEG_INF).T
            biasx_ref[h, BAND_BLOCKS] = jnp.full((QBLK, LANES), NEG_INF, F32)
            bias_ref[h, BAND_BLOCKS] = jnp.full((QBLK, LANES), NEG_INF, F32)
        for r in range(0, d_model, LANES):
            for c in range(0, attn_w, LANES):
                blk = win_s[r:r + LANES, o_v + c:o_v + c + LANES].astype(F32)
                wvt_s[c:c + LANES, r:r + LANES] = blk.T.astype(BF16)
        gain_s[0] = jnp.broadcast_to(jnp.concatenate([gq_ref[...]] * n_heads, axis=1)
                                     * (HEAD_DIM ** -0.5 * LOG2E), (8, attn_w))
        gain_s[1] = jnp.broadcast_to(jnp.concatenate([gk_ref[...]] * n_heads, axis=1), (8, attn_w))
        k_s[...] = jnp.zeros(k_s.shape, BF16)
        vt_s[...] = jnp.zeros(vt_s.shape, BF16)
        wp_s[...] = jnp.zeros(wp_s.shape, BF16)
        for gi in range(len(POOL_WINDOWS)):
            lo = (gi % 2) * pool_group
            wp_s[gi // 2, lo:lo + pool_group, lo:lo + pool_group] = wpool_ref[gi].astype(BF16)
        u_s[...] = jnp.zeros(u_s.shape, F32)
        t2_s[0:POOL_PAD, :] = jnp.zeros((POOL_PAD, pool_w), F32)
        t4_s[0:POOL_PAD, :] = jnp.zeros((POOL_PAD, pool_w), F32)
        t8_s[0:POOL_PAD, :] = jnp.zeros((POOL_PAD, pool_w), F32)
        for r in range(0, TILE, NORM_ROWS):
            norm_rows(x_ref, 0, r)

    gate = mod_s[pl.ds(seq, 1), 2 * d_model:3 * d_model]

    u_s[POOL_PAD:POOL_OFF, :] = jnp.where(t == 0, 0.0, u_s[TILE + POOL_PAD:TILE + POOL_OFF, :])

    k_s[0:TILE, :] = k_s[TILE:2 * TILE, :]
    vt_s[:, 0:TILE] = vt_s[:, TILE:2 * TILE]

    lane_g = lax.broadcasted_iota(jnp.int32, (SUB_ROWS, GROUP_W), 1)
    lane_h = lax.broadcasted_iota(jnp.int32, (SUB_ROWS, LANES), 1)

    def in_proj(rows, c0, width):
        return jnp.dot(h_s[rows, :], win_s[:, c0:c0 + width], preferred_element_type=F32)

    def head_rms(xf, gain_row):
        for sub in range(0, xf.shape[0], SUB_ROWS):
            x = xf[sub:sub + SUB_ROWS]
            sq = x * x
            ssq = []
            for c0 in range(0, GROUP_W, LANES):
                blk = sq[:, c0:c0 + LANES]
                first = jnp.sum(jnp.where(lane_h < HEAD_DIM, blk, 0.0), axis=-1, keepdims=True)
                both = jnp.sum(blk, axis=-1, keepdims=True)
                ssq.append(jnp.where(lane_h < HEAD_DIM, first, both - first))
            ssq = jnp.concatenate(ssq, axis=1)
            yield sub, x * lax.rsqrt(ssq * (1.0 / HEAD_DIM) + EPS) * gain_row

    tile_rows = slice(0, TILE)
    qf = in_proj(tile_rows, 0, attn_w)
    for rb in range(0, TILE, QBLK):
        for g in range(n_groups):
            cols = slice(g * GROUP_W, (g + 1) * GROUP_W)
            gain = gain_s[0, 0:1, cols]
            for sub, qn in head_rms(qf[rb:rb + QBLK, cols], gain):
                for h in range(HEADS_PER_GROUP):
                    keep = (lane_g >= h * HEAD_DIM) & (lane_g < (h + 1) * HEAD_DIM)
                    qbd_s[rb // QBLK, g, h * QBLK + sub:h * QBLK + sub + SUB_ROWS, :] = (
                        jnp.where(keep, qn, 0.0).astype(BF16))
    kf = in_proj(tile_rows, o_k, attn_w)
    for rb in range(0, TILE, QBLK):
        for g in range(n_groups):
            cols = slice(g * GROUP_W, (g + 1) * GROUP_W)
            for sub, kn in head_rms(kf[rb:rb + QBLK, cols], gain_s[1, 0:1, cols]):
                row = TILE + rb + sub
                k_s[row:row + SUB_ROWS, cols] = kn.astype(BF16)
    vt_s[:, TILE:2 * TILE] = lax.dot_general(
        wvt_s[...], h_s[...], (((1,), (1,)), ((), ())), preferred_element_type=F32).astype(BF16)
    u_s[POOL_OFF:POOL_OFF + TILE, :] = in_proj(tile_rows, o_u, pool_w)
    for c0 in range(0, attn_w + pool_w, 512):
        z_s[:, c0:c0 + 512] = _silu(in_proj(tile_rows, o_z + c0, 512)).astype(BF16)

    def pool():
        step = 88
        levels = ((u_s, t2_s, 1), (t2_s, t4_s, 2), (t4_s, t8_s, 4), (t8_s, None, 8))
        for gi in range(len(POOL_WINDOWS)):
            assert POOL_WINDOWS[gi] == 2 ** (gi + 1)
        for li, (src, dst, sh) in enumerate(levels[:-1]):
            lanes = slice((li + 1) * pool_group, pool_w)
            for r in range(POOL_PAD, pool_rows, step):
                dst[r:r + step, lanes] = src[r:r + step, lanes] + src[r - sh:r - sh + step, lanes]
        tok = t * TILE + lax.broadcasted_iota(jnp.int32, (QBLK, pool_group), 0)
        for r in range(0, TILE, QBLK):
            for pair in range(len(POOL_WINDOWS) // 2):
                mixed = []
                for gi in (2 * pair, 2 * pair + 1):
                    lanes = slice(gi * pool_group, (gi + 1) * pool_group)
                    src = levels[gi][0]
                    sh = levels[gi][2]
                    rows = slice(POOL_OFF + r, POOL_OFF + r + QBLK)
                    wsum = src[rows, lanes] + src[POOL_OFF + r - sh:POOL_OFF + r - sh + QBLK, lanes]
                    cnt = jnp.minimum(tok + (r + 1), POOL_WINDOWS[gi]).astype(F32)
                    mixed.append((wsum / cnt - u_s[rows, lanes]).astype(BF16))
                pm = jnp.concatenate(mixed, axis=1)
                c0 = pair * 2 * pool_group
                po = jnp.dot(pm, wp_s[pair], preferred_element_type=F32) * ps_ref[:, c0:c0 + 2 * pool_group]
                gz = z_s[r:r + QBLK, attn_w + c0:attn_w + c0 + 2 * pool_group].astype(F32)
                y_s[r:r + QBLK, attn_w + c0:attn_w + c0 + 2 * pool_group] = (po * gz).astype(BF16)

    row_q = lax.broadcasted_iota(jnp.int32, (QBLK, LANES), 0)
    n_qblk = TILE // QBLK

    def band_blocks(jb):
        n_before = TILE // LANES - jb
        return [jnp.where(t == 0, BAND_BLOCKS, cb) if cb < n_before else cb for cb in range(BAND_BLOCKS)]

    def score_dot(jb, g):
        qrow = jb * QBLK
        return lax.dot_general(k_s[qrow:qrow + BANDW, g * GROUP_W:(g + 1) * GROUP_W], qbd_s[jb, g],
                               (((1,), (1,)), ((), ())), preferred_element_type=F32)

    def probs(jb):
        buf = jb % 2
        blk = band_blocks(jb)
        for g in range(n_groups):
            st = score_dot(jb, g)
            for h in range(HEADS_PER_GROUP):
                hh = g * HEADS_PER_GROUP + h
                lanes = slice(h * QBLK, (h + 1) * QBLK)
                for cb in range(BAND_BLOCKS):
                    for sub in range(0, LANES, SUB_ROWS):
                        rows = slice(cb * LANES + sub, cb * LANES + sub + SUB_ROWS)
                        p = jnp.exp2(st[rows, lanes] + bias_ref[hh, blk[cb], sub:sub + SUB_ROWS, :])
                        p_s[buf, hh // 2, rows, (hh % 2) * QBLK:(hh % 2 + 1) * QBLK] = p.astype(BF16)

    def probs_exact(jb):
        buf = jb % 2
        blk = band_blocks(jb)
        for g in range(n_groups):
            st = score_dot(jb, g)
            for h in range(HEADS_PER_GROUP):
                hh = g * HEADS_PER_GROUP + h
                lanes = slice(h * QBLK, (h + 1) * QBLK)
                macc = None
                for cb in range(BAND_BLOCKS):
                    for sub in range(0, LANES, SUB_ROWS):
                        rows = slice(cb * LANES + sub, cb * LANES + sub + SUB_ROWS)
                        sb = st[rows, lanes] + biasx_ref[hh, blk[cb], sub:sub + SUB_ROWS, :]
                        s_s[hh, rows, :] = sb
                        for r8 in range(0, SUB_ROWS, 8):
                            macc = sb[r8:r8 + 8, :] if macc is None else jnp.maximum(macc, sb[r8:r8 + 8, :])
                m = jnp.max(macc, axis=0, keepdims=True)
                for r in range(0, BANDW, SUB_ROWS):
                    rows = slice(r, r + SUB_ROWS)
                    p = jnp.exp2(s_s[hh, rows, :] - m)
                    p_s[buf, hh // 2, rows, (hh % 2) * QBLK:(hh % 2 + 1) * QBLK] = p.astype(BF16)

    ones_rows = jnp.ones((ONES_ROWS, BANDW), BF16)

    def attend(jb, anchor=None):
        buf = jb % 2
        qrow = jb * QBLK
        lmin = None
        for pair in range(n_heads // 2):
            ha = 2 * pair
            vt1 = jnp.concatenate(
                [vt_s[ha * HEAD_DIM:(ha + 2) * HEAD_DIM, qrow:qrow + BANDW], ones_rows], axis=0)
            ot = jnp.dot(vt1, p_s[buf, pair], preferred_element_type=F32)
            l = ot[2 * HEAD_DIM:2 * HEAD_DIM + 1, :]
            lmin = l if lmin is None else jnp.minimum(lmin, l)
            linv = 1.0 / l
            if anchor is not None:
                linv = jnp.concatenate([jnp.where(never, anchor(pair), linv[:, 0:LANES]), linv[:, LANES:]], axis=1)
            oa = ot[0:2 * HEAD_DIM, 0:QBLK] * linv[:, 0:QBLK]
            ob = ot[0:2 * HEAD_DIM, QBLK:2 * QBLK] * linv[:, QBLK:2 * QBLK]
            a = jnp.where(row_q < HEAD_DIM, oa, ob).T
            c0 = ha * HEAD_DIM
            gz = z_s[qrow:qrow + QBLK, c0:c0 + LANES].astype(F32)
            y_s[qrow:qrow + QBLK, c0:c0 + LANES] = (a * gz).astype(BF16)
        return jnp.min(lmin)

    never = t < 0
    norm_steps = list(range(0, TILE, NORM_ROWS))
    per_pair = len(norm_steps) // (n_qblk * (n_heads // 2))

    def next_norm(jb):
        def anchor(pair):
            i0 = (jb * (n_heads // 2) + pair) * per_pair
            return sum(norm_rows(xn_ref, seq_next, rn) for rn in norm_steps[i0:i0 + per_pair])
        return anchor

    probs(0)
    pool()
    underflowed = []
    for jb in range(n_qblk):
        if jb + 1 < n_qblk:
            probs(jb + 1)
        underflowed.append(jnp.logical_not(attend(jb, next_norm(jb)) >= L_FLOOR))

    o = jnp.dot(y_s[...], wout_s[...], preferred_element_type=F32)
    for c0 in range(0, d_model, GROUP_W):
        cols = slice(c0, c0 + GROUP_W)
        for r in range(0, TILE, SUB_ROWS):
            o_ref[r:r + SUB_ROWS, cols] = x_ref[r:r + SUB_ROWS, cols] + gate[:, cols] * o[r:r + SUB_ROWS, cols]

    @pl.when(functools.reduce(jnp.logical_or, underflowed))
    def _():
        for jb in range(n_qblk):
            @pl.when(underflowed[jb])
            def _(jb=jb):
                probs_exact(jb)
                attend(jb)
                rows = slice(jb * QBLK, (jb + 1) * QBLK)
                o_jb = jnp.dot(y_s[rows, :], wout_s[...], preferred_element_type=F32)
                o_ref[rows, :] = x_ref[rows, :] + gate * o_jb


def _layer(x, c, w_ada, b_ada, norm_g, w_in, q_norm_g, k_norm_g, rel_bias, w_pool, pool_scale, w_out):
    bsz, seq, d_model = x.shape
    n_heads = rel_bias.shape[0]
    attn_w = n_heads * HEAD_DIM
    pool_w = w_pool.shape[0] * w_pool.shape[1]
    pool_group = w_pool.shape[1]
    in_w = w_in.shape[1]
    assert seq % TILE == 0 and attn_w % GROUP_W == 0 and TILE == N_LEFT_CHUNKS * CHUNK
    assert in_w == 3 * attn_w + pool_w + attn_w + pool_w and 2 * pool_group == GROUP_W
    assert attn_w == 512 and pool_w == 512 and bsz <= 8
    n_groups = attn_w // GROUP_W

    n_pairs = w_pool.shape[0] // 2
    assert QBLK <= REL_CLIP <= TILE and rel_bias.shape[1] == 2 * REL_CLIP + 1

    tiles_per_seq = seq // TILE

    def next_tile(b, t):
        n = jnp.minimum(b * tiles_per_seq + t + 1, bsz * tiles_per_seq - 1)
        return n // tiles_per_seq, n % tiles_per_seq

    const2 = lambda b, t: (0, 0)
    const3 = lambda b, t: (0, 0, 0)
    kernel = functools.partial(_block_kernel, d_model=d_model, attn_w=attn_w, pool_w=pool_w,
                               tiles_per_seq=tiles_per_seq, n_tiles=bsz * tiles_per_seq)
    return pl.pallas_call(
        kernel,
        grid=(bsz, seq // TILE),
        in_specs=[
            pl.BlockSpec((None, TILE, d_model), lambda b, t: (b, t, 0)),
            pl.BlockSpec((None, TILE, d_model), lambda b, t: (*next_tile(b, t), 0)),
            pl.BlockSpec((bsz, d_model), const2),
            pl.BlockSpec(memory_space=pl.ANY),
            pl.BlockSpec((1, 3 * d_model), const2),
            pl.BlockSpec((1, d_model), const2),
            pl.BlockSpec(memory_space=pl.ANY),
            pl.BlockSpec(memory_space=pl.ANY),
            pl.BlockSpec((1, HEAD_DIM), const2),
            pl.BlockSpec((1, HEAD_DIM), const2),
            pl.BlockSpec((n_heads, 2 * REL_CLIP + 1), const2),
            pl.BlockSpec((2 * n_pairs, pool_group, pool_group), const3),
            pl.BlockSpec((1, pool_w), const2),
        ],
        out_specs=pl.BlockSpec((None, TILE, d_model), lambda b, t: (b, t, 0)),
        out_shape=jax.ShapeDtypeStruct(x.shape, x.dtype),
        scratch_shapes=[
            pltpu.VMEM((TILE, d_model), BF16),
            pltpu.VMEM((TILE, attn_w + pool_w), BF16),
            pltpu.VMEM((TILE // QBLK, n_groups, HEADS_PER_GROUP * QBLK, GROUP_W), BF16),
            pltpu.VMEM((2 * TILE, attn_w), BF16),
            pltpu.VMEM((attn_w, 2 * TILE), BF16),
            pltpu.VMEM((TILE + POOL_OFF, pool_w), F32),
            pltpu.VMEM((TILE + POOL_OFF, pool_w), F32),
            pltpu.VMEM((TILE + POOL_OFF, pool_w), F32),
            pltpu.VMEM((TILE + POOL_OFF, pool_w), F32),
            pltpu.VMEM((n_heads, BANDW, QBLK), F32),
            pltpu.VMEM((2, n_heads // 2, BANDW, 2 * QBLK), BF16),
            pltpu.VMEM((TILE, attn_w + pool_w), BF16),
            pltpu.VMEM((n_heads, BAND_BLOCKS + 1, QBLK, LANES), F32),
            pltpu.VMEM((n_heads, BAND_BLOCKS + 1, QBLK, LANES), F32),
            pltpu.VMEM((attn_w, d_model), BF16),
            pltpu.VMEM((2, 8, attn_w), F32),
            pltpu.VMEM((n_pairs, GROUP_W, GROUP_W), BF16),
            pltpu.VMEM((d_model, in_w), BF16),
            pltpu.VMEM((attn_w + pool_w, d_model), BF16),
            pltpu.VMEM((8, 3 * d_model), F32),
            pltpu.VMEM((2, STAGE_ROWS, in_w), F32),
            pltpu.SemaphoreType.DMA((2,)),
        ],
        compiler_params=pltpu.CompilerParams(
            dimension_semantics=("arbitrary", "arbitrary"),
            vmem_limit_bytes=V7X_VMEM_LIMIT_BYTES,
        ),
        name="hybrid_block",
    )(x, x, c, w_ada, b_ada.reshape(1, 3 * d_model), norm_g.reshape(1, d_model), w_in, w_out,
      q_norm_g.reshape(1, HEAD_DIM), k_norm_g.reshape(1, HEAD_DIM), rel_bias, w_pool,
      pool_scale.reshape(1, pool_w))


def kernel(x, c, norm_g, w_ada, b_ada, w_in, q_norm_g, k_norm_g, rel_bias, w_pool, pool_scale, w_out):
    depth = w_in.shape[0]
    for l in range(depth):
        x = _layer(x, c, w_ada[l], b_ada[l], norm_g[l], w_in[l], q_norm_g[l], k_norm_g[l], rel_bias[l],
                   w_pool[l], pool_scale[l], w_out[l])
    return x
```

```python
import functools
import math

import jax
import jax.numpy as jnp
from jax import lax
from jax.experimental import pallas as pl
from jax.experimental.pallas import tpu as pltpu

CHUNK = 64
N_LEFT_CHUNKS = 8
HEAD_DIM = 64
REL_CLIP = 256
POOL_WINDOWS = (2, 4, 8, 16)
EPS = 1e-6
NEG_INF = -1e30
LOG2E = math.log2(math.e)

LANES = 128
TILE = 512
QBLK = 2 * CHUNK
BANDW = TILE + QBLK
BAND_BLOCKS = BANDW // LANES
BIAS_EXT = BANDW + QBLK
HEADS_PER_GROUP = 4
GROUP_W = HEADS_PER_GROUP * HEAD_DIM
POOL_PAD = 8
POOL_HIST = 16
POOL_OFF = POOL_PAD + POOL_HIST
ONES_ROWS = 16
SUB_ROWS = 32
NORM_ROWS = 16
BOUND_MARGIN = 1.02
L_FLOOR = 2.0 ** -90
STAGE_ROWS = 256
V7X_VMEM_LIMIT_BYTES = 56 * 1024 * 1024

F32 = jnp.float32
BF16 = jnp.bfloat16


def _silu(z):
    return z * (1.0 / (1.0 + jnp.exp(-z)))


def _block_kernel(x_ref, xn_ref, c_ref, wada_ref, bada_ref, g_ref, win_ref, wout_ref, gq_ref, gk_ref, rb_ref, wpool_ref,
                  ps_ref, o_ref,
                  h_s, z_s, qbd_s, k_s, vt_s, u_s, t2_s, t4_s, t8_s, s_s, p_s, y_s, bias_ref, biasx_ref,
                  wvt_s, gain_s, wp_s, win_s, wout_s, mod_s, stage_s, stage_sem,
                  *, d_model, attn_w, pool_w, tiles_per_seq, n_tiles):
    t = pl.program_id(1)
    n_heads = attn_w // HEAD_DIM
    n_groups = attn_w // GROUP_W
    o_k, o_v, o_u, o_z = attn_w, 2 * attn_w, 3 * attn_w, 3 * attn_w + pool_w
    pool_group = pool_w // len(POOL_WINDOWS)
    pool_rows = TILE + POOL_OFF

    seq = pl.program_id(0)
    seq_next = jnp.minimum(seq * tiles_per_seq + t + 1, n_tiles - 1) // tiles_per_seq

    def norm_rows(src_ref, which_seq, r):
        shift = mod_s[pl.ds(which_seq, 1), 0:d_model]
        scale = mod_s[pl.ds(which_seq, 1), d_model:2 * d_model]
        xc = src_ref[r:r + NORM_ROWS, :]
        ms = jnp.mean(xc * xc, axis=-1, keepdims=True)
        hc = xc * lax.rsqrt(ms + EPS) * (g_ref[...] * (1.0 + scale)) + shift
        h_s[r:r + NORM_ROWS, :] = hc.astype(BF16)
        return hc[0:1, 0:LANES]

    @pl.when((seq == 0) & (t == 0))
    def _():
        chunks = ([(wada_ref, None, r) for r in range(0, d_model, STAGE_ROWS)]
                  + [(win_ref, win_s, r) for r in range(0, win_s.shape[0], STAGE_ROWS)]
                  + [(wout_ref, wout_s, r) for r in range(0, wout_s.shape[0], STAGE_ROWS)])

        def chunk_copy(i):
            src = chunks[i][0]
            return pltpu.make_async_copy(src.at[pl.ds(chunks[i][2], STAGE_ROWS), :],
                                         stage_s.at[i % 2, :, pl.ds(0, src.shape[1])], stage_sem.at[i % 2])

        mod = bada_ref[...]
        chunk_copy(0).start()
        for i, (src, dst, r) in enumerate(chunks):
            if i + 1 < len(chunks):
                chunk_copy(i + 1).start()
            chunk_copy(i).wait()
            staged = stage_s[i % 2, :, 0:src.shape[1]]
            if dst is None:
                mod = mod + jnp.dot(c_ref[:, r:r + STAGE_ROWS], staged, preferred_element_type=F32)
            else:
                dst[r:r + STAGE_ROWS, :] = staged.astype(BF16)
            if i + 1 == d_model // STAGE_ROWS:
                mod_s[0:c_ref.shape[0], :] = mod

        a = lax.broadcasted_iota(jnp.int32, (QBLK, LANES), 0)
        lane = lax.broadcasted_iota(jnp.int32, (QBLK, LANES), 1)
        band_lo = jnp.where(a >= CHUNK, CHUNK, 0)
        gq_max = jnp.max(jnp.abs(gq_ref[...]), axis=-1, keepdims=True)
        gk_max = jnp.max(jnp.abs(gk_ref[...]), axis=-1, keepdims=True)
        dot_bound = gq_max * gk_max * (HEAD_DIM ** 0.5 * LOG2E * BOUND_MARGIN)
        for h in range(n_heads):
            first = rb_ref[h:h + 1, 0:1] * LOG2E
            row_ext = jnp.concatenate([
                jnp.broadcast_to(first, (1, TILE - REL_CLIP)),
                rb_ref[h:h + 1, 0:REL_CLIP + QBLK] * LOG2E,
                jnp.broadcast_to(first, (1, BIAS_EXT - BANDW))], axis=1)
            row = jnp.broadcast_to(row_ext, (QBLK, BIAS_EXT))
            skew = pltpu.roll(row, 0, 1, stride=1, stride_axis=0)
            reachable = rb_ref[h:h + 1, 0:REL_CLIP + CHUNK]
            bound = dot_bound + jnp.max(reachable, axis=-1, keepdims=True) * LOG2E + 1.0
            for cb in range(BAND_BLOCKS):
                j = lane + cb * LANES
                visible = (j >= band_lo) & (j < band_lo + (N_LEFT_CHUNKS + 1) * CHUNK)
                blk = skew[:, cb * LANES:(cb + 1) * LANES]
                biasx_ref[h, cb] = jnp.where(visible, blk, NEG_INF).T
                bias_ref[h, cb] = jnp.where(visible, blk - bound, NEG_INF).T
            biasx_ref[h, BAND_BLOCKS] = jnp.full((QBLK, LANES), NEG_INF, F32)
            bias_ref[h, BAND_BLOCKS] = jnp.full((QBLK, LANES), NEG_INF, F32)
        for r in range(0, d_model, LANES):
            for c in range(0, attn_w, LANES):
                blk = win_s[r:r + LANES, o_v + c:o_v + c + LANES].astype(F32)
                wvt_s[c:c + LANES, r:r + LANES] = blk.T.astype(BF16)
        gain_s[0] = jnp.broadcast_to(jnp.concatenate([gq_ref[...]] * n_heads, axis=1)
                                     * (HEAD_DIM ** -0.5 * LOG2E), (8, attn_w))
        gain_s[1] = jnp.broadcast_to(jnp.concatenate([gk_ref[...]] * n_heads, axis=1), (8, attn_w))
        k_s[...] = jnp.zeros(k_s.shape, BF16)
        vt_s[...] = jnp.zeros(vt_s.shape, BF16)
        wp_s[...] = jnp.zeros(wp_s.shape, BF16)
        for gi in range(len(POOL_WINDOWS)):
            lo = (gi % 2) * pool_group
            wp_s[gi // 2, lo:lo + pool_group, lo:lo + pool_group] = wpool_ref[gi].astype(BF16)
        u_s[...] = jnp.zeros(u_s.shape, F32)
        t2_s[0:POOL_PAD, :] = jnp.zeros((POOL_PAD, pool_w), F32)
        t4_s[0:POOL_PAD, :] = jnp.zeros((POOL_PAD, pool_w), F32)
        t8_s[0:POOL_PAD, :] = jnp.zeros((POOL_PAD, pool_w), F32)
        for r in range(0, TILE, NORM_ROWS):
            norm_rows(x_ref, 0, r)

    gate = mod_s[pl.ds(seq, 1), 2 * d_model:3 * d_model]

    u_s[POOL_PAD:POOL_OFF, :] = jnp.where(t == 0, 0.0, u_s[TILE + POOL_PAD:TILE + POOL_OFF, :])

    k_s[0:TILE, :] = k_s[TILE:2 * TILE, :]
    vt_s[:, 0:TILE] = vt_s[:, TILE:2 * TILE]

    lane_g = lax.broadcasted_iota(jnp.int32, (SUB_ROWS, GROUP_W), 1)
    lane_h = lax.broadcasted_iota(jnp.int32, (SUB_ROWS, LANES), 1)

    def in_proj(rows, c0, width):
        return jnp.dot(h_s[rows, :], win_s[:, c0:c0 + width], preferred_element_type=F32)

    def head_rms(xf, gain_row):
        for sub in range(0, xf.shape[0], SUB_ROWS):
            x = xf[sub:sub + SUB_ROWS]
            sq = x * x
            ssq = []
            for c0 in range(0, GROUP_W, LANES):
                blk = sq[:, c0:c0 + LANES]
                first = jnp.sum(jnp.where(lane_h < HEAD_DIM, blk, 0.0), axis=-1, keepdims=True)
                both = jnp.sum(blk, axis=-1, keepdims=True)
                ssq.append(jnp.where(lane_h < HEAD_DIM, first, both - first))
            ssq = jnp.concatenate(ssq, axis=1)
            yield sub, x * lax.rsqrt(ssq * (1.0 / HEAD_DIM) + EPS) * gain_row

    tile_rows = slice(0, TILE)
    qf = in_proj(tile_rows, 0, attn_w)
    for rb in range(0, TILE, QBLK):
        for g in range(n_groups):
            cols = slice(g * GROUP_W, (g + 1) * GROUP_W)
            gain = gain_s[0, 0:1, cols]
            for sub, qn in head_rms(qf[rb:rb + QBLK, cols], gain):
                for h in range(HEADS_PER_GROUP):
                    keep = (lane_g >= h * HEAD_DIM) & (lane_g < (h + 1) * HEAD_DIM)
                    qbd_s[rb // QBLK, g, h * QBLK + sub:h * QBLK + sub + SUB_ROWS, :] = (
                        jnp.where(keep, qn, 0.0).astype(BF16))
    kf = in_proj(tile_rows, o_k, attn_w)
    for rb in range(0, TILE, QBLK):
        for g in range(n_groups):
            cols = slice(g * GROUP_W, (g + 1) * GROUP_W)
            for sub, kn in head_rms(kf[rb:rb + QBLK, cols], gain_s[1, 0:1, cols]):
                row = TILE + rb + sub
                k_s[row:row + SUB_ROWS, cols] = kn.astype(BF16)
    vt_s[:, TILE:2 * TILE] = lax.dot_general(
        wvt_s[...], h_s[...], (((1,), (1,)), ((), ())), preferred_element_type=F32).astype(BF16)
    u_s[POOL_OFF:POOL_OFF + TILE, :] = in_proj(tile_rows, o_u, pool_w)

    levels = ((u_s, t2_s, 1), (t2_s, t4_s, 2), (t4_s, t8_s, 4), (t8_s, None, 8))
    for gi in range(len(POOL_WINDOWS)):
        assert POOL_WINDOWS[gi] == 2 ** (gi + 1)

    def pool_window_sums():
        step = 88
        for li, (src, dst, sh) in enumerate(levels[:-1]):
            lanes = slice((li + 1) * pool_group, pool_w)
            for r in range(POOL_PAD, pool_rows, step):
                dst[r:r + step, lanes] = src[r:r + step, lanes] + src[r - sh:r - sh + step, lanes]

    def pool_block(jb):
        tok = t * TILE + lax.broadcasted_iota(jnp.int32, (QBLK, pool_group), 0)
        r = jb * QBLK
        for pair in range(len(POOL_WINDOWS) // 2):
            mixed = []
            for gi in (2 * pair, 2 * pair + 1):
                lanes = slice(gi * pool_group, (gi + 1) * pool_group)
                src = levels[gi][0]
                sh = levels[gi][2]
                rows = slice(POOL_OFF + r, POOL_OFF + r + QBLK)
                wsum = src[rows, lanes] + src[POOL_OFF + r - sh:POOL_OFF + r - sh + QBLK, lanes]
                cnt = jnp.minimum(tok + (r + 1), POOL_WINDOWS[gi]).astype(F32)
                mixed.append((wsum / cnt - u_s[rows, lanes]).astype(BF16))
            pm = jnp.concatenate(mixed, axis=1)
            c0 = pair * 2 * pool_group
            po = jnp.dot(pm, wp_s[pair], preferred_element_type=F32) * ps_ref[:, c0:c0 + 2 * pool_group]
            gz = z_s[r:r + QBLK, attn_w + c0:attn_w + c0 + 2 * pool_group].astype(F32)
            y_s[r:r + QBLK, attn_w + c0:attn_w + c0 + 2 * pool_group] = (po * gz).astype(BF16)

    row_q = lax.broadcasted_iota(jnp.int32, (QBLK, LANES), 0)
    n_qblk = TILE // QBLK

    def band_blocks(jb):
        n_before = TILE // LANES - jb
        return [jnp.where(t == 0, BAND_BLOCKS, cb) if cb < n_before else cb for cb in range(BAND_BLOCKS)]

    def score_dot(jb, g):
        qrow = jb * QBLK
        return lax.dot_general(k_s[qrow:qrow + BANDW, g * GROUP_W:(g + 1) * GROUP_W], qbd_s[jb, g],
                               (((1,), (1,)), ((), ())), preferred_element_type=F32)

    def probs(jb):
        buf = jb % 2
        blk = band_blocks(jb)
        for g in range(n_groups):
            st = score_dot(jb, g)
            for h in range(HEADS_PER_GROUP):
                hh = g * HEADS_PER_GROUP + h
                lanes = slice(h * QBLK, (h + 1) * QBLK)
                for cb in range(BAND_BLOCKS):
                    for sub in range(0, LANES, SUB_ROWS):
                        rows = slice(cb * LANES + sub, cb * LANES + sub + SUB_ROWS)
                        p = jnp.exp2(st[rows, lanes] + bias_ref[hh, blk[cb], sub:sub + SUB_ROWS, :])
                        p_s[buf, hh // 2, rows, (hh % 2) * QBLK:(hh % 2 + 1) * QBLK] = p.astype(BF16)

    def probs_exact(jb):
        buf = jb % 2
        blk = band_blocks(jb)
        for g in range(n_groups):
            st = score_dot(jb, g)
            for h in range(HEADS_PER_GROUP):
                hh = g * HEADS_PER_GROUP + h
                lanes = slice(h * QBLK, (h + 1) * QBLK)
                macc = None
                for cb in range(BAND_BLOCKS):
                    for sub in range(0, LANES, SUB_ROWS):
                        rows = slice(cb * LANES + sub, cb * LANES + sub + SUB_ROWS)
                        sb = st[rows, lanes] + biasx_ref[hh, blk[cb], sub:sub + SUB_ROWS, :]
                        s_s[hh, rows, :] = sb
                        for r8 in range(0, SUB_ROWS, 8):
                            macc = sb[r8:r8 + 8, :] if macc is None else jnp.maximum(macc, sb[r8:r8 + 8, :])
                m = jnp.max(macc, axis=0, keepdims=True)
                for r in range(0, BANDW, SUB_ROWS):
                    rows = slice(r, r + SUB_ROWS)
                    p = jnp.exp2(s_s[hh, rows, :] - m)
                    p_s[buf, hh // 2, rows, (hh % 2) * QBLK:(hh % 2 + 1) * QBLK] = p.astype(BF16)

    ones_rows = jnp.ones((ONES_ROWS, BANDW), BF16)

    def attend(jb, anchor=None):
        buf = jb % 2
        qrow = jb * QBLK
        lmin = None
        for pair in range(n_heads // 2):
            ha = 2 * pair
            vt1 = jnp.concatenate(
                [vt_s[ha * HEAD_DIM:(ha + 2) * HEAD_DIM, qrow:qrow + BANDW], ones_rows], axis=0)
            ot = jnp.dot(vt1, p_s[buf, pair], preferred_element_type=F32)
            l = ot[2 * HEAD_DIM:2 * HEAD_DIM + 1, :]
            lmin = l if lmin is None else jnp.minimum(lmin, l)
            linv = 1.0 / l
            if anchor is not None:
                linv = jnp.concatenate([jnp.where(never, anchor(pair), linv[:, 0:LANES]), linv[:, LANES:]], axis=1)
            oa = ot[0:2 * HEAD_DIM, 0:QBLK] * linv[:, 0:QBLK]
            ob = ot[0:2 * HEAD_DIM, QBLK:2 * QBLK] * linv[:, QBLK:2 * QBLK]
            a = jnp.where(row_q < HEAD_DIM, oa, ob).T
            c0 = ha * HEAD_DIM
            gz = z_s[qrow:qrow + QBLK, c0:c0 + LANES].astype(F32)
            y_s[qrow:qrow + QBLK, c0:c0 + LANES] = (a * gz).astype(BF16)
        return jnp.min(lmin)

    never = t < 0
    norm_steps = list(range(0, TILE, NORM_ROWS))
    per_pair = len(norm_steps) // (n_qblk * (n_heads // 2))

    def next_norm(jb):
        def anchor(pair):
            i0 = (jb * (n_heads // 2) + pair) * per_pair
            return sum(norm_rows(xn_ref, seq_next, rn) for rn in norm_steps[i0:i0 + per_pair])
        return anchor

    pool_window_sums()
    for c0 in range(0, attn_w + pool_w, 512):
        z_s[:, c0:c0 + 512] = _silu(in_proj(tile_rows, o_z + c0, 512)).astype(BF16)
    probs(0)
    underflowed = []
    for jb in range(n_qblk):
        if jb + 1 < n_qblk:
            probs(jb + 1)
        pool_block(jb)
        underflowed.append(jnp.logical_not(attend(jb, next_norm(jb)) >= L_FLOOR))

    o = jnp.dot(y_s[...], wout_s[...], preferred_element_type=F32)
    for c0 in range(0, d_model, GROUP_W):
        cols = slice(c0, c0 + GROUP_W)
        for r in range(0, TILE, SUB_ROWS):
            o_ref[r:r + SUB_ROWS, cols] = x_ref[r:r + SUB_ROWS, cols] + gate[:, cols] * o[r:r + SUB_ROWS, cols]

    @pl.when(functools.reduce(jnp.logical_or, underflowed))
    def _():
        for jb in range(n_qblk):
            @pl.when(underflowed[jb])
            def _(jb=jb):
                probs_exact(jb)
                attend(jb)
                rows = slice(jb * QBLK, (jb + 1) * QBLK)
                o_jb = jnp.dot(y_s[rows, :], wout_s[...], preferred_element_type=F32)
                o_ref[rows, :] = x_ref[rows, :] + gate * o_jb


def _layer(x, c, w_ada, b_ada, norm_g, w_in, q_norm_g, k_norm_g, rel_bias, w_pool, pool_scale, w_out):
    bsz, seq, d_model = x.shape
    n_heads = rel_bias.shape[0]
    attn_w = n_heads * HEAD_DIM
    pool_w = w_pool.shape[0] * w_pool.shape[1]
    pool_group = w_pool.shape[1]
    in_w = w_in.shape[1]
    assert seq % TILE == 0 and attn_w % GROUP_W == 0 and TILE == N_LEFT_CHUNKS * CHUNK
    assert in_w == 3 * attn_w + pool_w + attn_w + pool_w and 2 * pool_group == GROUP_W
    assert attn_w == 512 and pool_w == 512 and bsz <= 8
    n_groups = attn_w // GROUP_W

    n_pairs = w_pool.shape[0] // 2
    assert QBLK <= REL_CLIP <= TILE and rel_bias.shape[1] == 2 * REL_CLIP + 1

    tiles_per_seq = seq // TILE

    def next_tile(b, t):
        n = jnp.minimum(b * tiles_per_seq + t + 1, bsz * tiles_per_seq - 1)
        return n // tiles_per_seq, n % tiles_per_seq

    const2 = lambda b, t: (0, 0)
    const3 = lambda b, t: (0, 0, 0)
    kernel = functools.partial(_block_kernel, d_model=d_model, attn_w=attn_w, pool_w=pool_w,
                               tiles_per_seq=tiles_per_seq, n_tiles=bsz * tiles_per_seq)
    return pl.pallas_call(
        kernel,
        grid=(bsz, seq // TILE),
        in_specs=[
            pl.BlockSpec((None, TILE, d_model), lambda b, t: (b, t, 0)),
            pl.BlockSpec((None, TILE, d_model), lambda b, t: (*next_tile(b, t), 0)),
            pl.BlockSpec((bsz, d_model), const2),
            pl.BlockSpec(memory_space=pl.ANY),
            pl.BlockSpec((1, 3 * d_model), const2),
            pl.BlockSpec((1, d_model), const2),
            pl.BlockSpec(memory_space=pl.ANY),
            pl.BlockSpec(memory_space=pl.ANY),
            pl.BlockSpec((1, HEAD_DIM), const2),
            pl.BlockSpec((1, HEAD_DIM), const2),
            pl.BlockSpec((n_heads, 2 * REL_CLIP + 1), const2),
            pl.BlockSpec((2 * n_pairs, pool_group, pool_group), const3),
            pl.BlockSpec((1, pool_w), const2),
        ],
        out_specs=pl.BlockSpec((None, TILE, d_model), lambda b, t: (b, t, 0)),
        out_shape=jax.ShapeDtypeStruct(x.shape, x.dtype),
        scratch_shapes=[
            pltpu.VMEM((TILE, d_model), BF16),
            pltpu.VMEM((TILE, attn_w + pool_w), BF16),
            pltpu.VMEM((TILE // QBLK, n_groups, HEADS_PER_GROUP * QBLK, GROUP_W), BF16),
            pltpu.VMEM((2 * TILE, attn_w), BF16),
            pltpu.VMEM((attn_w, 2 * TILE), BF16),
            pltpu.VMEM((TILE + POOL_OFF, pool_w), F32),
            pltpu.VMEM((TILE + POOL_OFF, pool_w), F32),
            pltpu.VMEM((TILE + POOL_OFF, pool_w), F32),
            pltpu.VMEM((TILE + POOL_OFF, pool_w), F32),
            pltpu.VMEM((n_heads, BANDW, QBLK), F32),
            pltpu.VMEM((2, n_heads // 2, BANDW, 2 * QBLK), BF16),
            pltpu.VMEM((TILE, attn_w + pool_w), BF16),
            pltpu.VMEM((n_heads, BAND_BLOCKS + 1, QBLK, LANES), F32),
            pltpu.VMEM((n_heads, BAND_BLOCKS + 1, QBLK, LANES), F32),
            pltpu.VMEM((attn_w, d_model), BF16),
            pltpu.VMEM((2, 8, attn_w), F32),
            pltpu.VMEM((n_pairs, GROUP_W, GROUP_W), BF16),
            pltpu.VMEM((d_model, in_w), BF16),
            pltpu.VMEM((attn_w + pool_w, d_model), BF16),
            pltpu.VMEM((8, 3 * d_model), F32),
            pltpu.VMEM((2, STAGE_ROWS, in_w), F32),
            pltpu.SemaphoreType.DMA((2,)),
        ],
        compiler_params=pltpu.CompilerParams(
            dimension_semantics=("arbitrary", "arbitrary"),
            vmem_limit_bytes=V7X_VMEM_LIMIT_BYTES,
        ),
        name="hybrid_block",
    )(x, x, c, w_ada, b_ada.reshape(1, 3 * d_model), norm_g.reshape(1, d_model), w_in, w_out,
      q_norm_g.reshape(1, HEAD_DIM), k_norm_g.reshape(1, HEAD_DIM), rel_bias, w_pool,
      pool_scale.reshape(1, pool_w))


def kernel(x, c, norm_g, w_ada, b_ada, w_in, q_norm_g, k_norm_g, rel_bias, w_pool, pool_scale, w_out):
    depth = w_in.shape[0]
    for l in range(depth):
        x = _layer(x, c, w_ada[l], b_ada[l], norm_g[l], w_in[l], q_norm_g[l], k_norm_g[l], rel_bias[l],
                   w_pool[l], pool_scale[l], w_out[l])
    return x
```

```python
import functools
import math

import jax
import jax.numpy as jnp
from jax import lax
from jax.experimental import pallas as pl
from jax.experimental.pallas import tpu as pltpu

CHUNK = 64
N_LEFT_CHUNKS = 8
HEAD_DIM = 64
REL_CLIP = 256
POOL_WINDOWS = (2, 4, 8, 16)
EPS = 1e-6
NEG_INF = -1e30
LOG2E = math.log2(math.e)

LANES = 128
TILE = 512
QBLK = 2 * CHUNK
BANDW = TILE + QBLK
BAND_BLOCKS = BANDW // LANES
BIAS_EXT = BANDW + QBLK
HEADS_PER_GROUP = 4
GROUP_W = HEADS_PER_GROUP * HEAD_DIM
POOL_PAD = 8
POOL_HIST = 16
POOL_OFF = POOL_PAD + POOL_HIST
ONES_ROWS = 16
SUB_ROWS = 32
NORM_ROWS = 16
BOUND_MARGIN = 1.02
L_FLOOR = 2.0 ** -90
STAGE_ROWS = 256
V7X_VMEM_LIMIT_BYTES = 56 * 1024 * 1024

F32 = jnp.float32
BF16 = jnp.bfloat16


def _silu(z):
    return z * (1.0 / (1.0 + jnp.exp(-z)))


def _block_kernel(x_ref, xn_ref, c_ref, wada_ref, bada_ref, g_ref, win_ref, wout_ref, gq_ref, gk_ref, rb_ref, wpool_ref,
                  ps_ref, o_ref,
                  h_s, z_s, qbd_s, k_s, vt_s, u_s, t2_s, t4_s, t8_s, s_s, p_s, y_s, bias_ref, biasx_ref,
                  wvt_s, gain_s, wp_s, win_s, wout_s, mod_s, stage_s, stage_sem,
                  *, d_model, attn_w, pool_w, tiles_per_seq, n_tiles):
    t = pl.program_id(1)
    n_heads = attn_w // HEAD_DIM
    n_groups = attn_w // GROUP_W
    o_k, o_v, o_u, o_z = attn_w, 2 * attn_w, 3 * attn_w, 3 * attn_w + pool_w
    pool_group = pool_w // len(POOL_WINDOWS)
    pool_rows = TILE + POOL_OFF

    seq = pl.program_id(0)
    seq_next = jnp.minimum(seq * tiles_per_seq + t + 1, n_tiles - 1) // tiles_per_seq

    def norm_rows(src_ref, which_seq, r):
        shift = mod_s[pl.ds(which_seq, 1), 0:d_model]
        scale = mod_s[pl.ds(which_seq, 1), d_model:2 * d_model]
        xc = src_ref[r:r + NORM_ROWS, :]
        ms = jnp.mean(xc * xc, axis=-1, keepdims=True)
        hc = xc * lax.rsqrt(ms + EPS) * (g_ref[...] * (1.0 + scale)) + shift
        h_s[r:r + NORM_ROWS, :] = hc.astype(BF16)
        return hc[0:1, 0:LANES]

    @pl.when((seq == 0) & (t == 0))
    def _():
        chunks = ([(wada_ref, None, r) for r in range(0, d_model, STAGE_ROWS)]
                  + [(win_ref, win_s, r) for r in range(0, win_s.shape[0], STAGE_ROWS)]
                  + [(wout_ref, wout_s, r) for r in range(0, wout_s.shape[0], STAGE_ROWS)])

        def chunk_copy(i):
            src = chunks[i][0]
            return pltpu.make_async_copy(src.at[pl.ds(chunks[i][2], STAGE_ROWS), :],
                                         stage_s.at[i % 2, :, pl.ds(0, src.shape[1])], stage_sem.at[i % 2])

        mod = bada_ref[...]
        chunk_copy(0).start()
        for i, (src, dst, r) in enumerate(chunks):
            if i + 1 < len(chunks):
                chunk_copy(i + 1).start()
            chunk_copy(i).wait()
            staged = stage_s[i % 2, :, 0:src.shape[1]]
            if dst is None:
                mod = mod + jnp.dot(c_ref[:, r:r + STAGE_ROWS], staged, preferred_element_type=F32)
            else:
                dst[r:r + STAGE_ROWS, :] = staged.astype(BF16)
            if i + 1 == d_model // STAGE_ROWS:
                mod_s[0:c_ref.shape[0], :] = mod

        a = lax.broadcasted_iota(jnp.int32, (QBLK, LANES), 0)
        lane = lax.broadcasted_iota(jnp.int32, (QBLK, LANES), 1)
        band_lo = jnp.where(a >= CHUNK, CHUNK, 0)
        gq_max = jnp.max(jnp.abs(gq_ref[...]), axis=-1, keepdims=True)
        gk_max = jnp.max(jnp.abs(gk_ref[...]), axis=-1, keepdims=True)
        dot_bound = gq_max * gk_max * (HEAD_DIM ** 0.5 * LOG2E * BOUND_MARGIN)
        for h in range(n_heads):
            first = rb_ref[h:h + 1, 0:1] * LOG2E
            row_ext = jnp.concatenate([
                jnp.broadcast_to(first, (1, TILE - REL_CLIP)),
                rb_ref[h:h + 1, 0:REL_CLIP + QBLK] * LOG2E,
                jnp.broadcast_to(first, (1, BIAS_EXT - BANDW))], axis=1)
            row = jnp.broadcast_to(row_ext, (QBLK, BIAS_EXT))
            skew = pltpu.roll(row, 0, 1, stride=1, stride_axis=0)
            reachable = rb_ref[h:h + 1, 0:REL_CLIP + CHUNK]
            bound = dot_bound + jnp.max(reachable, axis=-1, keepdims=True) * LOG2E + 1.0
            for cb in range(BAND_BLOCKS):
                j = lane + cb * LANES
                visible = (j >= band_lo) & (j < band_lo + (N_LEFT_CHUNKS + 1) * CHUNK)
                blk = skew[:, cb * LANES:(cb + 1) * LANES]
                biasx_ref[h, cb] = jnp.where(visible, blk, NEG_INF).T
                bias_ref[h, cb] = jnp.where(visible, blk - bound, NEG_INF).T
            biasx_ref[h, BAND_BLOCKS] = jnp.full((QBLK, LANES), NEG_INF, F32)
            bias_ref[h, BAND_BLOCKS] = jnp.full((QBLK, LANES), NEG_INF, F32)
        for r in range(0, d_model, LANES):
            for c in range(0, attn_w, LANES):
                blk = win_s[r:r + LANES, o_v + c:o_v + c + LANES].astype(F32)
                wvt_s[c:c + LANES, r:r + LANES] = blk.T.astype(BF16)
        gain_s[0] = jnp.broadcast_to(jnp.concatenate([gq_ref[...]] * n_heads, axis=1)
                                     * (HEAD_DIM ** -0.5 * LOG2E), (8, attn_w))
        gain_s[1] = jnp.broadcast_to(jnp.concatenate([gk_ref[...]] * n_heads, axis=1), (8, attn_w))
        k_s[...] = jnp.zeros(k_s.shape, BF16)
        vt_s[...] = jnp.zeros(vt_s.shape, BF16)
        wp_s[...] = jnp.zeros(wp_s.shape, BF16)
        for gi in range(len(POOL_WINDOWS)):
            lo = (gi % 2) * pool_group
            wp_s[gi // 2, lo:lo + pool_group, lo:lo + pool_group] = wpool_ref[gi].astype(BF16)
        u_s[...] = jnp.zeros(u_s.shape, F32)
        t2_s[0:POOL_PAD, :] = jnp.zeros((POOL_PAD, pool_w), F32)
        t4_s[0:POOL_PAD, :] = jnp.zeros((POOL_PAD, pool_w), F32)
        t8_s[0:POOL_PAD, :] = jnp.zeros((POOL_PAD, pool_w), F32)
        for r in range(0, TILE, NORM_ROWS):
            norm_rows(x_ref, 0, r)

    gate = mod_s[pl.ds(seq, 1), 2 * d_model:3 * d_model]

    u_s[POOL_PAD:POOL_OFF, :] = jnp.where(t == 0, 0.0, u_s[TILE + POOL_PAD:TILE + POOL_OFF, :])

    k_s[0:TILE, :] = k_s[TILE:2 * TILE, :]
    vt_s[:, 0:TILE] = vt_s[:, TILE:2 * TILE]

    lane_g = lax.broadcasted_iota(jnp.int32, (SUB_ROWS, GROUP_W), 1)
    lane_h = lax.broadcasted_iota(jnp.int32, (SUB_ROWS, LANES), 1)

    def in_proj(rows, c0, width):
        return jnp.dot(h_s[rows, :], win_s[:, c0:c0 + width], preferred_element_type=F32)

    def head_rms(xf, gain_row):
        for sub in range(0, xf.shape[0], SUB_ROWS):
            x = xf[sub:sub + SUB_ROWS]
            sq = x * x
            ssq = []
            for c0 in range(0, GROUP_W, LANES):
                blk = sq[:, c0:c0 + LANES]
                first = jnp.sum(jnp.where(lane_h < HEAD_DIM, blk, 0.0), axis=-1, keepdims=True)
                both = jnp.sum(blk, axis=-1, keepdims=True)
                ssq.append(jnp.where(lane_h < HEAD_DIM, first, both - first))
            ssq = jnp.concatenate(ssq, axis=1)
            yield sub, x * lax.rsqrt(ssq * (1.0 / HEAD_DIM) + EPS) * gain_row

    tile_rows = slice(0, TILE)
    qf = in_proj(tile_rows, 0, attn_w)
    for rb in range(0, TILE, QBLK):
        for g in range(n_groups):
            cols = slice(g * GROUP_W, (g + 1) * GROUP_W)
            gain = gain_s[0, 0:1, cols]
            for sub, qn in head_rms(qf[rb:rb + QBLK, cols], gain):
                for h in range(HEADS_PER_GROUP):
                    keep = (lane_g >= h * HEAD_DIM) & (lane_g < (h + 1) * HEAD_DIM)
                    qbd_s[rb // QBLK, g, h * QBLK + sub:h * QBLK + sub + SUB_ROWS, :] = (
                        jnp.where(keep, qn, 0.0).astype(BF16))
    kf = in_proj(tile_rows, o_k, attn_w)
    for rb in range(0, TILE, QBLK):
        for g in range(n_groups):
            cols = slice(g * GROUP_W, (g + 1) * GROUP_W)
            for sub, kn in head_rms(kf[rb:rb + QBLK, cols], gain_s[1, 0:1, cols]):
                row = TILE + rb + sub
                k_s[row:row + SUB_ROWS, cols] = kn.astype(BF16)
    vt_s[:, TILE:2 * TILE] = lax.dot_general(
        wvt_s[...], h_s[...], (((1,), (1,)), ((), ())), preferred_element_type=F32).astype(BF16)
    u_s[POOL_OFF:POOL_OFF + TILE, :] = in_proj(tile_rows, o_u, pool_w)
    for c0 in range(0, attn_w + pool_w, 512):
        z_s[:, c0:c0 + 512] = _silu(in_proj(tile_rows, o_z + c0, 512)).astype(BF16)

    def pool():
        step = 176
        levels = ((u_s, t2_s, 1), (t2_s, t4_s, 2), (t4_s, t8_s, 4), (t8_s, None, 8))
        for gi in range(len(POOL_WINDOWS)):
            assert POOL_WINDOWS[gi] == 2 ** (gi + 1)
        for li, (src, dst, sh) in enumerate(levels[:-1]):
            lanes = slice((li + 1) * pool_group, pool_w)
            for r in range(POOL_PAD, pool_rows, step):
                dst[r:r + step, lanes] = src[r:r + step, lanes] + src[r - sh:r - sh + step, lanes]
        tok = t * TILE + lax.broadcasted_iota(jnp.int32, (QBLK, pool_group), 0)
        for r in range(0, TILE, QBLK):
            for pair in range(len(POOL_WINDOWS) // 2):
                mixed = []
                for gi in (2 * pair, 2 * pair + 1):
                    lanes = slice(gi * pool_group, (gi + 1) * pool_group)
                    src = levels[gi][0]
                    sh = levels[gi][2]
                    rows = slice(POOL_OFF + r, POOL_OFF + r + QBLK)
                    wsum = src[rows, lanes] + src[POOL_OFF + r - sh:POOL_OFF + r - sh + QBLK, lanes]
                    cnt = jnp.minimum(tok + (r + 1), POOL_WINDOWS[gi]).astype(F32)
                    mixed.append((wsum / cnt - u_s[rows, lanes]).astype(BF16))
                pm = jnp.concatenate(mixed, axis=1)
                c0 = pair * 2 * pool_group
                po = jnp.dot(pm, wp_s[pair], preferred_element_type=F32) * ps_ref[:, c0:c0 + 2 * pool_group]
                gz = z_s[r:r + QBLK, attn_w + c0:attn_w + c0 + 2 * pool_group].astype(F32)
                y_s[r:r + QBLK, attn_w + c0:attn_w + c0 + 2 * pool_group] = (po * gz).astype(BF16)

    row_q = lax.broadcasted_iota(jnp.int32, (QBLK, LANES), 0)
    n_qblk = TILE // QBLK

    def band_blocks(jb):
        n_before = TILE // LANES - jb
        return [jnp.where(t == 0, BAND_BLOCKS, cb) if cb < n_before else cb for cb in range(BAND_BLOCKS)]

    def score_dot(jb, g):
        qrow = jb * QBLK
        return lax.dot_general(k_s[qrow:qrow + BANDW, g * GROUP_W:(g + 1) * GROUP_W], qbd_s[jb, g],
                               (((1,), (1,)), ((), ())), preferred_element_type=F32)

    def probs(jb):
        buf = jb % 2
        blk = band_blocks(jb)
        for g in range(n_groups):
            st = score_dot(jb, g)
            for h in range(HEADS_PER_GROUP):
                hh = g * HEADS_PER_GROUP + h
                lanes = slice(h * QBLK, (h + 1) * QBLK)
                for cb in range(BAND_BLOCKS):
                    for sub in range(0, LANES, SUB_ROWS):
                        rows = slice(cb * LANES + sub, cb * LANES + sub + SUB_ROWS)
                        p = jnp.exp2(st[rows, lanes] + bias_ref[hh, blk[cb], sub:sub + SUB_ROWS, :])
                        p_s[buf, hh // 2, rows, (hh % 2) * QBLK:(hh % 2 + 1) * QBLK] = p.astype(BF16)

    def probs_exact(jb):
        buf = jb % 2
        blk = band_blocks(jb)
        for g in range(n_groups):
            st = score_dot(jb, g)
            for h in range(HEADS_PER_GROUP):
                hh = g * HEADS_PER_GROUP + h
                lanes = slice(h * QBLK, (h + 1) * QBLK)
                macc = None
                for cb in range(BAND_BLOCKS):
                    for sub in range(0, LANES, SUB_ROWS):
                        rows = slice(cb * LANES + sub, cb * LANES + sub + SUB_ROWS)
                        sb = st[rows, lanes] + biasx_ref[hh, blk[cb], sub:sub + SUB_ROWS, :]
                        s_s[hh, rows, :] = sb
                        for r8 in range(0, SUB_ROWS, 8):
                            macc = sb[r8:r8 + 8, :] if macc is None else jnp.maximum(macc, sb[r8:r8 + 8, :])
                m = jnp.max(macc, axis=0, keepdims=True)
                for r in range(0, BANDW, SUB_ROWS):
                    rows = slice(r, r + SUB_ROWS)
                    p = jnp.exp2(s_s[hh, rows, :] - m)
                    p_s[buf, hh // 2, rows, (hh % 2) * QBLK:(hh % 2 + 1) * QBLK] = p.astype(BF16)

    ones_rows = jnp.ones((ONES_ROWS, BANDW), BF16)

    def attend(jb, anchor=None):
        buf = jb % 2
        qrow = jb * QBLK
        lmin = None
        for pair in range(n_heads // 2):
            ha = 2 * pair
            vt1 = jnp.concatenate(
                [vt_s[ha * HEAD_DIM:(ha + 2) * HEAD_DIM, qrow:qrow + BANDW], ones_rows], axis=0)
            ot = jnp.dot(vt1, p_s[buf, pair], preferred_element_type=F32)
            l = ot[2 * HEAD_DIM:2 * HEAD_DIM + 1, :]
            lmin = l if lmin is None else jnp.minimum(lmin, l)
            linv = 1.0 / l
            if anchor is not None:
                linv = jnp.concatenate([jnp.where(never, anchor(pair), linv[:, 0:LANES]), linv[:, LANES:]], axis=1)
            oa = ot[0:2 * HEAD_DIM, 0:QBLK] * linv[:, 0:QBLK]
            ob = ot[0:2 * HEAD_DIM, QBLK:2 * QBLK] * linv[:, QBLK:2 * QBLK]
            a = jnp.where(row_q < HEAD_DIM, oa, ob).T
            c0 = ha * HEAD_DIM
            gz = z_s[qrow:qrow + QBLK, c0:c0 + LANES].astype(F32)
            y_s[qrow:qrow + QBLK, c0:c0 + LANES] = (a * gz).astype(BF16)
        return jnp.min(lmin)

    never = t < 0
    norm_steps = list(range(0, TILE, NORM_ROWS))
    per_pair = len(norm_steps) // (n_qblk * (n_heads // 2))

    def next_norm(jb):
        def anchor(pair):
            i0 = (jb * (n_heads // 2) + pair) * per_pair
            return sum(norm_rows(xn_ref, seq_next, rn) for rn in norm_steps[i0:i0 + per_pair])
        return anchor

    pool()
    probs(0)
    underflowed = []
    for jb in range(n_qblk):
        if jb + 1 < n_qblk:
            probs(jb + 1)
        underflowed.append(jnp.logical_not(attend(jb, next_norm(jb)) >= L_FLOOR))

    o = jnp.dot(y_s[...], wout_s[...], preferred_element_type=F32)
    for c0 in range(0, d_model, GROUP_W):
        cols = slice(c0, c0 + GROUP_W)
        for r in range(0, TILE, SUB_ROWS):
            o_ref[r:r + SUB_ROWS, cols] = x_ref[r:r + SUB_ROWS, cols] + gate[:, cols] * o[r:r + SUB_ROWS, cols]

    @pl.when(functools.reduce(jnp.logical_or, underflowed))
    def _():
        for jb in range(n_qblk):
            @pl.when(underflowed[jb])
            def _(jb=jb):
                probs_exact(jb)
                attend(jb)
                rows = slice(jb * QBLK, (jb + 1) * QBLK)
                o_jb = jnp.dot(y_s[rows, :], wout_s[...], preferred_element_type=F32)
                o_ref[rows, :] = x_ref[rows, :] + gate * o_jb


def _layer(x, c, w_ada, b_ada, norm_g, w_in, q_norm_g, k_norm_g, rel_bias, w_pool, pool_scale, w_out):
    bsz, seq, d_model = x.shape
    n_heads = rel_bias.shape[0]
    attn_w = n_heads * HEAD_DIM
    pool_w = w_pool.shape[0] * w_pool.shape[1]
    pool_group = w_pool.shape[1]
    in_w = w_in.shape[1]
    assert seq % TILE == 0 and attn_w % GROUP_W == 0 and TILE == N_LEFT_CHUNKS * CHUNK
    assert in_w == 3 * attn_w + pool_w + attn_w + pool_w and 2 * pool_group == GROUP_W
    assert attn_w == 512 and pool_w == 512 and bsz <= 8
    n_groups = attn_w // GROUP_W

    n_pairs = w_pool.shape[0] // 2
    assert QBLK <= REL_CLIP <= TILE and rel_bias.shape[1] == 2 * REL_CLIP + 1

    tiles_per_seq = seq // TILE

    def next_tile(b, t):
        n = jnp.minimum(b * tiles_per_seq + t + 1, bsz * tiles_per_seq - 1)
        return n // tiles_per_seq, n % tiles_per_seq

    const2 = lambda b, t: (0, 0)
    const3 = lambda b, t: (0, 0, 0)
    kernel = functools.partial(_block_kernel, d_model=d_model, attn_w=attn_w, pool_w=pool_w,
                               tiles_per_seq=tiles_per_seq, n_tiles=bsz * tiles_per_seq)
    return pl.pallas_call(
        kernel,
        grid=(bsz, seq // TILE),
        in_specs=[
            pl.BlockSpec((None, TILE, d_model), lambda b, t: (b, t, 0)),
            pl.BlockSpec((None, TILE, d_model), lambda b, t: (*next_tile(b, t), 0)),
            pl.BlockSpec((bsz, d_model), const2),
            pl.BlockSpec(memory_space=pl.ANY),
            pl.BlockSpec((1, 3 * d_model), const2),
            pl.BlockSpec((1, d_model), const2),
            pl.BlockSpec(memory_space=pl.ANY),
            pl.BlockSpec(memory_space=pl.ANY),
            pl.BlockSpec((1, HEAD_DIM), const2),
            pl.BlockSpec((1, HEAD_DIM), const2),
            pl.BlockSpec((n_heads, 2 * REL_CLIP + 1), const2),
            pl.BlockSpec((2 * n_pairs, pool_group, pool_group), const3),
            pl.BlockSpec((1, pool_w), const2),
        ],
        out_specs=pl.BlockSpec((None, TILE, d_model), lambda b, t: (b, t, 0)),
        out_shape=jax.ShapeDtypeStruct(x.shape, x.dtype),
        scratch_shapes=[
            pltpu.VMEM((TILE, d_model), BF16),
            pltpu.VMEM((TILE, attn_w + pool_w), BF16),
            pltpu.VMEM((TILE // QBLK, n_groups, HEADS_PER_GROUP * QBLK, GROUP_W), BF16),
            pltpu.VMEM((2 * TILE, attn_w), BF16),
            pltpu.VMEM((attn_w, 2 * TILE), BF16),
            pltpu.VMEM((TILE + POOL_OFF, pool_w), F32),
            pltpu.VMEM((TILE + POOL_OFF, pool_w), F32),
            pltpu.VMEM((TILE + POOL_OFF, pool_w), F32),
            pltpu.VMEM((TILE + POOL_OFF, pool_w), F32),
            pltpu.VMEM((n_heads, BANDW, QBLK), F32),
            pltpu.VMEM((2, n_heads // 2, BANDW, 2 * QBLK), BF16),
            pltpu.VMEM((TILE, attn_w + pool_w), BF16),
            pltpu.VMEM((n_heads, BAND_BLOCKS + 1, QBLK, LANES), F32),
            pltpu.VMEM((n_heads, BAND_BLOCKS + 1, QBLK, LANES), F32),
            pltpu.VMEM((attn_w, d_model), BF16),
            pltpu.VMEM((2, 8, attn_w), F32),
            pltpu.VMEM((n_pairs, GROUP_W, GROUP_W), BF16),
            pltpu.VMEM((d_model, in_w), BF16),
            pltpu.VMEM((attn_w + pool_w, d_model), BF16),
            pltpu.VMEM((8, 3 * d_model), F32),
            pltpu.VMEM((2, STAGE_ROWS, in_w), F32),
            pltpu.SemaphoreType.DMA((2,)),
        ],
        compiler_params=pltpu.CompilerParams(
            dimension_semantics=("arbitrary", "arbitrary"),
            vmem_limit_bytes=V7X_VMEM_LIMIT_BYTES,
        ),
        name="hybrid_block",
    )(x, x, c, w_ada, b_ada.reshape(1, 3 * d_model), norm_g.reshape(1, d_model), w_in, w_out,
      q_norm_g.reshape(1, HEAD_DIM), k_norm_g.reshape(1, HEAD_DIM), rel_bias, w_pool,
      pool_scale.reshape(1, pool_w))


def kernel(x, c, norm_g, w_ada, b_ada, w_in, q_norm_g, k_norm_g, rel_bias, w_pool, pool_scale, w_out):
    depth = w_in.shape[0]
    for l in range(depth):
        x = _layer(x, c, w_ada[l], b_ada[l], norm_g[l], w_in[l], q_norm_g[l], k_norm_g[l], rel_bias[l],
                   w_pool[l], pool_scale[l], w_out[l])
    return x
```

```python
import functools
import math

import jax
import jax.numpy as jnp
from jax import lax
from jax.experimental import pallas as pl
from jax.experimental.pallas import tpu as pltpu

CHUNK = 64
N_LEFT_CHUNKS = 8
HEAD_DIM = 64
REL_CLIP = 256
POOL_WINDOWS = (2, 4, 8, 16)
EPS = 1e-6
NEG_INF = -1e30
LOG2E = math.log2(math.e)

LANES = 128
TILE = 512
QBLK = 2 * CHUNK
BANDW = TILE + QBLK
BAND_BLOCKS = BANDW // LANES
BIAS_EXT = BANDW + QBLK
HEADS_PER_GROUP = 4
GROUP_W = HEADS_PER_GROUP * HEAD_DIM
POOL_PAD = 8
POOL_HIST = 16
POOL_OFF = POOL_PAD + POOL_HIST
ONES_ROWS = 16
SUB_ROWS = 32
NORM_ROWS = 16
BOUND_MARGIN = 1.02
L_FLOOR = 2.0 ** -90
STAGE_ROWS = 256
V7X_VMEM_LIMIT_BYTES = 56 * 1024 * 1024

F32 = jnp.float32
BF16 = jnp.bfloat16


def _silu(z):
    h = 0.5 * z
    return h + h * jnp.tanh(h)


def _block_kernel(x_ref, xn_ref, c_ref, wada_ref, bada_ref, g_ref, win_ref, wout_ref, gq_ref, gk_ref, rb_ref, wpool_ref,
                  ps_ref, o_ref,
                  h_s, z_s, qbd_s, k_s, vt_s, u_s, t2_s, t4_s, t8_s, s_s, p_s, y_s, bias_ref, biasx_ref,
                  wvt_s, gain_s, wp_s, win_s, wout_s, mod_s, stage_s, stage_sem,
                  *, d_model, attn_w, pool_w, tiles_per_seq, n_tiles):
    t = pl.program_id(1)
    n_heads = attn_w // HEAD_DIM
    n_groups = attn_w // GROUP_W
    o_k, o_v, o_u, o_z = attn_w, 2 * attn_w, 3 * attn_w, 3 * attn_w + pool_w
    pool_group = pool_w // len(POOL_WINDOWS)
    pool_rows = TILE + POOL_OFF

    seq = pl.program_id(0)
    seq_next = jnp.minimum(seq * tiles_per_seq + t + 1, n_tiles - 1) // tiles_per_seq

    def norm_rows(src_ref, which_seq, r):
        shift = mod_s[pl.ds(which_seq, 1), 0:d_model]
        scale = mod_s[pl.ds(which_seq, 1), d_model:2 * d_model]
        xc = src_ref[r:r + NORM_ROWS, :]
        ms = jnp.mean(xc * xc, axis=-1, keepdims=True)
        hc = xc * lax.rsqrt(ms + EPS) * (g_ref[...] * (1.0 + scale)) + shift
        h_s[r:r + NORM_ROWS, :] = hc.astype(BF16)
        return hc[0:1, 0:LANES]

    @pl.when((seq == 0) & (t == 0))
    def _():
        chunks = ([(wada_ref, None, r) for r in range(0, d_model, STAGE_ROWS)]
                  + [(win_ref, win_s, r) for r in range(0, win_s.shape[0], STAGE_ROWS)]
                  + [(wout_ref, wout_s, r) for r in range(0, wout_s.shape[0], STAGE_ROWS)])

        def chunk_copy(i):
            src = chunks[i][0]
            return pltpu.make_async_copy(src.at[pl.ds(chunks[i][2], STAGE_ROWS), :],
                                         stage_s.at[i % 2, :, pl.ds(0, src.shape[1])], stage_sem.at[i % 2])

        mod = bada_ref[...]
        chunk_copy(0).start()
        for i, (src, dst, r) in enumerate(chunks):
            if i + 1 < len(chunks):
                chunk_copy(i + 1).start()
            chunk_copy(i).wait()
            staged = stage_s[i % 2, :, 0:src.shape[1]]
            if dst is None:
                mod = mod + jnp.dot(c_ref[:, r:r + STAGE_ROWS], staged, preferred_element_type=F32)
            else:
                dst[r:r + STAGE_ROWS, :] = staged.astype(BF16)
            if i + 1 == d_model // STAGE_ROWS:
                mod_s[0:c_ref.shape[0], :] = mod

        a = lax.broadcasted_iota(jnp.int32, (QBLK, LANES), 0)
        lane = lax.broadcasted_iota(jnp.int32, (QBLK, LANES), 1)
        band_lo = jnp.where(a >= CHUNK, CHUNK, 0)
        gq_max = jnp.max(jnp.abs(gq_ref[...]), axis=-1, keepdims=True)
        gk_max = jnp.max(jnp.abs(gk_ref[...]), axis=-1, keepdims=True)
        dot_bound = gq_max * gk_max * (HEAD_DIM ** 0.5 * LOG2E * BOUND_MARGIN)
        for h in range(n_heads):
            first = rb_ref[h:h + 1, 0:1] * LOG2E
            row_ext = jnp.concatenate([
                jnp.broadcast_to(first, (1, TILE - REL_CLIP)),
                rb_ref[h:h + 1, 0:REL_CLIP + QBLK] * LOG2E,
                jnp.broadcast_to(first, (1, BIAS_EXT - BANDW))], axis=1)
            row = jnp.broadcast_to(row_ext, (QBLK, BIAS_EXT))
            skew = pltpu.roll(row, 0, 1, stride=1, stride_axis=0)
            reachable = rb_ref[h:h + 1, 0:REL_CLIP + CHUNK]
            bound = dot_bound + jnp.max(reachable, axis=-1, keepdims=True) * LOG2E + 1.0
            for cb in range(BAND_BLOCKS):
                j = lane + cb * LANES
                visible = (j >= band_lo) & (j < band_lo + (N_LEFT_CHUNKS + 1) * CHUNK)
                blk = skew[:, cb * LANES:(cb + 1) * LANES]
                biasx_ref[h, cb] = jnp.where(visible, blk, NEG_INF).T
                bias_ref[h, cb] = jnp.where(visible, blk - bound, NEG_INF).T
            biasx_ref[h, BAND_BLOCKS] = jnp.full((QBLK, LANES), NEG_INF, F32)
            bias_ref[h, BAND_BLOCKS] = jnp.full((QBLK, LANES), NEG_INF, F32)
        for r in range(0, d_model, LANES):
            for c in range(0, attn_w, LANES):
                blk = win_s[r:r + LANES, o_v + c:o_v + c + LANES].astype(F32)
                wvt_s[c:c + LANES, r:r + LANES] = blk.T.astype(BF16)
        gain_s[0] = jnp.broadcast_to(jnp.concatenate([gq_ref[...]] * n_heads, axis=1)
                                     * (HEAD_DIM ** -0.5 * LOG2E), (8, attn_w))
        gain_s[1] = jnp.broadcast_to(jnp.concatenate([gk_ref[...]] * n_heads, axis=1), (8, attn_w))
        k_s[...] = jnp.zeros(k_s.shape, BF16)
        vt_s[...] = jnp.zeros(vt_s.shape, BF16)
        wp_s[...] = jnp.zeros(wp_s.shape, BF16)
        for gi in range(len(POOL_WINDOWS)):
            lo = (gi % 2) * pool_group
            wp_s[gi // 2, lo:lo + pool_group, lo:lo + pool_group] = wpool_ref[gi].astype(BF16)
        u_s[...] = jnp.zeros(u_s.shape, F32)
        t2_s[0:POOL_PAD, :] = jnp.zeros((POOL_PAD, pool_w), F32)
        t4_s[0:POOL_PAD, :] = jnp.zeros((POOL_PAD, pool_w), F32)
        t8_s[0:POOL_PAD, :] = jnp.zeros((POOL_PAD, pool_w), F32)
        for r in range(0, TILE, NORM_ROWS):
            norm_rows(x_ref, 0, r)

    gate = mod_s[pl.ds(seq, 1), 2 * d_model:3 * d_model]

    u_s[POOL_PAD:POOL_OFF, :] = jnp.where(t == 0, 0.0, u_s[TILE + POOL_PAD:TILE + POOL_OFF, :])

    k_s[0:TILE, :] = k_s[TILE:2 * TILE, :]
    vt_s[:, 0:TILE] = vt_s[:, TILE:2 * TILE]

    lane_g = lax.broadcasted_iota(jnp.int32, (SUB_ROWS, GROUP_W), 1)
    lane_h = lax.broadcasted_iota(jnp.int32, (SUB_ROWS, LANES), 1)

    def in_proj(rows, c0, width):
        return jnp.dot(h_s[rows, :], win_s[:, c0:c0 + width], preferred_element_type=F32)

    def head_rms(xf, gain_row):
        for sub in range(0, xf.shape[0], SUB_ROWS):
            x = xf[sub:sub + SUB_ROWS]
            sq = x * x
            ssq = []
            for c0 in range(0, GROUP_W, LANES):
                blk = sq[:, c0:c0 + LANES]
                first = jnp.sum(jnp.where(lane_h < HEAD_DIM, blk, 0.0), axis=-1, keepdims=True)
                both = jnp.sum(blk, axis=-1, keepdims=True)
                ssq.append(jnp.where(lane_h < HEAD_DIM, first, both - first))
            ssq = jnp.concatenate(ssq, axis=1)
            yield sub, x * lax.rsqrt(ssq * (1.0 / HEAD_DIM) + EPS) * gain_row

    tile_rows = slice(0, TILE)
    qf = in_proj(tile_rows, 0, attn_w)
    for rb in range(0, TILE, QBLK):
        for g in range(n_groups):
            cols = slice(g * GROUP_W, (g + 1) * GROUP_W)
            gain = gain_s[0, 0:1, cols]
            for sub, qn in head_rms(qf[rb:rb + QBLK, cols], gain):
                for h in range(HEADS_PER_GROUP):
                    keep = (lane_g >= h * HEAD_DIM) & (lane_g < (h + 1) * HEAD_DIM)
                    qbd_s[rb // QBLK, g, h * QBLK + sub:h * QBLK + sub + SUB_ROWS, :] = (
                        jnp.where(keep, qn, 0.0).astype(BF16))
    kf = in_proj(tile_rows, o_k, attn_w)
    for rb in range(0, TILE, QBLK):
        for g in range(n_groups):
            cols = slice(g * GROUP_W, (g + 1) * GROUP_W)
            for sub, kn in head_rms(kf[rb:rb + QBLK, cols], gain_s[1, 0:1, cols]):
                row = TILE + rb + sub
                k_s[row:row + SUB_ROWS, cols] = kn.astype(BF16)
    vt_s[:, TILE:2 * TILE] = lax.dot_general(
        wvt_s[...], h_s[...], (((1,), (1,)), ((), ())), preferred_element_type=F32).astype(BF16)
    u_s[POOL_OFF:POOL_OFF + TILE, :] = in_proj(tile_rows, o_u, pool_w)
    for c0 in range(0, attn_w + pool_w, 512):
        z_s[:, c0:c0 + 512] = _silu(in_proj(tile_rows, o_z + c0, 512)).astype(BF16)

    def pool():
        step = 176
        levels = ((u_s, t2_s, 1), (t2_s, t4_s, 2), (t4_s, t8_s, 4), (t8_s, None, 8))
        for gi in range(len(POOL_WINDOWS)):
            assert POOL_WINDOWS[gi] == 2 ** (gi + 1)
        for li, (src, dst, sh) in enumerate(levels[:-1]):
            lanes = slice((li + 1) * pool_group, pool_w)
            for r in range(POOL_PAD, pool_rows, step):
                dst[r:r + step, lanes] = src[r:r + step, lanes] + src[r - sh:r - sh + step, lanes]
        tok = t * TILE + lax.broadcasted_iota(jnp.int32, (QBLK, pool_group), 0)
        for r in range(0, TILE, QBLK):
            for pair in range(len(POOL_WINDOWS) // 2):
                mixed = []
                for gi in (2 * pair, 2 * pair + 1):
                    lanes = slice(gi * pool_group, (gi + 1) * pool_group)
                    src = levels[gi][0]
                    sh = levels[gi][2]
                    rows = slice(POOL_OFF + r, POOL_OFF + r + QBLK)
                    wsum = src[rows, lanes] + src[POOL_OFF + r - sh:POOL_OFF + r - sh + QBLK, lanes]
                    cnt = jnp.minimum(tok + (r + 1), POOL_WINDOWS[gi]).astype(F32)
                    mixed.append((wsum / cnt - u_s[rows, lanes]).astype(BF16))
                pm = jnp.concatenate(mixed, axis=1)
                c0 = pair * 2 * pool_group
                po = jnp.dot(pm, wp_s[pair], preferred_element_type=F32) * ps_ref[:, c0:c0 + 2 * pool_group]
                gz = z_s[r:r + QBLK, attn_w + c0:attn_w + c0 + 2 * pool_group].astype(F32)
                y_s[r:r + QBLK, attn_w + c0:attn_w + c0 + 2 * pool_group] = (po * gz).astype(BF16)

    row_q = lax.broadcasted_iota(jnp.int32, (QBLK, LANES), 0)
    n_qblk = TILE // QBLK

    def band_blocks(jb):
        n_before = TILE // LANES - jb
        return [jnp.where(t == 0, BAND_BLOCKS, cb) if cb < n_before else cb for cb in range(BAND_BLOCKS)]

    def score_dot(jb, g):
        qrow = jb * QBLK
        return lax.dot_general(k_s[qrow:qrow + BANDW, g * GROUP_W:(g + 1) * GROUP_W], qbd_s[jb, g],
                               (((1,), (1,)), ((), ())), preferred_element_type=F32)

    def probs(jb):
        buf = jb % 2
        blk = band_blocks(jb)
        for g in range(n_groups):
            st = score_dot(jb, g)
            for h in range(HEADS_PER_GROUP):
                hh = g * HEADS_PER_GROUP + h
                lanes = slice(h * QBLK, (h + 1) * QBLK)
                for cb in range(BAND_BLOCKS):
                    for sub in range(0, LANES, SUB_ROWS):
                        rows = slice(cb * LANES + sub, cb * LANES + sub + SUB_ROWS)
                        p = jnp.exp2(st[rows, lanes] + bias_ref[hh, blk[cb], sub:sub + SUB_ROWS, :])
                        p_s[buf, hh // 2, rows, (hh % 2) * QBLK:(hh % 2 + 1) * QBLK] = p.astype(BF16)

    def probs_exact(jb):
        buf = jb % 2
        blk = band_blocks(jb)
        for g in range(n_groups):
            st = score_dot(jb, g)
            for h in range(HEADS_PER_GROUP):
                hh = g * HEADS_PER_GROUP + h
                lanes = slice(h * QBLK, (h + 1) * QBLK)
                macc = None
                for cb in range(BAND_BLOCKS):
                    for sub in range(0, LANES, SUB_ROWS):
                        rows = slice(cb * LANES + sub, cb * LANES + sub + SUB_ROWS)
                        sb = st[rows, lanes] + biasx_ref[hh, blk[cb], sub:sub + SUB_ROWS, :]
                        s_s[hh, rows, :] = sb
                        for r8 in range(0, SUB_ROWS, 8):
                            macc = sb[r8:r8 + 8, :] if macc is None else jnp.maximum(macc, sb[r8:r8 + 8, :])
                m = jnp.max(macc, axis=0, keepdims=True)
                for r in range(0, BANDW, SUB_ROWS):
                    rows = slice(r, r + SUB_ROWS)
                    p = jnp.exp2(s_s[hh, rows, :] - m)
                    p_s[buf, hh // 2, rows, (hh % 2) * QBLK:(hh % 2 + 1) * QBLK] = p.astype(BF16)

    ones_rows = jnp.ones((ONES_ROWS, BANDW), BF16)

    def attend(jb, anchor=None):
        buf = jb % 2
        qrow = jb * QBLK
        lmin = None
        for pair in range(n_heads // 2):
            ha = 2 * pair
            vt1 = jnp.concatenate(
                [vt_s[ha * HEAD_DIM:(ha + 2) * HEAD_DIM, qrow:qrow + BANDW], ones_rows], axis=0)
            ot = jnp.dot(vt1, p_s[buf, pair], preferred_element_type=F32)
            l = ot[2 * HEAD_DIM:2 * HEAD_DIM + 1, :]
            lmin = l if lmin is None else jnp.minimum(lmin, l)
            linv = 1.0 / l
            if anchor is not None:
                linv = jnp.concatenate([jnp.where(never, anchor(pair), linv[:, 0:LANES]), linv[:, LANES:]], axis=1)
            oa = ot[0:2 * HEAD_DIM, 0:QBLK] * linv[:, 0:QBLK]
            ob = ot[0:2 * HEAD_DIM, QBLK:2 * QBLK] * linv[:, QBLK:2 * QBLK]
            a = jnp.where(row_q < HEAD_DIM, oa, ob).T
            c0 = ha * HEAD_DIM
            gz = z_s[qrow:qrow + QBLK, c0:c0 + LANES].astype(F32)
            y_s[qrow:qrow + QBLK, c0:c0 + LANES] = (a * gz).astype(BF16)
        return jnp.min(lmin)

    never = t < 0
    norm_steps = list(range(0, TILE, NORM_ROWS))
    per_pair = len(norm_steps) // (n_qblk * (n_heads // 2))

    def next_norm(jb):
        def anchor(pair):
            i0 = (jb * (n_heads // 2) + pair) * per_pair
            return sum(norm_rows(xn_ref, seq_next, rn) for rn in norm_steps[i0:i0 + per_pair])
        return anchor

    pool()
    probs(0)
    underflowed = []
    for jb in range(n_qblk):
        if jb + 1 < n_qblk:
            probs(jb + 1)
        underflowed.append(jnp.logical_not(attend(jb, next_norm(jb)) >= L_FLOOR))

    o = jnp.dot(y_s[...], wout_s[...], preferred_element_type=F32)
    for c0 in range(0, d_model, GROUP_W):
        cols = slice(c0, c0 + GROUP_W)
        for r in range(0, TILE, SUB_ROWS):
            o_ref[r:r + SUB_ROWS, cols] = x_ref[r:r + SUB_ROWS, cols] + gate[:, cols] * o[r:r + SUB_ROWS, cols]

    @pl.when(functools.reduce(jnp.logical_or, underflowed))
    def _():
        for jb in range(n_qblk):
            @pl.when(underflowed[jb])
            def _(jb=jb):
                probs_exact(jb)
                attend(jb)
                rows = slice(jb * QBLK, (jb + 1) * QBLK)
                o_jb = jnp.dot(y_s[rows, :], wout_s[...], preferred_element_type=F32)
                o_ref[rows, :] = x_ref[rows, :] + gate * o_jb


def _layer(x, c, w_ada, b_ada, norm_g, w_in, q_norm_g, k_norm_g, rel_bias, w_pool, pool_scale, w_out):
    bsz, seq, d_model = x.shape
    n_heads = rel_bias.shape[0]
    attn_w = n_heads * HEAD_DIM
    pool_w = w_pool.shape[0] * w_pool.shape[1]
    pool_group = w_pool.shape[1]
    in_w = w_in.shape[1]
    assert seq % TILE == 0 and attn_w % GROUP_W == 0 and TILE == N_LEFT_CHUNKS * CHUNK
    assert in_w == 3 * attn_w + pool_w + attn_w + pool_w and 2 * pool_group == GROUP_W
    assert attn_w == 512 and pool_w == 512 and bsz <= 8
    n_groups = attn_w // GROUP_W

    n_pairs = w_pool.shape[0] // 2
    assert QBLK <= REL_CLIP <= TILE and rel_bias.shape[1] == 2 * REL_CLIP + 1

    tiles_per_seq = seq // TILE

    def next_tile(b, t):
        n = jnp.minimum(b * tiles_per_seq + t + 1, bsz * tiles_per_seq - 1)
        return n // tiles_per_seq, n % tiles_per_seq

    const2 = lambda b, t: (0, 0)
    const3 = lambda b, t: (0, 0, 0)
    kernel = functools.partial(_block_kernel, d_model=d_model, attn_w=attn_w, pool_w=pool_w,
                               tiles_per_seq=tiles_per_seq, n_tiles=bsz * tiles_per_seq)
    return pl.pallas_call(
        kernel,
        grid=(bsz, seq // TILE),
        in_specs=[
            pl.BlockSpec((None, TILE, d_model), lambda b, t: (b, t, 0)),
            pl.BlockSpec((None, TILE, d_model), lambda b, t: (*next_tile(b, t), 0)),
            pl.BlockSpec((bsz, d_model), const2),
            pl.BlockSpec(memory_space=pl.ANY),
            pl.BlockSpec((1, 3 * d_model), const2),
            pl.BlockSpec((1, d_model), const2),
            pl.BlockSpec(memory_space=pl.ANY),
            pl.BlockSpec(memory_space=pl.ANY),
            pl.BlockSpec((1, HEAD_DIM), const2),
            pl.BlockSpec((1, HEAD_DIM), const2),
            pl.BlockSpec((n_heads, 2 * REL_CLIP + 1), const2),
            pl.BlockSpec((2 * n_pairs, pool_group, pool_group), const3),
            pl.BlockSpec((1, pool_w), const2),
        ],
        out_specs=pl.BlockSpec((None, TILE, d_model), lambda b, t: (b, t, 0)),
        out_shape=jax.ShapeDtypeStruct(x.shape, x.dtype),
        scratch_shapes=[
            pltpu.VMEM((TILE, d_model), BF16),
            pltpu.VMEM((TILE, attn_w + pool_w), BF16),
            pltpu.VMEM((TILE // QBLK, n_groups, HEADS_PER_GROUP * QBLK, GROUP_W), BF16),
            pltpu.VMEM((2 * TILE, attn_w), BF16),
            pltpu.VMEM((attn_w, 2 * TILE), BF16),
            pltpu.VMEM((TILE + POOL_OFF, pool_w), F32),
            pltpu.VMEM((TILE + POOL_OFF, pool_w), F32),
            pltpu.VMEM((TILE + POOL_OFF, pool_w), F32),
            pltpu.VMEM((TILE + POOL_OFF, pool_w), F32),
            pltpu.VMEM((n_heads, BANDW, QBLK), F32),
            pltpu.VMEM((2, n_heads // 2, BANDW, 2 * QBLK), BF16),
            pltpu.VMEM((TILE, attn_w + pool_w), BF16),
            pltpu.VMEM((n_heads, BAND_BLOCKS + 1, QBLK, LANES), F32),
            pltpu.VMEM((n_heads, BAND_BLOCKS + 1, QBLK, LANES), F32),
            pltpu.VMEM((attn_w, d_model), BF16),
            pltpu.VMEM((2, 8, attn_w), F32),
            pltpu.VMEM((n_pairs, GROUP_W, GROUP_W), BF16),
            pltpu.VMEM((d_model, in_w), BF16),
            pltpu.VMEM((attn_w + pool_w, d_model), BF16),
            pltpu.VMEM((8, 3 * d_model), F32),
            pltpu.VMEM((2, STAGE_ROWS, in_w), F32),
            pltpu.SemaphoreType.DMA((2,)),
        ],
        compiler_params=pltpu.CompilerParams(
            dimension_semantics=("arbitrary", "arbitrary"),
            vmem_limit_bytes=V7X_VMEM_LIMIT_BYTES,
        ),
        name="hybrid_block",
    )(x, x, c, w_ada, b_ada.reshape(1, 3 * d_model), norm_g.reshape(1, d_model), w_in, w_out,
      q_norm_g.reshape(1, HEAD_DIM), k_norm_g.reshape(1, HEAD_DIM), rel_bias, w_pool,
      pool_scale.reshape(1, pool_w))


def kernel(x, c, norm_g, w_ada, b_ada, w_in, q_norm_g, k_norm_g, rel_bias, w_pool, pool_scale, w_out):
    depth = w_in.shape[0]
    for l in range(depth):
        x = _layer(x, c, w_ada[l], b_ada[l], norm_g[l], w_in[l], q_norm_g[l], k_norm_g[l], rel_bias[l],
                   w_pool[l], pool_scale[l], w_out[l])
    return x
```

```python
import functools
import math

import jax
import jax.numpy as jnp
from jax import lax
from jax.experimental import pallas as pl
from jax.experimental.pallas import tpu as pltpu

CHUNK = 64
N_LEFT_CHUNKS = 8
HEAD_DIM = 64
REL_CLIP = 256
POOL_WINDOWS = (2, 4, 8, 16)
EPS = 1e-6
NEG_INF = -1e30
LOG2E = math.log2(math.e)

LANES = 128
TILE = 512
QBLK = 2 * CHUNK
BANDW = TILE + QBLK
BAND_BLOCKS = BANDW // LANES
BIAS_EXT = BANDW + QBLK
HEADS_PER_GROUP = 4
GROUP_W = HEADS_PER_GROUP * HEAD_DIM
POOL_PAD = 8
POOL_HIST = 16
POOL_OFF = POOL_PAD + POOL_HIST
ONES_ROWS = 16
SUB_ROWS = 32
NORM_ROWS = 16
BOUND_MARGIN = 1.02
L_FLOOR = 2.0 ** -90
STAGE_ROWS = 256
V7X_VMEM_LIMIT_BYTES = 56 * 1024 * 1024

F32 = jnp.float32
BF16 = jnp.bfloat16


def _silu(z):
    h = 0.5 * z
    return h + h * jnp.tanh(h)


def _block_kernel(x_ref, xn_ref, c_ref, wada_ref, bada_ref, g_ref, win_ref, wout_ref, gq_ref, gk_ref, rb_ref, wpool_ref,
                  ps_ref, o_ref,
                  h_s, z_s, qbd_s, k_s, vt_s, u_s, t2_s, t4_s, t8_s, s_s, p_s, y_s, bias_ref, biasx_ref,
                  wvt_s, gain_s, wp_s, win_s, wout_s, mod_s, stage_s, stage_sem,
                  *, d_model, attn_w, pool_w, tiles_per_seq, n_tiles):
    t = pl.program_id(1)
    n_heads = attn_w // HEAD_DIM
    n_groups = attn_w // GROUP_W
    o_k, o_v, o_u, o_z = attn_w, 2 * attn_w, 3 * attn_w, 3 * attn_w + pool_w
    pool_group = pool_w // len(POOL_WINDOWS)
    pool_rows = TILE + POOL_OFF

    seq = pl.program_id(0)
    seq_next = jnp.minimum(seq * tiles_per_seq + t + 1, n_tiles - 1) // tiles_per_seq

    def norm_rows(src_ref, which_seq, r):
        shift = mod_s[pl.ds(which_seq, 1), 0:d_model]
        gain = mod_s[pl.ds(which_seq, 1), d_model:2 * d_model]
        xc = src_ref[r:r + NORM_ROWS, :]
        ms = jnp.mean(xc * xc, axis=-1, keepdims=True)
        hc = xc * lax.rsqrt(ms + EPS) * gain + shift
        h_s[r:r + NORM_ROWS, :] = hc.astype(BF16)
        return hc[0:1, 0:LANES]

    @pl.when((seq == 0) & (t == 0))
    def _():
        chunks = ([(wada_ref, None, r) for r in range(0, d_model, STAGE_ROWS)]
                  + [(win_ref, win_s, r) for r in range(0, win_s.shape[0], STAGE_ROWS)]
                  + [(wout_ref, wout_s, r) for r in range(0, wout_s.shape[0], STAGE_ROWS)])

        def chunk_copy(i):
            src = chunks[i][0]
            return pltpu.make_async_copy(src.at[pl.ds(chunks[i][2], STAGE_ROWS), :],
                                         stage_s.at[i % 2, :, pl.ds(0, src.shape[1])], stage_sem.at[i % 2])

        mod = bada_ref[...]
        chunk_copy(0).start()
        for i, (src, dst, r) in enumerate(chunks):
            if i + 1 < len(chunks):
                chunk_copy(i + 1).start()
            chunk_copy(i).wait()
            staged = stage_s[i % 2, :, 0:src.shape[1]]
            if dst is None:
                mod = mod + jnp.dot(c_ref[:, r:r + STAGE_ROWS], staged, preferred_element_type=F32)
            else:
                dst[r:r + STAGE_ROWS, :] = staged.astype(BF16)
            if i + 1 == d_model // STAGE_ROWS:
                mod_s[0:c_ref.shape[0], :] = mod
                mod_s[0:c_ref.shape[0], d_model:2 * d_model] = g_ref[...] * (1.0 + mod[:, d_model:2 * d_model])

        a = lax.broadcasted_iota(jnp.int32, (QBLK, LANES), 0)
        lane = lax.broadcasted_iota(jnp.int32, (QBLK, LANES), 1)
        band_lo = jnp.where(a >= CHUNK, CHUNK, 0)
        gq_max = jnp.max(jnp.abs(gq_ref[...]), axis=-1, keepdims=True)
        gk_max = jnp.max(jnp.abs(gk_ref[...]), axis=-1, keepdims=True)
        dot_bound = gq_max * gk_max * (HEAD_DIM ** 0.5 * LOG2E * BOUND_MARGIN)
        for h in range(n_heads):
            first = rb_ref[h:h + 1, 0:1] * LOG2E
            row_ext = jnp.concatenate([
                jnp.broadcast_to(first, (1, TILE - REL_CLIP)),
                rb_ref[h:h + 1, 0:REL_CLIP + QBLK] * LOG2E,
                jnp.broadcast_to(first, (1, BIAS_EXT - BANDW))], axis=1)
            row = jnp.broadcast_to(row_ext, (QBLK, BIAS_EXT))
            skew = pltpu.roll(row, 0, 1, stride=1, stride_axis=0)
            reachable = rb_ref[h:h + 1, 0:REL_CLIP + CHUNK]
            bound = dot_bound + jnp.max(reachable, axis=-1, keepdims=True) * LOG2E + 1.0
            for cb in range(BAND_BLOCKS):
                j = lane + cb * LANES
                visible = (j >= band_lo) & (j < band_lo + (N_LEFT_CHUNKS + 1) * CHUNK)
                blk = skew[:, cb * LANES:(cb + 1) * LANES]
                biasx_ref[h, cb] = jnp.where(visible, blk, NEG_INF).T
                bias_ref[h, cb] = jnp.where(visible, blk - bound, NEG_INF).T
            biasx_ref[h, BAND_BLOCKS] = jnp.full((QBLK, LANES), NEG_INF, F32)
            bias_ref[h, BAND_BLOCKS] = jnp.full((QBLK, LANES), NEG_INF, F32)
        for r in range(0, d_model, LANES):
            for c in range(0, attn_w, LANES):
                blk = win_s[r:r + LANES, o_v + c:o_v + c + LANES].astype(F32)
                wvt_s[c:c + LANES, r:r + LANES] = blk.T.astype(BF16)
        gain_s[0] = jnp.broadcast_to(jnp.concatenate([gq_ref[...]] * n_heads, axis=1)
                                     * (HEAD_DIM ** -0.5 * LOG2E), (8, attn_w))
        gain_s[1] = jnp.broadcast_to(jnp.concatenate([gk_ref[...]] * n_heads, axis=1), (8, attn_w))
        k_s[...] = jnp.zeros(k_s.shape, BF16)
        vt_s[...] = jnp.zeros(vt_s.shape, BF16)
        wp_s[...] = jnp.zeros(wp_s.shape, BF16)
        for gi in range(len(POOL_WINDOWS)):
            lo = (gi % 2) * pool_group
            wp_s[gi // 2, lo:lo + pool_group, lo:lo + pool_group] = wpool_ref[gi].astype(BF16)
        u_s[...] = jnp.zeros(u_s.shape, F32)
        t2_s[0:POOL_PAD, :] = jnp.zeros((POOL_PAD, pool_w), F32)
        t4_s[0:POOL_PAD, :] = jnp.zeros((POOL_PAD, pool_w), F32)
        t8_s[0:POOL_PAD, :] = jnp.zeros((POOL_PAD, pool_w), F32)
        for r in range(0, TILE, NORM_ROWS):
            norm_rows(x_ref, 0, r)

    gate = mod_s[pl.ds(seq, 1), 2 * d_model:3 * d_model]

    u_s[POOL_PAD:POOL_OFF, :] = jnp.where(t == 0, 0.0, u_s[TILE + POOL_PAD:TILE + POOL_OFF, :])

    k_s[0:TILE, :] = k_s[TILE:2 * TILE, :]
    vt_s[:, 0:TILE] = vt_s[:, TILE:2 * TILE]

    lane_g = lax.broadcasted_iota(jnp.int32, (SUB_ROWS, GROUP_W), 1)
    lane_h = lax.broadcasted_iota(jnp.int32, (SUB_ROWS, LANES), 1)

    def in_proj(rows, c0, width):
        return jnp.dot(h_s[rows, :], win_s[:, c0:c0 + width], preferred_element_type=F32)

    def head_rms(xf, gain_row):
        for sub in range(0, xf.shape[0], SUB_ROWS):
            x = xf[sub:sub + SUB_ROWS]
            sq = x * x
            ssq = []
            for c0 in range(0, GROUP_W, LANES):
                blk = sq[:, c0:c0 + LANES]
                first = jnp.sum(jnp.where(lane_h < HEAD_DIM, blk, 0.0), axis=-1, keepdims=True)
                both = jnp.sum(blk, axis=-1, keepdims=True)
                ssq.append(jnp.where(lane_h < HEAD_DIM, first, both - first))
            ssq = jnp.concatenate(ssq, axis=1)
            yield sub, x * lax.rsqrt(ssq * (1.0 / HEAD_DIM) + EPS) * gain_row

    tile_rows = slice(0, TILE)
    qf = in_proj(tile_rows, 0, attn_w)
    for rb in range(0, TILE, QBLK):
        for g in range(n_groups):
            cols = slice(g * GROUP_W, (g + 1) * GROUP_W)
            gain = gain_s[0, 0:1, cols]
            for sub, qn in head_rms(qf[rb:rb + QBLK, cols], gain):
                for h in range(HEADS_PER_GROUP):
                    keep = (lane_g >= h * HEAD_DIM) & (lane_g < (h + 1) * HEAD_DIM)
                    qbd_s[rb // QBLK, g, h * QBLK + sub:h * QBLK + sub + SUB_ROWS, :] = (
                        jnp.where(keep, qn, 0.0).astype(BF16))
    kf = in_proj(tile_rows, o_k, attn_w)
    for rb in range(0, TILE, QBLK):
        for g in range(n_groups):
            cols = slice(g * GROUP_W, (g + 1) * GROUP_W)
            for sub, kn in head_rms(kf[rb:rb + QBLK, cols], gain_s[1, 0:1, cols]):
                row = TILE + rb + sub
                k_s[row:row + SUB_ROWS, cols] = kn.astype(BF16)
    vt_s[:, TILE:2 * TILE] = lax.dot_general(
        wvt_s[...], h_s[...], (((1,), (1,)), ((), ())), preferred_element_type=F32).astype(BF16)
    u_s[POOL_OFF:POOL_OFF + TILE, :] = in_proj(tile_rows, o_u, pool_w)
    for c0 in range(0, attn_w + pool_w, 512):
        z_s[:, c0:c0 + 512] = _silu(in_proj(tile_rows, o_z + c0, 512)).astype(BF16)

    def pool():
        step = 176
        levels = ((u_s, t2_s, 1), (t2_s, t4_s, 2), (t4_s, t8_s, 4), (t8_s, None, 8))
        for gi in range(len(POOL_WINDOWS)):
            assert POOL_WINDOWS[gi] == 2 ** (gi + 1)
        for li, (src, dst, sh) in enumerate(levels[:-1]):
            lanes = slice((li + 1) * pool_group, pool_w)
            for r in range(POOL_PAD, pool_rows, step):
                dst[r:r + step, lanes] = src[r:r + step, lanes] + src[r - sh:r - sh + step, lanes]
        tok = t * TILE + lax.broadcasted_iota(jnp.int32, (QBLK, pool_group), 0)
        for r in range(0, TILE, QBLK):
            for pair in range(len(POOL_WINDOWS) // 2):
                mixed = []
                for gi in (2 * pair, 2 * pair + 1):
                    lanes = slice(gi * pool_group, (gi + 1) * pool_group)
                    src = levels[gi][0]
                    sh = levels[gi][2]
                    rows = slice(POOL_OFF + r, POOL_OFF + r + QBLK)
                    wsum = src[rows, lanes] + src[POOL_OFF + r - sh:POOL_OFF + r - sh + QBLK, lanes]
                    cnt = jnp.minimum(tok + (r + 1), POOL_WINDOWS[gi]).astype(F32)
                    mixed.append((wsum / cnt - u_s[rows, lanes]).astype(BF16))
                pm = jnp.concatenate(mixed, axis=1)
                c0 = pair * 2 * pool_group
                po = jnp.dot(pm, wp_s[pair], preferred_element_type=F32) * ps_ref[:, c0:c0 + 2 * pool_group]
                gz = z_s[r:r + QBLK, attn_w + c0:attn_w + c0 + 2 * pool_group].astype(F32)
                y_s[r:r + QBLK, attn_w + c0:attn_w + c0 + 2 * pool_group] = (po * gz).astype(BF16)

    row_q = lax.broadcasted_iota(jnp.int32, (QBLK, LANES), 0)
    n_qblk = TILE // QBLK

    def band_blocks(jb):
        n_before = TILE // LANES - jb
        return [jnp.where(t == 0, BAND_BLOCKS, cb) if cb < n_before else cb for cb in range(BAND_BLOCKS)]

    def score_dot(jb, g):
        qrow = jb * QBLK
        return lax.dot_general(k_s[qrow:qrow + BANDW, g * GROUP_W:(g + 1) * GROUP_W], qbd_s[jb, g],
                               (((1,), (1,)), ((), ())), preferred_element_type=F32)

    def probs(jb):
        buf = jb % 2
        blk = band_blocks(jb)
        for g in range(n_groups):
            st = score_dot(jb, g)
            for h in range(HEADS_PER_GROUP):
                hh = g * HEADS_PER_GROUP + h
                lanes = slice(h * QBLK, (h + 1) * QBLK)
                for cb in range(BAND_BLOCKS):
                    for sub in range(0, LANES, SUB_ROWS):
                        rows = slice(cb * LANES + sub, cb * LANES + sub + SUB_ROWS)
                        p = jnp.exp2(st[rows, lanes] + bias_ref[hh, blk[cb], sub:sub + SUB_ROWS, :])
                        p_s[buf, hh // 2, rows, (hh % 2) * QBLK:(hh % 2 + 1) * QBLK] = p.astype(BF16)

    def probs_exact(jb):
        buf = jb % 2
        blk = band_blocks(jb)
        for g in range(n_groups):
            st = score_dot(jb, g)
            for h in range(HEADS_PER_GROUP):
                hh = g * HEADS_PER_GROUP + h
                lanes = slice(h * QBLK, (h + 1) * QBLK)
                macc = None
                for cb in range(BAND_BLOCKS):
                    for sub in range(0, LANES, SUB_ROWS):
                        rows = slice(cb * LANES + sub, cb * LANES + sub + SUB_ROWS)
                        sb = st[rows, lanes] + biasx_ref[hh, blk[cb], sub:sub + SUB_ROWS, :]
                        s_s[hh, rows, :] = sb
                        for r8 in range(0, SUB_ROWS, 8):
                            macc = sb[r8:r8 + 8, :] if macc is None else jnp.maximum(macc, sb[r8:r8 + 8, :])
                m = jnp.max(macc, axis=0, keepdims=True)
                for r in range(0, BANDW, SUB_ROWS):
                    rows = slice(r, r + SUB_ROWS)
                    p = jnp.exp2(s_s[hh, rows, :] - m)
                    p_s[buf, hh // 2, rows, (hh % 2) * QBLK:(hh % 2 + 1) * QBLK] = p.astype(BF16)

    ones_rows = jnp.ones((ONES_ROWS, BANDW), BF16)

    def attend(jb, anchor=None):
        buf = jb % 2
        qrow = jb * QBLK
        lmin = None
        for pair in range(n_heads // 2):
            ha = 2 * pair
            vt1 = jnp.concatenate(
                [vt_s[ha * HEAD_DIM:(ha + 2) * HEAD_DIM, qrow:qrow + BANDW], ones_rows], axis=0)
            ot = jnp.dot(vt1, p_s[buf, pair], preferred_element_type=F32)
            l = ot[2 * HEAD_DIM:2 * HEAD_DIM + 1, :]
            lmin = l if lmin is None else jnp.minimum(lmin, l)
            linv = 1.0 / l
            if anchor is not None:
                linv = jnp.concatenate([jnp.where(never, anchor(pair), linv[:, 0:LANES]), linv[:, LANES:]], axis=1)
            oa = ot[0:2 * HEAD_DIM, 0:QBLK] * linv[:, 0:QBLK]
            ob = ot[0:2 * HEAD_DIM, QBLK:2 * QBLK] * linv[:, QBLK:2 * QBLK]
            a = jnp.where(row_q < HEAD_DIM, oa, ob).T
            c0 = ha * HEAD_DIM
            gz = z_s[qrow:qrow + QBLK, c0:c0 + LANES].astype(F32)
            y_s[qrow:qrow + QBLK, c0:c0 + LANES] = (a * gz).astype(BF16)
        return jnp.min(lmin)

    never = t < 0
    norm_steps = list(range(0, TILE, NORM_ROWS))
    per_pair = len(norm_steps) // (n_qblk * (n_heads // 2))

    def next_norm(jb):
        def anchor(pair):
            i0 = (jb * (n_heads // 2) + pair) * per_pair
            return sum(norm_rows(xn_ref, seq_next, rn) for rn in norm_steps[i0:i0 + per_pair])
        return anchor

    pool()
    probs(0)
    underflowed = []
    for jb in range(n_qblk):
        if jb + 1 < n_qblk:
            probs(jb + 1)
        underflowed.append(jnp.logical_not(attend(jb, next_norm(jb)) >= L_FLOOR))

    o = jnp.dot(y_s[...], wout_s[...], preferred_element_type=F32)
    for c0 in range(0, d_model, GROUP_W):
        cols = slice(c0, c0 + GROUP_W)
        for r in range(0, TILE, SUB_ROWS):
            o_ref[r:r + SUB_ROWS, cols] = x_ref[r:r + SUB_ROWS, cols] + gate[:, cols] * o[r:r + SUB_ROWS, cols]

    @pl.when(functools.reduce(jnp.logical_or, underflowed))
    def _():
        for jb in range(n_qblk):
            @pl.when(underflowed[jb])
            def _(jb=jb):
                probs_exact(jb)
                attend(jb)
                rows = slice(jb * QBLK, (jb + 1) * QBLK)
                o_jb = jnp.dot(y_s[rows, :], wout_s[...], preferred_element_type=F32)
                o_ref[rows, :] = x_ref[rows, :] + gate * o_jb


def _layer(x, c, w_ada, b_ada, norm_g, w_in, q_norm_g, k_norm_g, rel_bias, w_pool, pool_scale, w_out):
    bsz, seq, d_model = x.shape
    n_heads = rel_bias.shape[0]
    attn_w = n_heads * HEAD_DIM
    pool_w = w_pool.shape[0] * w_pool.shape[1]
    pool_group = w_pool.shape[1]
    in_w = w_in.shape[1]
    assert seq % TILE == 0 and attn_w % GROUP_W == 0 and TILE == N_LEFT_CHUNKS * CHUNK
    assert in_w == 3 * attn_w + pool_w + attn_w + pool_w and 2 * pool_group == GROUP_W
    assert attn_w == 512 and pool_w == 512 and bsz <= 8
    n_groups = attn_w // GROUP_W

    n_pairs = w_pool.shape[0] // 2
    assert QBLK <= REL_CLIP <= TILE and rel_bias.shape[1] == 2 * REL_CLIP + 1

    tiles_per_seq = seq // TILE

    def next_tile(b, t):
        n = jnp.minimum(b * tiles_per_seq + t + 1, bsz * tiles_per_seq - 1)
        return n // tiles_per_seq, n % tiles_per_seq

    const2 = lambda b, t: (0, 0)
    const3 = lambda b, t: (0, 0, 0)
    kernel = functools.partial(_block_kernel, d_model=d_model, attn_w=attn_w, pool_w=pool_w,
                               tiles_per_seq=tiles_per_seq, n_tiles=bsz * tiles_per_seq)
    return pl.pallas_call(
        kernel,
        grid=(bsz, seq // TILE),
        in_specs=[
            pl.BlockSpec((None, TILE, d_model), lambda b, t: (b, t, 0)),
            pl.BlockSpec((None, TILE, d_model), lambda b, t: (*next_tile(b, t), 0)),
            pl.BlockSpec((bsz, d_model), const2),
            pl.BlockSpec(memory_space=pl.ANY),
            pl.BlockSpec((1, 3 * d_model), const2),
            pl.BlockSpec((1, d_model), const2),
            pl.BlockSpec(memory_space=pl.ANY),
            pl.BlockSpec(memory_space=pl.ANY),
            pl.BlockSpec((1, HEAD_DIM), const2),
            pl.BlockSpec((1, HEAD_DIM), const2),
            pl.BlockSpec((n_heads, 2 * REL_CLIP + 1), const2),
            pl.BlockSpec((2 * n_pairs, pool_group, pool_group), const3),
            pl.BlockSpec((1, pool_w), const2),
        ],
        out_specs=pl.BlockSpec((None, TILE, d_model), lambda b, t: (b, t, 0)),
        out_shape=jax.ShapeDtypeStruct(x.shape, x.dtype),
        scratch_shapes=[
            pltpu.VMEM((TILE, d_model), BF16),
            pltpu.VMEM((TILE, attn_w + pool_w), BF16),
            pltpu.VMEM((TILE // QBLK, n_groups, HEADS_PER_GROUP * QBLK, GROUP_W), BF16),
            pltpu.VMEM((2 * TILE, attn_w), BF16),
            pltpu.VMEM((attn_w, 2 * TILE), BF16),
            pltpu.VMEM((TILE + POOL_OFF, pool_w), F32),
            pltpu.VMEM((TILE + POOL_OFF, pool_w), F32),
            pltpu.VMEM((TILE + POOL_OFF, pool_w), F32),
            pltpu.VMEM((TILE + POOL_OFF, pool_w), F32),
            pltpu.VMEM((n_heads, BANDW, QBLK), F32),
            pltpu.VMEM((2, n_heads // 2, BANDW, 2 * QBLK), BF16),
            pltpu.VMEM((TILE, attn_w + pool_w), BF16),
            pltpu.VMEM((n_heads, BAND_BLOCKS + 1, QBLK, LANES), F32),
            pltpu.VMEM((n_heads, BAND_BLOCKS + 1, QBLK, LANES), F32),
            pltpu.VMEM((attn_w, d_model), BF16),
            pltpu.VMEM((2, 8, attn_w), F32),
            pltpu.VMEM((n_pairs, GROUP_W, GROUP_W), BF16),
            pltpu.VMEM((d_model, in_w), BF16),
            pltpu.VMEM((attn_w + pool_w, d_model), BF16),
            pltpu.VMEM((8, 3 * d_model), F32),
            pltpu.VMEM((2, STAGE_ROWS, in_w), F32),
            pltpu.SemaphoreType.DMA((2,)),
        ],
        compiler_params=pltpu.CompilerParams(
            dimension_semantics=("arbitrary", "arbitrary"),
            vmem_limit_bytes=V7X_VMEM_LIMIT_BYTES,
        ),
        name="hybrid_block",
    )(x, x, c, w_ada, b_ada.reshape(1, 3 * d_model), norm_g.reshape(1, d_model), w_in, w_out,
      q_norm_g.reshape(1, HEAD_DIM), k_norm_g.reshape(1, HEAD_DIM), rel_bias, w_pool,
      pool_scale.reshape(1, pool_w))


def kernel(x, c, norm_g, w_ada, b_ada, w_in, q_norm_g, k_norm_g, rel_bias, w_pool, pool_scale, w_out):
    depth = w_in.shape[0]
    for l in range(depth):
        x = _layer(x, c, w_ada[l], b_ada[l], norm_g[l], w_in[l], q_norm_g[l], k_norm_g[l], rel_bias[l],
                   w_pool[l], pool_scale[l], w_out[l])
    return x
```

```python
import functools
import math

import jax
import jax.numpy as jnp
from jax import lax
from jax.experimental import pallas as pl
from jax.experimental.pallas import tpu as pltpu

CHUNK = 64
N_LEFT_CHUNKS = 8
HEAD_DIM = 64
REL_CLIP = 256
POOL_WINDOWS = (2, 4, 8, 16)
EPS = 1e-6
NEG_INF = -1e30
LOG2E = math.log2(math.e)

LANES = 128
TILE = 512
QBLK = 2 * CHUNK
BANDW = TILE + QBLK
BAND_BLOCKS = BANDW // LANES
BIAS_EXT = BANDW + QBLK
HEADS_PER_GROUP = 4
GROUP_W = HEADS_PER_GROUP * HEAD_DIM
POOL_PAD = 8
POOL_HIST = 16
POOL_OFF = POOL_PAD + POOL_HIST
ONES_ROWS = 16
SUB_ROWS = 32
NORM_ROWS = 16
BOUND_MARGIN = 1.02
L_FLOOR = 2.0 ** -90
STAGE_ROWS = 256
V7X_VMEM_LIMIT_BYTES = 56 * 1024 * 1024

F32 = jnp.float32
BF16 = jnp.bfloat16


def _silu(z):
    h = 0.5 * z
    return h + h * jnp.tanh(h)


def _block_kernel(x_ref, xn_ref, c_ref, wada_ref, bada_ref, g_ref, win_ref, wout_ref, gq_ref, gk_ref, rb_ref, wpool_ref,
                  ps_ref, o_ref,
                  h_s, z_s, qbd_s, k_s, vt_s, u_s, t2_s, t4_s, t8_s, s_s, p_s, y_s, bias_ref, biasx_ref,
                  wvt_s, gain_s, wp_s, win_s, wout_s, mod_s, stage_s, stage_sem,
                  *, d_model, attn_w, pool_w, tiles_per_seq, n_tiles):
    t = pl.program_id(1)
    n_heads = attn_w // HEAD_DIM
    n_groups = attn_w // GROUP_W
    o_k, o_v, o_u, o_z = attn_w, 2 * attn_w, 3 * attn_w, 3 * attn_w + pool_w
    pool_group = pool_w // len(POOL_WINDOWS)
    pool_rows = TILE + POOL_OFF

    seq = pl.program_id(0)
    seq_next = jnp.minimum(seq * tiles_per_seq + t + 1, n_tiles - 1) // tiles_per_seq

    def norm_rows(src_ref, which_seq, r):
        shift = mod_s[pl.ds(which_seq, 1), 0:d_model]
        scale = mod_s[pl.ds(which_seq, 1), d_model:2 * d_model]
        xc = src_ref[r:r + NORM_ROWS, :]
        ms = jnp.mean(xc * xc, axis=-1, keepdims=True)
        hc = xc * lax.rsqrt(ms + EPS) * (g_ref[...] * (1.0 + scale)) + shift
        h_s[r:r + NORM_ROWS, :] = hc.astype(BF16)
        return hc[0:1, 0:LANES]

    @pl.when((seq == 0) & (t == 0))
    def _():
        chunks = ([(wada_ref, None, r) for r in range(0, d_model, STAGE_ROWS)]
                  + [(win_ref, win_s, r) for r in range(0, win_s.shape[0], STAGE_ROWS)]
                  + [(wout_ref, wout_s, r) for r in range(0, wout_s.shape[0], STAGE_ROWS)])

        def chunk_copy(i):
            src = chunks[i][0]
            return pltpu.make_async_copy(src.at[pl.ds(chunks[i][2], STAGE_ROWS), :],
                                         stage_s.at[i % 2, :, pl.ds(0, src.shape[1])], stage_sem.at[i % 2])

        a = lax.broadcasted_iota(jnp.int32, (QBLK, LANES), 0)
        lane = lax.broadcasted_iota(jnp.int32, (QBLK, LANES), 1)
        band_lo = jnp.where(a >= CHUNK, CHUNK, 0)
        gq_max = jnp.max(jnp.abs(gq_ref[...]), axis=-1, keepdims=True)
        gk_max = jnp.max(jnp.abs(gk_ref[...]), axis=-1, keepdims=True)
        dot_bound = gq_max * gk_max * (HEAD_DIM ** 0.5 * LOG2E * BOUND_MARGIN)

        def build_tables(h):
            first = rb_ref[h:h + 1, 0:1] * LOG2E
            row_ext = jnp.concatenate([
                jnp.broadcast_to(first, (1, TILE - REL_CLIP)),
                rb_ref[h:h + 1, 0:REL_CLIP + QBLK] * LOG2E,
                jnp.broadcast_to(first, (1, BIAS_EXT - BANDW))], axis=1)
            row = jnp.broadcast_to(row_ext, (QBLK, BIAS_EXT))
            skew = pltpu.roll(row, 0, 1, stride=1, stride_axis=0)
            reachable = rb_ref[h:h + 1, 0:REL_CLIP + CHUNK]
            bound = dot_bound + jnp.max(reachable, axis=-1, keepdims=True) * LOG2E + 1.0
            for cb in range(BAND_BLOCKS):
                j = lane + cb * LANES
                visible = (j >= band_lo) & (j < band_lo + (N_LEFT_CHUNKS + 1) * CHUNK)
                blk = skew[:, cb * LANES:(cb + 1) * LANES]
                biasx_ref[h, cb] = jnp.where(visible, blk, NEG_INF).T
                bias_ref[h, cb] = jnp.where(visible, blk - bound, NEG_INF).T
            biasx_ref[h, BAND_BLOCKS] = jnp.full((QBLK, LANES), NEG_INF, F32)
            bias_ref[h, BAND_BLOCKS] = jnp.full((QBLK, LANES), NEG_INF, F32)

        assert len(chunks) >= n_heads
        mod = bada_ref[...]
        chunk_copy(0).start()
        for i, (src, dst, r) in enumerate(chunks):
            if i + 1 < len(chunks):
                chunk_copy(i + 1).start()
            if i < n_heads:
                build_tables(i)
            chunk_copy(i).wait()
            staged = stage_s[i % 2, :, 0:src.shape[1]]
            if dst is None:
                mod = mod + jnp.dot(c_ref[:, r:r + STAGE_ROWS], staged, preferred_element_type=F32)
            else:
                dst[r:r + STAGE_ROWS, :] = staged.astype(BF16)
            if i + 1 == d_model // STAGE_ROWS:
                mod_s[0:c_ref.shape[0], :] = mod

        for r in range(0, d_model, LANES):
            for c in range(0, attn_w, LANES):
                blk = win_s[r:r + LANES, o_v + c:o_v + c + LANES].astype(F32)
                wvt_s[c:c + LANES, r:r + LANES] = blk.T.astype(BF16)
        gain_s[0] = jnp.broadcast_to(jnp.concatenate([gq_ref[...]] * n_heads, axis=1)
                                     * (HEAD_DIM ** -0.5 * LOG2E), (8, attn_w))
        gain_s[1] = jnp.broadcast_to(jnp.concatenate([gk_ref[...]] * n_heads, axis=1), (8, attn_w))
        k_s[...] = jnp.zeros(k_s.shape, BF16)
        vt_s[...] = jnp.zeros(vt_s.shape, BF16)
        wp_s[...] = jnp.zeros(wp_s.shape, BF16)
        for gi in range(len(POOL_WINDOWS)):
            lo = (gi % 2) * pool_group
            wp_s[gi // 2, lo:lo + pool_group, lo:lo + pool_group] = wpool_ref[gi].astype(BF16)
        u_s[...] = jnp.zeros(u_s.shape, F32)
        t2_s[0:POOL_PAD, :] = jnp.zeros((POOL_PAD, pool_w), F32)
        t4_s[0:POOL_PAD, :] = jnp.zeros((POOL_PAD, pool_w), F32)
        t8_s[0:POOL_PAD, :] = jnp.zeros((POOL_PAD, pool_w), F32)
        for r in range(0, TILE, NORM_ROWS):
            norm_rows(x_ref, 0, r)

    gate = mod_s[pl.ds(seq, 1), 2 * d_model:3 * d_model]

    u_s[POOL_PAD:POOL_OFF, :] = jnp.where(t == 0, 0.0, u_s[TILE + POOL_PAD:TILE + POOL_OFF, :])

    k_s[0:TILE, :] = k_s[TILE:2 * TILE, :]
    vt_s[:, 0:TILE] = vt_s[:, TILE:2 * TILE]

    lane_g = lax.broadcasted_iota(jnp.int32, (SUB_ROWS, GROUP_W), 1)
    lane_h = lax.broadcasted_iota(jnp.int32, (SUB_ROWS, LANES), 1)

    def in_proj(rows, c0, width):
        return jnp.dot(h_s[rows, :], win_s[:, c0:c0 + width], preferred_element_type=F32)

    def head_rms(xf, gain_row):
        for sub in range(0, xf.shape[0], SUB_ROWS):
            x = xf[sub:sub + SUB_ROWS]
            sq = x * x
            ssq = []
            for c0 in range(0, GROUP_W, LANES):
                blk = sq[:, c0:c0 + LANES]
                first = jnp.sum(jnp.where(lane_h < HEAD_DIM, blk, 0.0), axis=-1, keepdims=True)
                both = jnp.sum(blk, axis=-1, keepdims=True)
                ssq.append(jnp.where(lane_h < HEAD_DIM, first, both - first))
            ssq = jnp.concatenate(ssq, axis=1)
            yield sub, x * lax.rsqrt(ssq * (1.0 / HEAD_DIM) + EPS) * gain_row

    tile_rows = slice(0, TILE)
    qf = in_proj(tile_rows, 0, attn_w)
    for rb in range(0, TILE, QBLK):
        for g in range(n_groups):
            cols = slice(g * GROUP_W, (g + 1) * GROUP_W)
            gain = gain_s[0, 0:1, cols]
            for sub, qn in head_rms(qf[rb:rb + QBLK, cols], gain):
                for h in range(HEADS_PER_GROUP):
                    keep = (lane_g >= h * HEAD_DIM) & (lane_g < (h + 1) * HEAD_DIM)
                    qbd_s[rb // QBLK, g, h * QBLK + sub:h * QBLK + sub + SUB_ROWS, :] = (
                        jnp.where(keep, qn, 0.0).astype(BF16))
    kf = in_proj(tile_rows, o_k, attn_w)
    for rb in range(0, TILE, QBLK):
        for g in range(n_groups):
            cols = slice(g * GROUP_W, (g + 1) * GROUP_W)
            for sub, kn in head_rms(kf[rb:rb + QBLK, cols], gain_s[1, 0:1, cols]):
                row = TILE + rb + sub
                k_s[row:row + SUB_ROWS, cols] = kn.astype(BF16)
    vt_s[:, TILE:2 * TILE] = lax.dot_general(
        wvt_s[...], h_s[...], (((1,), (1,)), ((), ())), preferred_element_type=F32).astype(BF16)
    u_s[POOL_OFF:POOL_OFF + TILE, :] = in_proj(tile_rows, o_u, pool_w)
    for c0 in range(0, attn_w + pool_w, 512):
        z_s[:, c0:c0 + 512] = _silu(in_proj(tile_rows, o_z + c0, 512)).astype(BF16)

    def pool():
        step = 176
        levels = ((u_s, t2_s, 1), (t2_s, t4_s, 2), (t4_s, t8_s, 4), (t8_s, None, 8))
        for gi in range(len(POOL_WINDOWS)):
            assert POOL_WINDOWS[gi] == 2 ** (gi + 1)
        for li, (src, dst, sh) in enumerate(levels[:-1]):
            lanes = slice((li + 1) * pool_group, pool_w)
            for r in range(POOL_PAD, pool_rows, step):
                dst[r:r + step, lanes] = src[r:r + step, lanes] + src[r - sh:r - sh + step, lanes]
        tok = t * TILE + lax.broadcasted_iota(jnp.int32, (QBLK, pool_group), 0)
        for r in range(0, TILE, QBLK):
            for pair in range(len(POOL_WINDOWS) // 2):
                mixed = []
                for gi in (2 * pair, 2 * pair + 1):
                    lanes = slice(gi * pool_group, (gi + 1) * pool_group)
                    src = levels[gi][0]
                    sh = levels[gi][2]
                    rows = slice(POOL_OFF + r, POOL_OFF + r + QBLK)
                    wsum = src[rows, lanes] + src[POOL_OFF + r - sh:POOL_OFF + r - sh + QBLK, lanes]
                    cnt = jnp.minimum(tok + (r + 1), POOL_WINDOWS[gi]).astype(F32)
                    mixed.append((wsum / cnt - u_s[rows, lanes]).astype(BF16))
                pm = jnp.concatenate(mixed, axis=1)
                c0 = pair * 2 * pool_group
                po = jnp.dot(pm, wp_s[pair], preferred_element_type=F32) * ps_ref[:, c0:c0 + 2 * pool_group]
                gz = z_s[r:r + QBLK, attn_w + c0:attn_w + c0 + 2 * pool_group].astype(F32)
                y_s[r:r + QBLK, attn_w + c0:attn_w + c0 + 2 * pool_group] = (po * gz).astype(BF16)

    row_q = lax.broadcasted_iota(jnp.int32, (QBLK, LANES), 0)
    n_qblk = TILE // QBLK

    def band_blocks(jb):
        n_before = TILE // LANES - jb
        return [jnp.where(t == 0, BAND_BLOCKS, cb) if cb < n_before else cb for cb in range(BAND_BLOCKS)]

    def score_dot(jb, g):
        qrow = jb * QBLK
        return lax.dot_general(k_s[qrow:qrow + BANDW, g * GROUP_W:(g + 1) * GROUP_W], qbd_s[jb, g],
                               (((1,), (1,)), ((), ())), preferred_element_type=F32)

    def probs(jb):
        buf = jb % 2
        blk = band_blocks(jb)
        for g in range(n_groups):
            st = score_dot(jb, g)
            for h in range(HEADS_PER_GROUP):
                hh = g * HEADS_PER_GROUP + h
                lanes = slice(h * QBLK, (h + 1) * QBLK)
                for cb in range(BAND_BLOCKS):
                    for sub in range(0, LANES, SUB_ROWS):
                        rows = slice(cb * LANES + sub, cb * LANES + sub + SUB_ROWS)
                        p = jnp.exp2(st[rows, lanes] + bias_ref[hh, blk[cb], sub:sub + SUB_ROWS, :])
                        p_s[buf, hh // 2, rows, (hh % 2) * QBLK:(hh % 2 + 1) * QBLK] = p.astype(BF16)

    def probs_exact(jb):
        buf = jb % 2
        blk = band_blocks(jb)
        for g in range(n_groups):
            st = score_dot(jb, g)
            for h in range(HEADS_PER_GROUP):
                hh = g * HEADS_PER_GROUP + h
                lanes = slice(h * QBLK, (h + 1) * QBLK)
                macc = None
                for cb in range(BAND_BLOCKS):
                    for sub in range(0, LANES, SUB_ROWS):
                        rows = slice(cb * LANES + sub, cb * LANES + sub + SUB_ROWS)
                        sb = st[rows, lanes] + biasx_ref[hh, blk[cb], sub:sub + SUB_ROWS, :]
                        s_s[hh, rows, :] = sb
                        for r8 in range(0, SUB_ROWS, 8):
                            macc = sb[r8:r8 + 8, :] if macc is None else jnp.maximum(macc, sb[r8:r8 + 8, :])
                m = jnp.max(macc, axis=0, keepdims=True)
                for r in range(0, BANDW, SUB_ROWS):
                    rows = slice(r, r + SUB_ROWS)
                    p = jnp.exp2(s_s[hh, rows, :] - m)
                    p_s[buf, hh // 2, rows, (hh % 2) * QBLK:(hh % 2 + 1) * QBLK] = p.astype(BF16)

    ones_rows = jnp.ones((ONES_ROWS, BANDW), BF16)

    def attend(jb, anchor=None):
        buf = jb % 2
        qrow = jb * QBLK
        lmin = None
        for pair in range(n_heads // 2):
            ha = 2 * pair
            vt1 = jnp.concatenate(
                [vt_s[ha * HEAD_DIM:(ha + 2) * HEAD_DIM, qrow:qrow + BANDW], ones_rows], axis=0)
            ot = jnp.dot(vt1, p_s[buf, pair], preferred_element_type=F32)
            l = ot[2 * HEAD_DIM:2 * HEAD_DIM + 1, :]
            lmin = l if lmin is None else jnp.minimum(lmin, l)
            linv = 1.0 / l
            if anchor is not None:
                linv = jnp.concatenate([jnp.where(never, anchor(pair), linv[:, 0:LANES]), linv[:, LANES:]], axis=1)
            oa = ot[0:2 * HEAD_DIM, 0:QBLK] * linv[:, 0:QBLK]
            ob = ot[0:2 * HEAD_DIM, QBLK:2 * QBLK] * linv[:, QBLK:2 * QBLK]
            a = jnp.where(row_q < HEAD_DIM, oa, ob).T
            c0 = ha * HEAD_DIM
            gz = z_s[qrow:qrow + QBLK, c0:c0 + LANES].astype(F32)
            y_s[qrow:qrow + QBLK, c0:c0 + LANES] = (a * gz).astype(BF16)
        return jnp.min(lmin)

    never = t < 0
    norm_steps = list(range(0, TILE, NORM_ROWS))
    per_pair = len(norm_steps) // (n_qblk * (n_heads // 2))

    def next_norm(jb):
        def anchor(pair):
            i0 = (jb * (n_heads // 2) + pair) * per_pair
            return sum(norm_rows(xn_ref, seq_next, rn) for rn in norm_steps[i0:i0 + per_pair])
        return anchor

    pool()
    probs(0)
    underflowed = []
    for jb in range(n_qblk):
        if jb + 1 < n_qblk:
            probs(jb + 1)
        underflowed.append(jnp.logical_not(attend(jb, next_norm(jb)) >= L_FLOOR))

    o = jnp.dot(y_s[...], wout_s[...], preferred_element_type=F32)
    for c0 in range(0, d_model, GROUP_W):
        cols = slice(c0, c0 + GROUP_W)
        for r in range(0, TILE, SUB_ROWS):
            o_ref[r:r + SUB_ROWS, cols] = x_ref[r:r + SUB_ROWS, cols] + gate[:, cols] * o[r:r + SUB_ROWS, cols]

    @pl.when(functools.reduce(jnp.logical_or, underflowed))
    def _():
        for jb in range(n_qblk):
            @pl.when(underflowed[jb])
            def _(jb=jb):
                probs_exact(jb)
                attend(jb)
                rows = slice(jb * QBLK, (jb + 1) * QBLK)
                o_jb = jnp.dot(y_s[rows, :], wout_s[...], preferred_element_type=F32)
                o_ref[rows, :] = x_ref[rows, :] + gate * o_jb


def _layer(x, c, w_ada, b_ada, norm_g, w_in, q_norm_g, k_norm_g, rel_bias, w_pool, pool_scale, w_out):
    bsz, seq, d_model = x.shape
    n_heads = rel_bias.shape[0]
    attn_w = n_heads * HEAD_DIM
    pool_w = w_pool.shape[0] * w_pool.shape[1]
    pool_group = w_pool.shape[1]
    in_w = w_in.shape[1]
    assert seq % TILE == 0 and attn_w % GROUP_W == 0 and TILE == N_LEFT_CHUNKS * CHUNK
    assert in_w == 3 * attn_w + pool_w + attn_w + pool_w and 2 * pool_group == GROUP_W
    assert attn_w == 512 and pool_w == 512 and bsz <= 8
    n_groups = attn_w // GROUP_W

    n_pairs = w_pool.shape[0] // 2
    assert QBLK <= REL_CLIP <= TILE and rel_bias.shape[1] == 2 * REL_CLIP + 1

    tiles_per_seq = seq // TILE

    def next_tile(b, t):
        n = jnp.minimum(b * tiles_per_seq + t + 1, bsz * tiles_per_seq - 1)
        return n // tiles_per_seq, n % tiles_per_seq

    const2 = lambda b, t: (0, 0)
    const3 = lambda b, t: (0, 0, 0)
    kernel = functools.partial(_block_kernel, d_model=d_model, attn_w=attn_w, pool_w=pool_w,
                               tiles_per_seq=tiles_per_seq, n_tiles=bsz * tiles_per_seq)
    return pl.pallas_call(
        kernel,
        grid=(bsz, seq // TILE),
        in_specs=[
            pl.BlockSpec((None, TILE, d_model), lambda b, t: (b, t, 0)),
            pl.BlockSpec((None, TILE, d_model), lambda b, t: (*next_tile(b, t), 0)),
            pl.BlockSpec((bsz, d_model), const2),
            pl.BlockSpec(memory_space=pl.ANY),
            pl.BlockSpec((1, 3 * d_model), const2),
            pl.BlockSpec((1, d_model), const2),
            pl.BlockSpec(memory_space=pl.ANY),
            pl.BlockSpec(memory_space=pl.ANY),
            pl.BlockSpec((1, HEAD_DIM), const2),
            pl.BlockSpec((1, HEAD_DIM), const2),
            pl.BlockSpec((n_heads, 2 * REL_CLIP + 1), const2),
            pl.BlockSpec((2 * n_pairs, pool_group, pool_group), const3),
            pl.BlockSpec((1, pool_w), const2),
        ],
        out_specs=pl.BlockSpec((None, TILE, d_model), lambda b, t: (b, t, 0)),
        out_shape=jax.ShapeDtypeStruct(x.shape, x.dtype),
        scratch_shapes=[
            pltpu.VMEM((TILE, d_model), BF16),
            pltpu.VMEM((TILE, attn_w + pool_w), BF16),
            pltpu.VMEM((TILE // QBLK, n_groups, HEADS_PER_GROUP * QBLK, GROUP_W), BF16),
            pltpu.VMEM((2 * TILE, attn_w), BF16),
            pltpu.VMEM((attn_w, 2 * TILE), BF16),
            pltpu.VMEM((TILE + POOL_OFF, pool_w), F32),
            pltpu.VMEM((TILE + POOL_OFF, pool_w), F32),
            pltpu.VMEM((TILE + POOL_OFF, pool_w), F32),
            pltpu.VMEM((TILE + POOL_OFF, pool_w), F32),
            pltpu.VMEM((n_heads, BANDW, QBLK), F32),
            pltpu.VMEM((2, n_heads // 2, BANDW, 2 * QBLK), BF16),
            pltpu.VMEM((TILE, attn_w + pool_w), BF16),
            pltpu.VMEM((n_heads, BAND_BLOCKS + 1, QBLK, LANES), F32),
            pltpu.VMEM((n_heads, BAND_BLOCKS + 1, QBLK, LANES), F32),
            pltpu.VMEM((attn_w, d_model), BF16),
            pltpu.VMEM((2, 8, attn_w), F32),
            pltpu.VMEM((n_pairs, GROUP_W, GROUP_W), BF16),
            pltpu.VMEM((d_model, in_w), BF16),
            pltpu.VMEM((attn_w + pool_w, d_model), BF16),
            pltpu.VMEM((8, 3 * d_model), F32),
            pltpu.VMEM((2, STAGE_ROWS, in_w), F32),
            pltpu.SemaphoreType.DMA((2,)),
        ],
        compiler_params=pltpu.CompilerParams(
            dimension_semantics=("arbitrary", "arbitrary"),
            vmem_limit_bytes=V7X_VMEM_LIMIT_BYTES,
        ),
        name="hybrid_block",
    )(x, x, c, w_ada, b_ada.reshape(1, 3 * d_model), norm_g.reshape(1, d_model), w_in, w_out,
      q_norm_g.reshape(1, HEAD_DIM), k_norm_g.reshape(1, HEAD_DIM), rel_bias, w_pool,
      pool_scale.reshape(1, pool_w))


def kernel(x, c, norm_g, w_ada, b_ada, w_in, q_norm_g, k_norm_g, rel_bias, w_pool, pool_scale, w_out):
    depth = w_in.shape[0]
    for l in range(depth):
        x = _layer(x, c, w_ada[l], b_ada[l], norm_g[l], w_in[l], q_norm_g[l], k_norm_g[l], rel_bias[l],
                   w_pool[l], pool_scale[l], w_out[l])
    return x
```
